```python
import math
import jax, jax.numpy as jnp
from jax import lax
import numpy as np

D_MODEL = 1024
BATCH = 8
SEQ = 4096
DEPTH = 2
DEC_BATCH = 32
DEC_SEQ = 2048
PAST_LEN = 128

RET_HEADS = 8
RET_DK = 64
RET_DV = 64
RET_WIDTH = RET_HEADS * RET_DV
RET_QK = RET_HEADS * RET_DK
DIFF_HEADS = 4
DIFF_DH = 64
DIFF_DV = 2 * DIFF_DH
DIFF_WIDTH = DIFF_HEADS * DIFF_DV
DIFF_QK = DIFF_HEADS * 2 * DIFF_DH
MIX_WIDTH = RET_WIDTH + DIFF_WIDTH
IN_WIDTH = 2 * RET_QK + 2 * RET_WIDTH + 2 * DIFF_QK + DIFF_WIDTH
D_FF = 2816
CHUNK = 128
Q_BLOCK = 128
EPS = 1e-5
DEEPNORM_ALPHA = (2 * DEPTH) ** 0.25
DEEPNORM_BETA = (8 * DEPTH) ** -0.25

kernel_name = "hymba_retnet_diffattn_macaron_encoder"


def layer_norm(x, g, b):
    xf = x.astype(jnp.float32)
    mu = jnp.mean(xf, axis=-1, keepdims=True)
    var = jnp.mean(jnp.square(xf - mu), axis=-1, keepdims=True)
    y = (xf - mu) * lax.rsqrt(var + EPS) * g.astype(jnp.float32) + b.astype(jnp.float32)
    return y.astype(x.dtype)


def swiglu_ffn(x, w13, w2):
    a, b = jnp.split(x @ w13, 2, axis=-1)
    return (jax.nn.silu(a) * b) @ w2


def lambda_init(layer):
    return 0.8 - 0.6 * math.exp(-0.3 * layer)


def alibi_slopes():
    return jnp.asarray([2.0 ** (-8.0 * (h + 1) / DIFF_HEADS) for h in range(DIFF_HEADS)], jnp.float32)


def retention_direction(q, k, v, log_gamma, strict):
    B, H, T, dk = q.shape
    dv = v.shape[-1]
    n = T // CHUNK
    dt = q.dtype
    idx = jnp.arange(CHUNK, dtype=jnp.float32)
    dist = idx[:, None] - idx[None, :]
    mask = dist > 0 if strict else dist >= 0
    intra = jnp.where(mask[None], jnp.exp(log_gamma[:, None, None] * jnp.maximum(dist, 0.0)[None]), 0.0)
    q_decay = jnp.exp(log_gamma[:, None] * (idx + 1.0)[None])
    k_decay = jnp.exp(log_gamma[:, None] * (CHUNK - 1.0 - idx)[None])
    chunk_decay = jnp.exp(log_gamma * CHUNK)
    intra_d = intra.astype(dt)
    q_decay_d = q_decay[None, :, :, None].astype(dt)
    k_decay_d = k_decay[None, :, :, None].astype(dt)

    def to_chunks(a):
        return jnp.moveaxis(a.reshape(B, H, n, CHUNK, a.shape[-1]), 2, 0)

    def step(state, xs):
        qi, ki, vi = xs
        s = jnp.einsum('bhid,bhjd->bhij', qi, ki) * intra_d
        inner = jnp.einsum('bhij,bhjv->bhiv', s, vi)
        cross = jnp.einsum('bhid,bhdv->bhiv', qi * q_decay_d, state.astype(dt))
        new_state = state * chunk_decay[None, :, None, None] + jnp.einsum(
            'bhjd,bhjv->bhdv', ki * k_decay_d, vi).astype(jnp.float32)
        return new_state, inner + cross

    state0 = jnp.zeros((B, H, dk, dv), jnp.float32)
    _, out = lax.scan(step, state0, (to_chunks(q), to_chunks(k), to_chunks(v)))
    return jnp.moveaxis(out, 0, 2).reshape(B, H, T, dv)


def bidirectional_retention(q, k, v, decay_f, decay_b):
    lg_f = jax.nn.log_sigmoid(decay_f.astype(jnp.float32))
    lg_b = jax.nn.log_sigmoid(decay_b.astype(jnp.float32))
    fwd = retention_direction(q, k, v, lg_f, strict=False)
    bwd = retention_direction(jnp.flip(q, 2), jnp.flip(k, 2), jnp.flip(v, 2), lg_b, strict=True)
    return fwd + jnp.flip(bwd, 2)


def head_group_norm(o, g, b):
    B, H, T, dv = o.shape
    of = o.astype(jnp.float32)
    mu = jnp.mean(of, axis=-1, keepdims=True)
    var = jnp.mean(jnp.square(of - mu), axis=-1, keepdims=True)
    y = ((of - mu) * lax.rsqrt(var + EPS)).transpose(0, 2, 1, 3).reshape(B, T, H * dv)
    return (y * g.astype(jnp.float32) + b.astype(jnp.float32)).astype(o.dtype)


def differential_attention(q, k, v, lam, slopes):
    B, H, _, T, d = q.shape
    nq = T // Q_BLOCK
    scale = d ** -0.5
    qb = jnp.moveaxis(q.reshape(B, H, 2, nq, Q_BLOCK, d), 3, 0)
    starts = jnp.arange(nq, dtype=jnp.float32) * Q_BLOCK
    kpos = jnp.arange(T, dtype=jnp.float32)

    def block(xs):
        qi, start = xs
        s = jnp.einsum('bhmqd,bhmkd->bhmqk', qi, k).astype(jnp.float32) * scale
        qpos = start + jnp.arange(Q_BLOCK, dtype=jnp.float32)
        bias = -slopes[:, None, None] * jnp.abs(qpos[:, None] - kpos[None, :])[None]
        p = jax.nn.softmax(s + bias[None, :, None], axis=-1)
        a = p[:, :, 0] - lam * p[:, :, 1]
        return jnp.einsum('bhqk,bhkv->bhqv', a.astype(v.dtype), v)

    out = lax.map(block, (qb, starts))
    return jnp.moveaxis(out, 0, 2).reshape(B, H, T, v.shape[-1])


def hybrid_mixer(h, layer, w_in, w_out, decay_f, decay_b, gn_g, gn_b, lq1, lk1, lq2, lk2, subln_g):
    B, T, _ = h.shape
    splits = [RET_QK, 2 * RET_QK, 2 * RET_QK + RET_WIDTH, 2 * RET_QK + 2 * RET_WIDTH,
              2 * RET_QK + 2 * RET_WIDTH + DIFF_QK, 2 * RET_QK + 2 * RET_WIDTH + 2 * DIFF_QK]
    rq, rk, rv, rg, dq, dk, dv = jnp.split(h @ w_in, splits, axis=-1)

    rq = rq.reshape(B, T, RET_HEADS, RET_DK).transpose(0, 2, 1, 3)
    rk = rk.reshape(B, T, RET_HEADS, RET_DK).transpose(0, 2, 1, 3) * (RET_DK ** -0.5)
    rv = rv.reshape(B, T, RET_HEADS, RET_DV).transpose(0, 2, 1, 3)
    ret = bidirectional_retention(rq, rk, rv, decay_f, decay_b)
    ret_out = head_group_norm(ret, gn_g, gn_b) * jax.nn.silu(rg)

    dq = dq.reshape(B, T, DIFF_HEADS, 2, DIFF_DH).transpose(0, 2, 3, 1, 4)
    dk = dk.reshape(B, T, DIFF_HEADS, 2, DIFF_DH).transpose(0, 2, 3, 1, 4)
    dv = dv.reshape(B, T, DIFF_HEADS, DIFF_DV).transpose(0, 2, 1, 3)
    lam0 = lambda_init(layer)
    f32 = jnp.float32
    lam = (jnp.exp(jnp.sum(lq1.astype(f32) * lk1.astype(f32)))
           - jnp.exp(jnp.sum(lq2.astype(f32) * lk2.astype(f32))) + lam0)
    att = differential_attention(dq, dk, dv, lam, alibi_slopes()).astype(f32)
    att = att * lax.rsqrt(jnp.mean(jnp.square(att), axis=-1, keepdims=True) + EPS)
    att = att * subln_g.astype(f32) * (1.0 - lam0)
    diff_out = att.transpose(0, 2, 1, 3).reshape(B, T, DIFF_WIDTH).astype(h.dtype)

    return jnp.concatenate([ret_out, diff_out], axis=-1) @ w_out


def encoder_trunk(x, w_in, w_out, ret_decay_f, ret_decay_b, ret_gn_g, ret_gn_b,
                  diff_lq1, diff_lk1, diff_lq2, diff_lk2, diff_subln_g,
                  ffn1_w13, ffn1_w2, ffn2_w13, ffn2_w2,
                  ln1_g, ln1_b, ln2_g, ln2_b, ln3_g, ln3_b):
    for l in range(DEPTH):
        x = layer_norm(DEEPNORM_ALPHA * x + 0.5 * swiglu_ffn(x, ffn1_w13[l], ffn1_w2[l]), ln1_g[l], ln1_b[l])
        mix = hybrid_mixer(x, l, w_in[l], w_out[l], ret_decay_f[l], ret_decay_b[l], ret_gn_g[l], ret_gn_b[l],
                           diff_lq1[l], diff_lk1[l], diff_lq2[l], diff_lk2[l], diff_subln_g[l])
        x = layer_norm(DEEPNORM_ALPHA * x + mix, ln2_g[l], ln2_b[l])
        x = layer_norm(DEEPNORM_ALPHA * x + 0.5 * swiglu_ffn(x, ffn2_w13[l], ffn2_w2[l]), ln3_g[l], ln3_b[l])
    return x


def setup_inputs(seed: int = 0) -> dict:
    key = jax.random.key(seed)
    ks = jax.random.split(key, 32)
    f32 = jnp.float32
    nrm = lambda k, shape, s: jax.random.normal(k, shape, f32) * s
    col_scale = jnp.concatenate([
        jnp.ones((2 * RET_QK,), f32), jnp.full((RET_WIDTH,), DEEPNORM_BETA, f32),
        jnp.ones((RET_WIDTH + 2 * DIFF_QK,), f32), jnp.full((DIFF_WIDTH,), DEEPNORM_BETA, f32)])
    base_decay = jnp.asarray(np.log(2.0 ** (5.0 + np.arange(RET_HEADS)) - 1.0), f32)
    return {
        "x_prompt": jax.random.normal(ks[0], (BATCH, SEQ, D_MODEL), f32),
        "x_sample": jax.random.normal(ks[1], (DEC_BATCH, DEC_SEQ, D_MODEL), f32),
        "w_in": nrm(ks[2], (DEPTH, D_MODEL, IN_WIDTH), D_MODEL ** -0.5) * col_scale,
        "w_out": nrm(ks[3], (DEPTH, MIX_WIDTH, D_MODEL), DEEPNORM_BETA * MIX_WIDTH ** -0.5),
        "ret_decay_f": base_decay + nrm(ks[4], (DEPTH, RET_HEADS), 0.05),
        "ret_decay_b": base_decay + nrm(ks[5], (DEPTH, RET_HEADS), 0.05),
        "ret_gn_g": 1.0 + nrm(ks[6], (DEPTH, RET_WIDTH), 0.02),
        "ret_gn_b": nrm(ks[7], (DEPTH, RET_WIDTH), 0.02),
        "diff_lq1": nrm(ks[8], (DEPTH, DIFF_DH), 0.1),
        "diff_lk1": nrm(ks[9], (DEPTH, DIFF_DH), 0.1),
        "diff_lq2": nrm(ks[10], (DEPTH, DIFF_DH), 0.1),
        "diff_lk2": nrm(ks[11], (DEPTH, DIFF_DH), 0.1),
        "diff_subln_g": 1.0 + nrm(ks[12], (DEPTH, DIFF_DV), 0.02),
        "ffn1_w13": nrm(ks[13], (DEPTH, D_MODEL, 2 * D_FF), D_MODEL ** -0.5),
        "ffn1_w2": nrm(ks[14], (DEPTH, D_FF, D_MODEL), DEEPNORM_BETA * D_FF ** -0.5),
        "ffn2_w13": nrm(ks[15], (DEPTH, D_MODEL, 2 * D_FF), D_MODEL ** -0.5),
        "ffn2_w2": nrm(ks[16], (DEPTH, D_FF, D_MODEL), DEEPNORM_BETA * D_FF ** -0.5),
        "ln1_g": 1.0 + nrm(ks[17], (DEPTH, D_MODEL), 0.02),
        "ln1_b": nrm(ks[18], (DEPTH, D_MODEL), 0.02),
        "ln2_g": 1.0 + nrm(ks[19], (DEPTH, D_MODEL), 0.02),
        "ln2_b": nrm(ks[20], (DEPTH, D_MODEL), 0.02),
        "ln3_g": 1.0 + nrm(ks[21], (DEPTH, D_MODEL), 0.02),
        "ln3_b": nrm(ks[22], (DEPTH, D_MODEL), 0.02),
    }


def reference(x_prompt, x_sample, w_in, w_out, ret_decay_f, ret_decay_b, ret_gn_g, ret_gn_b,
              diff_lq1, diff_lk1, diff_lq2, diff_lk2, diff_subln_g,
              ffn1_w13, ffn1_w2, ffn2_w13, ffn2_w2,
              ln1_g, ln1_b, ln2_g, ln2_b, ln3_g, ln3_b):
    y_prompt = encoder_trunk(x_prompt, w_in, w_out, ret_decay_f, ret_decay_b, ret_gn_g, ret_gn_b,
                             diff_lq1, diff_lk1, diff_lq2, diff_lk2, diff_subln_g,
                             ffn1_w13, ffn1_w2, ffn2_w13, ffn2_w2,
                             ln1_g, ln1_b, ln2_g, ln2_b, ln3_g, ln3_b)
    y_sample = encoder_trunk(x_sample, w_in, w_out, ret_decay_f, ret_decay_b, ret_gn_g, ret_gn_b,
                             diff_lq1, diff_lk1, diff_lq2, diff_lk2, diff_subln_g,
                             ffn1_w13, ffn1_w2, ffn2_w13, ffn2_w2,
                             ln1_g, ln1_b, ln2_g, ln2_b, ln3_g, ln3_b)
    return (y_prompt, y_sample)
```

```python
import functools
import math

import jax
import jax.numpy as jnp
from jax import lax
from jax.experimental import pallas as pl
from jax.experimental.pallas import tpu as pltpu

D_MODEL = 1024
DEPTH = 2
RET_HEADS = 8
RET_DK = 64
RET_WIDTH = 512
DIFF_HEADS = 4
DIFF_DH = 64
DIFF_DV = 128
DIFF_WIDTH = 512
IN_WIDTH = 3584
D_FF = 2816
CHUNK = 128
EPS = 1e-5
ALPHA = (2 * DEPTH) ** 0.25

LANES = 128
FF_CHUNK = 1408
ROW_TILE = 512
Q_TILE = 256
KV_TILE = 512
VMEM_LIMIT = 56 * 1024 * 1024

RQ_BLK, RK_BLK, RV_BLK, RG_BLK, DQ_BLK, DK_BLK, DV_BLK = 0, 4, 8, 12, 16, 20, 24

BF16 = jnp.bfloat16
F32 = jnp.float32
NT_DIMS = (((1,), (1,)), ((), ()))


def _resident(shape):
    nd = len(shape)
    return pl.BlockSpec(shape, lambda *_: (0,) * nd, pipeline_mode=pl.Buffered(1))


def _layer_norm(y, g, b):
    mu = jnp.mean(y, axis=-1, keepdims=True)
    d = y - mu
    var = jnp.mean(d * d, axis=-1, keepdims=True)
    return d * lax.rsqrt(var + EPS) * g + b


def _params(*sem):
    return pltpu.CompilerParams(dimension_semantics=sem, vmem_limit_bytes=VMEM_LIMIT)


def _ffn_ln_kernel(x_ref, w13_ref, w2_ref, g_ref, b_ref, o_ref):
    x = x_ref[...]
    xb = x.astype(BF16)
    acc = None
    for lo in range(0, D_FF, FF_CHUNK):
        a = jnp.dot(xb, w13_ref[:, lo:lo + FF_CHUNK], preferred_element_type=F32)
        b = jnp.dot(xb, w13_ref[:, D_FF + lo:D_FF + lo + FF_CHUNK], preferred_element_type=F32)
        h = (jax.nn.silu(a) * b).astype(BF16)
        part = jnp.dot(h, w2_ref[lo:lo + FF_CHUNK, :], preferred_element_type=F32)
        acc = part if acc is None else acc + part
    o_ref[...] = _layer_norm(ALPHA * x + 0.5 * acc, g_ref[...], b_ref[...])


def _ffn_ln(x, w13, w2, g, b):
    m = x.shape[0]
    row = pl.BlockSpec((ROW_TILE, D_MODEL), lambda i: (i, 0))
    return pl.pallas_call(
        _ffn_ln_kernel,
        grid=(m // ROW_TILE,),
        in_specs=[row, _resident(w13.shape), _resident(w2.shape),
                  _resident((1, D_MODEL)), _resident((1, D_MODEL))],
        out_specs=row,
        out_shape=jax.ShapeDtypeStruct((m, D_MODEL), F32),
        compiler_params=_params("parallel"),
        name="ffn_ln",
    )(x, w13, w2, g, b)


def _inproj_kernel(x_ref, w_ref, o_ref, gate_ref):
    y = jnp.dot(x_ref[...].astype(BF16), w_ref[...], preferred_element_type=F32)
    o_ref[...] = y.astype(BF16)
    gate_ref[...] = y[:, RG_BLK * LANES:RG_BLK * LANES + RET_WIDTH]


def _inproj(x, w):
    m = x.shape[0]
    return pl.pallas_call(
        _inproj_kernel,
        grid=(m // ROW_TILE,),
        in_specs=[pl.BlockSpec((ROW_TILE, D_MODEL), lambda i: (i, 0)), _resident(w.shape)],
        out_specs=[pl.BlockSpec((ROW_TILE, IN_WIDTH), lambda i: (i, 0)),
                   pl.BlockSpec((ROW_TILE, RET_WIDTH), lambda i: (i, 0))],
        out_shape=[jax.ShapeDtypeStruct((m, IN_WIDTH), BF16),
                   jax.ShapeDtypeStruct((m, RET_WIDTH), F32)],
        compiler_params=_params("parallel"),
        name="inproj",
    )(x, w)


def _ret_kernel(q_ref, k_ref, v_ref, gate_ref, dec_ref, gng_ref, gnb_ref, o_ref,
                sf_ref, sb_ref, *, n_chunks):
    c_len = CHUNK
    k_scale = RET_DK ** -0.5
    lane = lax.broadcasted_iota(jnp.int32, (1, LANES), 1)
    lo_half = lane < RET_DK
    m_lo = lo_half.astype(F32)
    m_hi = 1.0 - m_lo
    lg = jax.nn.log_sigmoid(dec_ref[0])
    lgf0, lgf1, lgb0, lgb1 = lg[0:1], lg[1:2], lg[2:3], lg[3:4]
    lgf = jnp.where(lo_half, lgf0, lgf1)
    lgb = jnp.where(lo_half, lgb0, lgb1)

    ri = lax.broadcasted_iota(jnp.int32, (c_len, c_len), 0)
    ci = lax.broadcasted_iota(jnp.int32, (c_len, c_len), 1)
    dist = (ri - ci).astype(F32)
    causal = dist >= 0.0
    block_diag = (ri < RET_DK) == (ci < RET_DK)

    def decay_tile(lf, lb):
        return jnp.where(causal, jnp.exp(lf * jnp.maximum(dist, 0.0)),
                         jnp.exp(lb * jnp.maximum(-dist, 0.0))) * k_scale

    d0 = decay_tile(lgf0, lgb0)
    d1 = decay_tile(lgf1, lgb1)
    row = lax.broadcasted_iota(jnp.int32, (c_len, LANES), 0).astype(F32)
    qdf = jnp.exp(lgf * (row + 1.0))
    qdb = jnp.exp(lgb * (c_len - row))
    kdf = jnp.exp(lgf * (c_len - 1.0 - row)) * k_scale
    kdb = jnp.exp(lgb * row) * k_scale
    gcf = jnp.exp(lgf * c_len)
    gcb = jnp.exp(lgb * c_len)

    def rows_of(c):
        return pl.ds(pl.multiple_of(c * c_len, c_len), c_len)

    def kv_body(c, carry):
        rows = rows_of(c)
        k = k_ref[0, rows, :].astype(F32)
        v = v_ref[0, rows, :]
        kvf = jnp.dot((k * kdf).T.astype(BF16), v, preferred_element_type=F32)
        kvb = jnp.dot((k * kdb).T.astype(BF16), v, preferred_element_type=F32)
        sf_ref[c] = jnp.where(block_diag, kvf, 0.0)
        sb_ref[c] = jnp.where(block_diag, kvb, 0.0)
        return carry

    lax.fori_loop(0, n_chunks, kv_body, 0)

    def fwd_scan(c, s):
        kv = sf_ref[c]
        sf_ref[c] = s
        return s * gcf + kv

    lax.fori_loop(0, n_chunks, fwd_scan, jnp.zeros((c_len, LANES), F32))

    def bwd_scan(i, s):
        c = n_chunks - 1 - i
        kv = sb_ref[c]
        sb_ref[c] = s
        return s * gcb + kv

    lax.fori_loop(0, n_chunks, bwd_scan, jnp.zeros((c_len, LANES), F32))

    gng = gng_ref[...]
    gnb = gnb_ref[...]
    inv_dv = 1.0 / RET_DK

    def out_body(c, carry):
        rows = rows_of(c)
        q = q_ref[0, rows, :].astype(F32)
        k = k_ref[0, rows, :]
        v = v_ref[0, rows, :].astype(F32)
        s0 = lax.dot_general((q * m_lo).astype(BF16), k, NT_DIMS, preferred_element_type=F32)
        s1 = lax.dot_general((q * m_hi).astype(BF16), k, NT_DIMS, preferred_element_type=F32)
        o = jnp.dot((s0 * d0).astype(BF16), (v * m_lo).astype(BF16), preferred_element_type=F32)
        o = o + jnp.dot((s1 * d1).astype(BF16), (v * m_hi).astype(BF16), preferred_element_type=F32)
        o = o + jnp.dot((q * qdf).astype(BF16), sf_ref[c].astype(BF16), preferred_element_type=F32)
        o = o + jnp.dot((q * qdb).astype(BF16), sb_ref[c].astype(BF16), preferred_element_type=F32)
        s_lo = jnp.sum(o * m_lo, axis=-1, keepdims=True)
        s_hi = jnp.sum(o * m_hi, axis=-1, keepdims=True)
        d = o - jnp.where(lo_half, s_lo, s_hi) * inv_dv
        dd = d * d
        v_lo = jnp.sum(dd * m_lo, axis=-1, keepdims=True)
        v_hi = jnp.sum(dd * m_hi, axis=-1, keepdims=True)
        var = jnp.where(lo_half, v_lo, v_hi) * inv_dv
        y = d * lax.rsqrt(var + EPS) * gng + gnb
        o_ref[0, rows, :] = (y * jax.nn.silu(gate_ref[0, rows, :])).astype(BF16)
        return carry

    lax.fori_loop(0, n_chunks, out_body, 0)


def _retention(qkv, gate, dec, gng, gnb):
    bsz, t, _ = qkv.shape
    n_chunks = t // CHUNK
    n_pairs = RET_WIDTH // LANES

    def col(blk):
        return pl.BlockSpec((1, t, LANES), lambda b, p: (b, 0, blk + p))

    return pl.pallas_call(
        functools.partial(_ret_kernel, n_chunks=n_chunks),
        grid=(bsz, n_pairs),
        in_specs=[col(RQ_BLK), col(RK_BLK), col(RV_BLK),
                  pl.BlockSpec((1, t, LANES), lambda b, p: (b, 0, p)),
                  pl.BlockSpec((1, 4, LANES), lambda b, p: (p, 0, 0)),
                  pl.BlockSpec((1, LANES), lambda b, p: (0, p)),
                  pl.BlockSpec((1, LANES), lambda b, p: (0, p))],
        out_specs=pl.BlockSpec((1, t, LANES), lambda b, p: (b, 0, p)),
        out_shape=jax.ShapeDtypeStruct((bsz, t, RET_WIDTH), BF16),
        scratch_shapes=[pltpu.VMEM((n_chunks, CHUNK, LANES), F32),
                        pltpu.VMEM((n_chunks, CHUNK, LANES), F32)],
        compiler_params=_params("parallel", "parallel"),
        name="retention",
    )(qkv, qkv, qkv, gate, dec, gng, gnb)


def _diff_kernel(slopes_ref, q_ref, k_ref, v_ref, lam_ref, g_ref, o_ref,
                 m_ref, l_ref, acc_ref, *, n_kv, lam0):
    h = pl.program_id(1)
    qi = pl.program_id(2)
    slope = slopes_ref[h]
    lane = lax.broadcasted_iota(jnp.int32, (1, LANES), 1)
    m_lo = (lane < DIFF_DH).astype(F32)
    q = q_ref[0].astype(F32) * (DIFF_DH ** -0.5)
    q_maps = ((q * m_lo).astype(BF16), (q * (1.0 - m_lo)).astype(BF16))

    lv = lam_ref[...]
    lam = (jnp.exp(jnp.sum(lv[0:1] * lv[1:2], axis=-1, keepdims=True))
           - jnp.exp(jnp.sum(lv[2:3] * lv[3:4], axis=-1, keepdims=True)) + lam0)

    qpos = (qi * Q_TILE + lax.broadcasted_iota(jnp.int32, (Q_TILE, 1), 0)).astype(F32)
    m_ref[...] = jnp.full(m_ref.shape, -1e30, F32)
    l_ref[...] = jnp.zeros(l_ref.shape, F32)
    acc_ref[...] = jnp.zeros(acc_ref.shape, F32)

    def kv_body(j, carry):
        rows = pl.ds(pl.multiple_of(j * KV_TILE, KV_TILE), KV_TILE)
        k_t = k_ref[0, rows, :]
        v_t = v_ref[0, rows, :]
        kpos = (j * KV_TILE + lax.broadcasted_iota(jnp.int32, (1, KV_TILE), 1)).astype(F32)
        bias = -slope * jnp.abs(qpos - kpos)
        for mp in range(2):
            s = lax.dot_general(q_maps[mp], k_t, NT_DIMS, preferred_element_type=F32) + bias
            m_old = m_ref[mp]
            m_new = jnp.maximum(m_old, jnp.max(s, axis=-1, keepdims=True))
            alpha = jnp.exp(m_old - m_new)
            p = jnp.exp(s - m_new)
            l_ref[mp] = alpha * l_ref[mp] + jnp.sum(p, axis=-1, keepdims=True)
            acc_ref[mp] = alpha * acc_ref[mp] + jnp.dot(p.astype(BF16), v_t,
                                                        preferred_element_type=F32)
            m_ref[mp] = m_new
        return carry

    lax.fori_loop(0, n_kv, kv_body, 0)

    att = acc_ref[0] / l_ref[0] - lam * (acc_ref[1] / l_ref[1])
    ms = jnp.mean(att * att, axis=-1, keepdims=True)
    att = att * lax.rsqrt(ms + EPS) * g_ref[...] * (1.0 - lam0)
    o_ref[0] = att.astype(BF16)


def _diff_attention(qkv, slopes, lam_rows, subln_g, lam0):
    bsz, t, _ = qkv.shape
    return pl.pallas_call(
        functools.partial(_diff_kernel, n_kv=t // KV_TILE, lam0=lam0),
        grid_spec=pltpu.PrefetchScalarGridSpec(
            num_scalar_prefetch=1,
            grid=(bsz, DIFF_HEADS, t // Q_TILE),
            in_specs=[pl.BlockSpec((1, Q_TILE, LANES), lambda b, h, i, s: (b, i, DQ_BLK + h)),
                      pl.BlockSpec((1, t, LANES), lambda b, h, i, s: (b, 0, DK_BLK + h)),
                      pl.BlockSpec((1, t, LANES), lambda b, h, i, s: (b, 0, DV_BLK + h)),
                      pl.BlockSpec((8, LANES), lambda b, h, i, s: (0, 0)),
                      pl.BlockSpec((1, LANES), lambda b, h, i, s: (0, 0))],
            out_specs=pl.BlockSpec((1, Q_TILE, LANES), lambda b, h, i, s: (b, i, h)),
            scratch_shapes=[pltpu.VMEM((2, Q_TILE, 1), F32),
                            pltpu.VMEM((2, Q_TILE, 1), F32),
                            pltpu.VMEM((2, Q_TILE, DIFF_DV), F32)]),
        out_shape=jax.ShapeDtypeStruct((bsz, t, DIFF_WIDTH), BF16),
        compiler_params=_params("parallel", "parallel", "arbitrary"),
        name="diff_attention",
    )(slopes, qkv, qkv, qkv, lam_rows, subln_g)


def _outproj_ln_kernel(x_ref, r_ref, d_ref, w_ref, g_ref, b_ref, o_ref):
    mix = jnp.dot(r_ref[...], w_ref[0:RET_WIDTH, :], preferred_element_type=F32)
    mix = mix + jnp.dot(d_ref[...], w_ref[RET_WIDTH:, :], preferred_element_type=F32)
    o_ref[...] = _layer_norm(ALPHA * x_ref[...] + mix, g_ref[...], b_ref[...])


def _outproj_ln(x, ret, dif, w, g, b):
    m = x.shape[0]
    row = pl.BlockSpec((ROW_TILE, D_MODEL), lambda i: (i, 0))
    half = pl.BlockSpec((ROW_TILE, RET_WIDTH), lambda i: (i, 0))
    return pl.pallas_call(
        _outproj_ln_kernel,
        grid=(m // ROW_TILE,),
        in_specs=[row, half, half, _resident(w.shape),
                  _resident((1, D_MODEL)), _resident((1, D_MODEL))],
        out_specs=row,
        out_shape=jax.ShapeDtypeStruct((m, D_MODEL), F32),
        compiler_params=_params("parallel"),
        name="outproj_ln",
    )(x, ret, dif, w, g, b)


def _lambda_init(layer):
    return 0.8 - 0.6 * math.exp(-0.3 * layer)


def _trunk(x, p):
    bsz, t, _ = x.shape
    m = bsz * t
    xf = x.reshape(m, D_MODEL)
    for l in range(DEPTH):
        xf = _ffn_ln(xf, p["ffn1_w13"][l], p["ffn1_w2"][l], p["ln1_g"][l], p["ln1_b"][l])
        qkv, gate = _inproj(xf, p["w_in"][l])
        qkv = qkv.reshape(bsz, t, IN_WIDTH)
        ret = _retention(qkv, gate.reshape(bsz, t, RET_WIDTH), p["ret_dec"][l],
                         p["ret_gn_g"][l], p["ret_gn_b"][l])
        dif = _diff_attention(qkv, p["slopes"], p["lam_rows"][l], p["diff_subln_g"][l],
                              _lambda_init(l))
        xf = _outproj_ln(xf, ret.reshape(m, RET_WIDTH), dif.reshape(m, DIFF_WIDTH),
                         p["w_out"][l], p["ln2_g"][l], p["ln2_b"][l])
        xf = _ffn_ln(xf, p["ffn2_w13"][l], p["ffn2_w2"][l], p["ln3_g"][l], p["ln3_b"][l])
    return xf.reshape(bsz, t, D_MODEL)


def kernel(x_prompt, x_sample, w_in, w_out, ret_decay_f, ret_decay_b, ret_gn_g, ret_gn_b,
           diff_lq1, diff_lk1, diff_lq2, diff_lk2, diff_subln_g,
           ffn1_w13, ffn1_w2, ffn2_w13, ffn2_w2,
           ln1_g, ln1_b, ln2_g, ln2_b, ln3_g, ln3_b):
    n_pairs = RET_WIDTH // LANES

    def pair_rows(dec):
        return jnp.broadcast_to(dec.reshape(DEPTH, n_pairs, 2, 1), (DEPTH, n_pairs, 2, LANES))

    row = lambda a: a.reshape(DEPTH, 1, -1)
    lam_rows = jnp.stack([diff_lq1, diff_lk1, diff_lq2, diff_lk2], axis=1)
    lam_rows = jnp.pad(lam_rows, ((0, 0), (0, 4), (0, LANES - DIFF_DH)))
    p = {
        "w_in": w_in.astype(BF16), "w_out": w_out.astype(BF16),
        "ffn1_w13": ffn1_w13.astype(BF16), "ffn1_w2": ffn1_w2.astype(BF16),
        "ffn2_w13": ffn2_w13.astype(BF16), "ffn2_w2": ffn2_w2.astype(BF16),
        "ret_dec": jnp.concatenate([pair_rows(ret_decay_f), pair_rows(ret_decay_b)], axis=2),
        "ret_gn_g": row(ret_gn_g), "ret_gn_b": row(ret_gn_b),
        "lam_rows": lam_rows, "diff_subln_g": row(diff_subln_g),
        "slopes": jnp.asarray([2.0 ** (-8.0 * (h + 1) / DIFF_HEADS) for h in range(DIFF_HEADS)], F32),
        "ln1_g": row(ln1_g), "ln1_b": row(ln1_b), "ln2_g": row(ln2_g), "ln2_b": row(ln2_b),
        "ln3_g": row(ln3_g), "ln3_b": row(ln3_b),
    }
    return _trunk(x_prompt, p), _trunk(x_sample, p)
```

```python
import functools
import math

import jax
import jax.numpy as jnp
from jax import lax
from jax.experimental import pallas as pl
from jax.experimental.pallas import tpu as pltpu

D_MODEL = 1024
DEPTH = 2
RET_HEADS = 8
RET_DK = 64
RET_WIDTH = 512
DIFF_HEADS = 4
DIFF_DH = 64
DIFF_DV = 128
DIFF_WIDTH = 512
IN_WIDTH = 3584
D_FF = 2816
CHUNK = 128
EPS = 1e-5
ALPHA = (2 * DEPTH) ** 0.25

LANES = 128
FF_CHUNK = 1408
ROW_TILE = 512
Q_TILE = 256
KV_TILE = 512
VMEM_LIMIT = 56 * 1024 * 1024

RQ_BLK, RK_BLK, RV_BLK, RG_BLK, DQ_BLK, DK_BLK, DV_BLK = 0, 4, 8, 12, 16, 20, 24

BF16 = jnp.bfloat16
F32 = jnp.float32
NT_DIMS = (((1,), (1,)), ((), ()))


def _resident(shape):
    nd = len(shape)
    return pl.BlockSpec(shape, lambda *_: (0,) * nd, pipeline_mode=pl.Buffered(1))


def _layer_norm(y, g, b):
    mu = jnp.mean(y, axis=-1, keepdims=True)
    d = y - mu
    var = jnp.mean(d * d, axis=-1, keepdims=True)
    return d * lax.rsqrt(var + EPS) * g + b


def _params(*sem):
    return pltpu.CompilerParams(dimension_semantics=sem, vmem_limit_bytes=VMEM_LIMIT)


def _ffn_ln_kernel(x_ref, w13_ref, w2_ref, g_ref, b_ref, o_ref):
    x = x_ref[...]
    xb = x.astype(BF16)
    acc = None
    for lo in range(0, D_FF, FF_CHUNK):
        a = jnp.dot(xb, w13_ref[:, lo:lo + FF_CHUNK], preferred_element_type=F32)
        b = jnp.dot(xb, w13_ref[:, D_FF + lo:D_FF + lo + FF_CHUNK], preferred_element_type=F32)
        h = (jax.nn.silu(a) * b).astype(BF16)
        part = jnp.dot(h, w2_ref[lo:lo + FF_CHUNK, :], preferred_element_type=F32)
        acc = part if acc is None else acc + part
    o_ref[...] = _layer_norm(ALPHA * x + 0.5 * acc, g_ref[...], b_ref[...])


def _ffn_ln(x, w13, w2, g, b):
    m = x.shape[0]
    row = pl.BlockSpec((ROW_TILE, D_MODEL), lambda i: (i, 0))
    return pl.pallas_call(
        _ffn_ln_kernel,
        grid=(m // ROW_TILE,),
        in_specs=[row, _resident(w13.shape), _resident(w2.shape),
                  _resident((1, D_MODEL)), _resident((1, D_MODEL))],
        out_specs=row,
        out_shape=jax.ShapeDtypeStruct((m, D_MODEL), F32),
        compiler_params=_params("parallel"),
        name="ffn_ln",
    )(x, w13, w2, g, b)


def _inproj_kernel(x_ref, w_ref, o_ref, gate_ref):
    y = jnp.dot(x_ref[...].astype(BF16), w_ref[...], preferred_element_type=F32)
    o_ref[...] = y.astype(BF16)
    gate_ref[...] = y[:, RG_BLK * LANES:RG_BLK * LANES + RET_WIDTH]


def _inproj(x, w):
    m = x.shape[0]
    return pl.pallas_call(
        _inproj_kernel,
        grid=(m // ROW_TILE,),
        in_specs=[pl.BlockSpec((ROW_TILE, D_MODEL), lambda i: (i, 0)), _resident(w.shape)],
        out_specs=[pl.BlockSpec((ROW_TILE, IN_WIDTH), lambda i: (i, 0)),
                   pl.BlockSpec((ROW_TILE, RET_WIDTH), lambda i: (i, 0))],
        out_shape=[jax.ShapeDtypeStruct((m, IN_WIDTH), BF16),
                   jax.ShapeDtypeStruct((m, RET_WIDTH), F32)],
        compiler_params=_params("parallel"),
        name="inproj",
    )(x, w)


def _ret_kernel(q_ref, k_ref, v_ref, gate_ref, dec_ref, gng_ref, gnb_ref, o_ref,
                sf_ref, sb_ref, *, n_chunks):
    c_len = CHUNK
    k_scale = RET_DK ** -0.5
    lane = lax.broadcasted_iota(jnp.int32, (1, LANES), 1)
    lo_half = lane < RET_DK
    m_lo = lo_half.astype(F32)
    m_hi = 1.0 - m_lo
    lg = jax.nn.log_sigmoid(dec_ref[0])
    lgf0, lgf1, lgb0, lgb1 = lg[0:1], lg[1:2], lg[2:3], lg[3:4]
    lgf = jnp.where(lo_half, lgf0, lgf1)
    lgb = jnp.where(lo_half, lgb0, lgb1)

    ri = lax.broadcasted_iota(jnp.int32, (c_len, c_len), 0)
    ci = lax.broadcasted_iota(jnp.int32, (c_len, c_len), 1)
    dist = (ri - ci).astype(F32)
    causal = dist >= 0.0
    block_diag = (ri < RET_DK) == (ci < RET_DK)

    def decay_tile(lf, lb):
        return jnp.where(causal, jnp.exp(lf * jnp.maximum(dist, 0.0)),
                         jnp.exp(lb * jnp.maximum(-dist, 0.0))) * k_scale

    d0 = decay_tile(lgf0, lgb0)
    d1 = decay_tile(lgf1, lgb1)
    row = lax.broadcasted_iota(jnp.int32, (c_len, LANES), 0).astype(F32)
    qdf = jnp.exp(lgf * (row + 1.0))
    qdb = jnp.exp(lgb * (c_len - row))
    kdf = jnp.exp(lgf * (c_len - 1.0 - row)) * k_scale
    kdb = jnp.exp(lgb * row) * k_scale
    gcf = jnp.exp(lgf * c_len)
    gcb = jnp.exp(lgb * c_len)

    def rows_of(c):
        return pl.ds(pl.multiple_of(c * c_len, c_len), c_len)

    def kv_body(c, carry):
        rows = rows_of(c)
        k = k_ref[0, rows, :].astype(F32)
        v = v_ref[0, rows, :]
        kvf = jnp.dot((k * kdf).T.astype(BF16), v, preferred_element_type=F32)
        kvb = jnp.dot((k * kdb).T.astype(BF16), v, preferred_element_type=F32)
        sf_ref[c] = jnp.where(block_diag, kvf, 0.0)
        sb_ref[c] = jnp.where(block_diag, kvb, 0.0)
        return carry

    lax.fori_loop(0, n_chunks, kv_body, 0)

    def fwd_scan(c, s):
        kv = sf_ref[c]
        sf_ref[c] = s
        return s * gcf + kv

    lax.fori_loop(0, n_chunks, fwd_scan, jnp.zeros((c_len, LANES), F32))

    def bwd_scan(i, s):
        c = n_chunks - 1 - i
        kv = sb_ref[c]
        sb_ref[c] = s
        return s * gcb + kv

    lax.fori_loop(0, n_chunks, bwd_scan, jnp.zeros((c_len, LANES), F32))

    gng = gng_ref[...]
    gnb = gnb_ref[...]
    inv_dv = 1.0 / RET_DK

    def out_body(c, carry):
        rows = rows_of(c)
        q = q_ref[0, rows, :].astype(F32)
        k = k_ref[0, rows, :]
        v = v_ref[0, rows, :].astype(F32)
        s0 = lax.dot_general((q * m_lo).astype(BF16), k, NT_DIMS, preferred_element_type=F32)
        s1 = lax.dot_general((q * m_hi).astype(BF16), k, NT_DIMS, preferred_element_type=F32)
        o = jnp.dot((s0 * d0).astype(BF16), (v * m_lo).astype(BF16), preferred_element_type=F32)
        o = o + jnp.dot((s1 * d1).astype(BF16), (v * m_hi).astype(BF16), preferred_element_type=F32)
        o = o + jnp.dot((q * qdf).astype(BF16), sf_ref[c].astype(BF16), preferred_element_type=F32)
        o = o + jnp.dot((q * qdb).astype(BF16), sb_ref[c].astype(BF16), preferred_element_type=F32)
        s_lo = jnp.sum(o * m_lo, axis=-1, keepdims=True)
        s_hi = jnp.sum(o * m_hi, axis=-1, keepdims=True)
        d = o - jnp.where(lo_half, s_lo, s_hi) * inv_dv
        dd = d * d
        v_lo = jnp.sum(dd * m_lo, axis=-1, keepdims=True)
        v_hi = jnp.sum(dd * m_hi, axis=-1, keepdims=True)
        var = jnp.where(lo_half, v_lo, v_hi) * inv_dv
        y = d * lax.rsqrt(var + EPS) * gng + gnb
        o_ref[0, rows, :] = (y * jax.nn.silu(gate_ref[0, rows, :])).astype(BF16)
        return carry

    lax.fori_loop(0, n_chunks, out_body, 0)


def _retention(qkv, gate, dec, gng, gnb):
    bsz, t, _ = qkv.shape
    n_chunks = t // CHUNK
    n_pairs = RET_WIDTH // LANES

    def col(blk):
        return pl.BlockSpec((1, t, LANES), lambda b, p: (b, 0, blk + p))

    return pl.pallas_call(
        functools.partial(_ret_kernel, n_chunks=n_chunks),
        grid=(bsz, n_pairs),
        in_specs=[col(RQ_BLK), col(RK_BLK), col(RV_BLK),
                  pl.BlockSpec((1, t, LANES), lambda b, p: (b, 0, p)),
                  pl.BlockSpec((1, 4, LANES), lambda b, p: (p, 0, 0)),
                  pl.BlockSpec((1, LANES), lambda b, p: (0, p)),
                  pl.BlockSpec((1, LANES), lambda b, p: (0, p))],
        out_specs=pl.BlockSpec((1, t, LANES), lambda b, p: (b, 0, p)),
        out_shape=jax.ShapeDtypeStruct((bsz, t, RET_WIDTH), BF16),
        scratch_shapes=[pltpu.VMEM((n_chunks, CHUNK, LANES), F32),
                        pltpu.VMEM((n_chunks, CHUNK, LANES), F32)],
        compiler_params=_params("parallel", "parallel"),
        name="retention",
    )(qkv, qkv, qkv, gate, dec, gng, gnb)


def _diff_kernel(slopes_ref, q_ref, k_ref, v_ref, lam_ref, g_ref, o_ref,
                 vt_ref, srel_ref, s_even_ref, s_odd_ref, acc_ref, *, n_kv, lam0):
    h = pl.program_id(1)
    qi = pl.program_id(2)
    slope = slopes_ref[h]

    @pl.when(qi == 0)
    def _():
        for j in range(n_kv):
            vt_ref[j] = v_ref[0, j * KV_TILE:(j + 1) * KV_TILE, :].astype(F32).T.astype(BF16)

    qt = (q_ref[0].astype(F32) * (DIFF_DH ** -0.5)).T
    feat = lax.broadcasted_iota(jnp.int32, (LANES, 1), 0)
    q_maps = (jnp.where(feat < DIFF_DH, qt, 0.0).astype(BF16),
              jnp.where(feat < DIFF_DH, 0.0, qt).astype(BF16))

    lv = lam_ref[...]
    lam = (jnp.exp(jnp.sum(lv[0:1] * lv[1:2], axis=-1, keepdims=True))
           - jnp.exp(jnp.sum(lv[2:3] * lv[3:4], axis=-1, keepdims=True)) + lam0)

    kio = lax.broadcasted_iota(jnp.int32, (KV_TILE, Q_TILE), 0)
    qio = lax.broadcasted_iota(jnp.int32, (KV_TILE, Q_TILE), 1)
    srel_ref[...] = slope * (kio - qio).astype(F32)

    acc_ref[...] = jnp.zeros(acc_ref.shape, F32)

    def score_stage(j, dst_ref):
        k_t = k_ref[0, pl.ds(pl.multiple_of(j * KV_TILE, KV_TILE), KV_TILE), :]
        shift = slope * (j * KV_TILE - qi * Q_TILE).astype(F32)
        bias = -jnp.abs(srel_ref[...] + shift)
        col_max = []
        for mp in range(2):
            s = jnp.dot(k_t, q_maps[mp], preferred_element_type=F32) + bias
            dst_ref[mp] = s
            col_max.append(jnp.max(s, axis=0, keepdims=True))
        return tuple(col_max)

    def softmax_stage(j, src_ref, col_max, stats):
        vt_t = vt_ref[j]
        new = []
        for mp in range(2):
            m_old, l_old = stats[mp]
            m_new = jnp.maximum(m_old, col_max[mp])
            alpha = jnp.exp(m_old - m_new)
            p = jnp.exp(src_ref[mp] - m_new)
            l_new = alpha * l_old + jnp.sum(p, axis=0, keepdims=True)
            acc_ref[mp] = alpha * acc_ref[mp] + jnp.dot(vt_t, p.astype(BF16),
                                                        preferred_element_type=F32)
            new.append((m_new, l_new))
        return tuple(new)

    def pair_body(i, carry):
        stats, cm_even = carry
        cm_odd = score_stage(2 * i + 1, s_odd_ref)
        stats = softmax_stage(2 * i, s_even_ref, cm_even, stats)
        cm_even = score_stage(2 * i + 2, s_even_ref)
        stats = softmax_stage(2 * i + 1, s_odd_ref, cm_odd, stats)
        return stats, cm_even

    init = (jnp.full((1, Q_TILE), -1e30, F32), jnp.zeros((1, Q_TILE), F32))
    n_pairs = n_kv // 2
    stats, cm_even = lax.fori_loop(0, n_pairs - 1, pair_body,
                                   ((init, init), score_stage(0, s_even_ref)))
    cm_odd = score_stage(n_kv - 1, s_odd_ref)
    stats = softmax_stage(n_kv - 2, s_even_ref, cm_even, stats)
    (_, l0), (_, l1) = softmax_stage(n_kv - 1, s_odd_ref, cm_odd, stats)

    att = acc_ref[0] / l0 - lam * (acc_ref[1] / l1)
    ms = jnp.mean(att * att, axis=0, keepdims=True)
    att = (att * lax.rsqrt(ms + EPS)).T
    o_ref[0] = (att * g_ref[...] * (1.0 - lam0)).astype(BF16)


def _diff_attention(qkv, slopes, lam_rows, subln_g, lam0):
    bsz, t, _ = qkv.shape
    n_kv = t // KV_TILE
    return pl.pallas_call(
        functools.partial(_diff_kernel, n_kv=n_kv, lam0=lam0),
        grid_spec=pltpu.PrefetchScalarGridSpec(
            num_scalar_prefetch=1,
            grid=(bsz, DIFF_HEADS, t // Q_TILE),
            in_specs=[pl.BlockSpec((1, Q_TILE, LANES), lambda b, h, i, s: (b, i, DQ_BLK + h)),
                      pl.BlockSpec((1, t, LANES), lambda b, h, i, s: (b, 0, DK_BLK + h)),
                      pl.BlockSpec((1, t, LANES), lambda b, h, i, s: (b, 0, DV_BLK + h)),
                      pl.BlockSpec((8, LANES), lambda b, h, i, s: (0, 0)),
                      pl.BlockSpec((1, LANES), lambda b, h, i, s: (0, 0))],
            out_specs=pl.BlockSpec((1, Q_TILE, LANES), lambda b, h, i, s: (b, i, h)),
            scratch_shapes=[pltpu.VMEM((n_kv, DIFF_DV, KV_TILE), BF16),
                            pltpu.VMEM((KV_TILE, Q_TILE), F32),
                            pltpu.VMEM((2, KV_TILE, Q_TILE), F32),
                            pltpu.VMEM((2, KV_TILE, Q_TILE), F32),
                            pltpu.VMEM((2, DIFF_DV, Q_TILE), F32)]),
        out_shape=jax.ShapeDtypeStruct((bsz, t, DIFF_WIDTH), BF16),
        compiler_params=_params("parallel", "parallel", "arbitrary"),
        name="diff_attention",
    )(slopes, qkv, qkv, qkv, lam_rows, subln_g)


def _outproj_ln_kernel(x_ref, r_ref, d_ref, w_ref, g_ref, b_ref, o_ref):
    mix = jnp.dot(r_ref[...], w_ref[0:RET_WIDTH, :], preferred_element_type=F32)
    mix = mix + jnp.dot(d_ref[...], w_ref[RET_WIDTH:, :], preferred_element_type=F32)
    o_ref[...] = _layer_norm(ALPHA * x_ref[...] + mix, g_ref[...], b_ref[...])


def _outproj_ln(x, ret, dif, w, g, b):
    m = x.shape[0]
    row = pl.BlockSpec((ROW_TILE, D_MODEL), lambda i: (i, 0))
    half = pl.BlockSpec((ROW_TILE, RET_WIDTH), lambda i: (i, 0))
    return pl.pallas_call(
        _outproj_ln_kernel,
        grid=(m // ROW_TILE,),
        in_specs=[row, half, half, _resident(w.shape),
                  _resident((1, D_MODEL)), _resident((1, D_MODEL))],
        out_specs=row,
        out_shape=jax.ShapeDtypeStruct((m, D_MODEL), F32),
        compiler_params=_params("parallel"),
        name="outproj_ln",
    )(x, ret, dif, w, g, b)


def _lambda_init(layer):
    return 0.8 - 0.6 * math.exp(-0.3 * layer)


def _trunk(x, p):
    bsz, t, _ = x.shape
    m = bsz * t
    xf = x.reshape(m, D_MODEL)
    for l in range(DEPTH):
        xf = _ffn_ln(xf, p["ffn1_w13"][l], p["ffn1_w2"][l], p["ln1_g"][l], p["ln1_b"][l])
        qkv, gate = _inproj(xf, p["w_in"][l])
        qkv = qkv.reshape(bsz, t, IN_WIDTH)
        ret = _retention(qkv, gate.reshape(bsz, t, RET_WIDTH), p["ret_dec"][l],
                         p["ret_gn_g"][l], p["ret_gn_b"][l])
        dif = _diff_attention(qkv, p["slopes"], p["lam_rows"][l], p["diff_subln_g"][l],
                              _lambda_init(l))
        xf = _outproj_ln(xf, ret.reshape(m, RET_WIDTH), dif.reshape(m, DIFF_WIDTH),
                         p["w_out"][l], p["ln2_g"][l], p["ln2_b"][l])
        xf = _ffn_ln(xf, p["ffn2_w13"][l], p["ffn2_w2"][l], p["ln3_g"][l], p["ln3_b"][l])
    return xf.reshape(bsz, t, D_MODEL)


def kernel(x_prompt, x_sample, w_in, w_out, ret_decay_f, ret_decay_b, ret_gn_g, ret_gn_b,
           diff_lq1, diff_lk1, diff_lq2, diff_lk2, diff_subln_g,
           ffn1_w13, ffn1_w2, ffn2_w13, ffn2_w2,
           ln1_g, ln1_b, ln2_g, ln2_b, ln3_g, ln3_b):
    n_pairs = RET_WIDTH // LANES

    def pair_rows(dec):
        return jnp.broadcast_to(dec.reshape(DEPTH, n_pairs, 2, 1), (DEPTH, n_pairs, 2, LANES))

    row = lambda a: a.reshape(DEPTH, 1, -1)
    lam_rows = jnp.stack([diff_lq1, diff_lk1, diff_lq2, diff_lk2], axis=1)
    lam_rows = jnp.pad(lam_rows, ((0, 0), (0, 4), (0, LANES - DIFF_DH)))
    p = {
        "w_in": w_in.astype(BF16), "w_out": w_out.astype(BF16),
        "ffn1_w13": ffn1_w13.astype(BF16), "ffn1_w2": ffn1_w2.astype(BF16),
        "ffn2_w13": ffn2_w13.astype(BF16), "ffn2_w2": ffn2_w2.astype(BF16),
        "ret_dec": jnp.concatenate([pair_rows(ret_decay_f), pair_rows(ret_decay_b)], axis=2),
        "ret_gn_g": row(ret_gn_g), "ret_gn_b": row(ret_gn_b),
        "lam_rows": lam_rows, "diff_subln_g": row(diff_subln_g),
        "slopes": jnp.asarray([2.0 ** (-8.0 * (h + 1) / DIFF_HEADS) for h in range(DIFF_HEADS)], F32),
        "ln1_g": row(ln1_g), "ln1_b": row(ln1_b), "ln2_g": row(ln2_g), "ln2_b": row(ln2_b),
        "ln3_g": row(ln3_g), "ln3_b": row(ln3_b),
    }
    return _trunk(x_prompt, p), _trunk(x_sample, p)
```

```python
import functools
import math

import jax
import jax.numpy as jnp
from jax import lax
from jax.experimental import pallas as pl
from jax.experimental.pallas import tpu as pltpu

D_MODEL = 1024
DEPTH = 2
RET_HEADS = 8
RET_DK = 64
RET_WIDTH = 512
DIFF_HEADS = 4
DIFF_DH = 64
DIFF_DV = 128
DIFF_WIDTH = 512
IN_WIDTH = 3584
D_FF = 2816
CHUNK = 128
EPS = 1e-5
ALPHA = (2 * DEPTH) ** 0.25

LANES = 128
FF_CHUNK = 1408
ROW_TILE = 512
Q_TILE = 256
KV_TILE = 512
VT_PAD = 16
AUG_ROWS = 16
FEAT_RADIX = 64
LOG2E = math.log2(math.e)
VMEM_LIMIT = 56 * 1024 * 1024

RQ_BLK, RK_BLK, RV_BLK, RG_BLK, DQ_BLK, DK_BLK, DV_BLK = 0, 4, 8, 12, 16, 20, 24

BF16 = jnp.bfloat16
F32 = jnp.float32
NT_DIMS = (((1,), (1,)), ((), ()))


def _resident(shape):
    nd = len(shape)
    return pl.BlockSpec(shape, lambda *_: (0,) * nd, pipeline_mode=pl.Buffered(1))


def _layer_norm(y, g, b):
    mu = jnp.mean(y, axis=-1, keepdims=True)
    d = y - mu
    var = jnp.mean(d * d, axis=-1, keepdims=True)
    return d * lax.rsqrt(var + EPS) * g + b


def _params(*sem):
    return pltpu.CompilerParams(dimension_semantics=sem, vmem_limit_bytes=VMEM_LIMIT)


def _ffn_ln_kernel(x_ref, w13_ref, w2_ref, g_ref, b_ref, o_ref):
    x = x_ref[...]
    xb = x.astype(BF16)
    acc = None
    for lo in range(0, D_FF, FF_CHUNK):
        a = jnp.dot(xb, w13_ref[:, lo:lo + FF_CHUNK], preferred_element_type=F32)
        b = jnp.dot(xb, w13_ref[:, D_FF + lo:D_FF + lo + FF_CHUNK], preferred_element_type=F32)
        h = (jax.nn.silu(a) * b).astype(BF16)
        part = jnp.dot(h, w2_ref[lo:lo + FF_CHUNK, :], preferred_element_type=F32)
        acc = part if acc is None else acc + part
    o_ref[...] = _layer_norm(ALPHA * x + 0.5 * acc, g_ref[...], b_ref[...])


def _ffn_ln(x, w13, w2, g, b):
    m = x.shape[0]
    row = pl.BlockSpec((ROW_TILE, D_MODEL), lambda i: (i, 0))
    return pl.pallas_call(
        _ffn_ln_kernel,
        grid=(m // ROW_TILE,),
        in_specs=[row, _resident(w13.shape), _resident(w2.shape),
                  _resident((1, D_MODEL)), _resident((1, D_MODEL))],
        out_specs=row,
        out_shape=jax.ShapeDtypeStruct((m, D_MODEL), F32),
        compiler_params=_params("parallel"),
        name="ffn_ln",
    )(x, w13, w2, g, b)


def _inproj_kernel(x_ref, w_ref, o_ref, gate_ref):
    y = jnp.dot(x_ref[...].astype(BF16), w_ref[...], preferred_element_type=F32)
    o_ref[...] = y.astype(BF16)
    gate_ref[...] = y[:, RG_BLK * LANES:RG_BLK * LANES + RET_WIDTH]


def _inproj(x, w):
    m = x.shape[0]
    return pl.pallas_call(
        _inproj_kernel,
        grid=(m // ROW_TILE,),
        in_specs=[pl.BlockSpec((ROW_TILE, D_MODEL), lambda i: (i, 0)), _resident(w.shape)],
        out_specs=[pl.BlockSpec((ROW_TILE, IN_WIDTH), lambda i: (i, 0)),
                   pl.BlockSpec((ROW_TILE, RET_WIDTH), lambda i: (i, 0))],
        out_shape=[jax.ShapeDtypeStruct((m, IN_WIDTH), BF16),
                   jax.ShapeDtypeStruct((m, RET_WIDTH), F32)],
        compiler_params=_params("parallel"),
        name="inproj",
    )(x, w)


def _ret_kernel(q_ref, k_ref, v_ref, gate_ref, dec_ref, gng_ref, gnb_ref, o_ref,
                sf_ref, sb_ref, *, n_chunks):
    c_len = CHUNK
    k_scale = RET_DK ** -0.5
    lane = lax.broadcasted_iota(jnp.int32, (1, LANES), 1)
    lo_half = lane < RET_DK
    m_lo = lo_half.astype(F32)
    m_hi = 1.0 - m_lo
    lg = jax.nn.log_sigmoid(dec_ref[0])
    lgf0, lgf1, lgb0, lgb1 = lg[0:1], lg[1:2], lg[2:3], lg[3:4]
    lgf = jnp.where(lo_half, lgf0, lgf1)
    lgb = jnp.where(lo_half, lgb0, lgb1)

    ri = lax.broadcasted_iota(jnp.int32, (c_len, c_len), 0)
    ci = lax.broadcasted_iota(jnp.int32, (c_len, c_len), 1)
    dist = (ri - ci).astype(F32)
    causal = dist >= 0.0
    block_diag = (ri < RET_DK) == (ci < RET_DK)

    def decay_tile(lf, lb):
        return jnp.where(causal, jnp.exp(lf * jnp.maximum(dist, 0.0)),
                         jnp.exp(lb * jnp.maximum(-dist, 0.0))) * k_scale

    d0 = decay_tile(lgf0, lgb0)
    d1 = decay_tile(lgf1, lgb1)
    row = lax.broadcasted_iota(jnp.int32, (c_len, LANES), 0).astype(F32)
    qdf = jnp.exp(lgf * (row + 1.0))
    qdb = jnp.exp(lgb * (c_len - row))
    kdf = jnp.exp(lgf * (c_len - 1.0 - row)) * k_scale
    kdb = jnp.exp(lgb * row) * k_scale
    gcf = jnp.exp(lgf * c_len)
    gcb = jnp.exp(lgb * c_len)

    def rows_of(c):
        return pl.ds(pl.multiple_of(c * c_len, c_len), c_len)

    def kv_body(c, carry):
        rows = rows_of(c)
        k = k_ref[0, rows, :].astype(F32)
        v = v_ref[0, rows, :]
        kvf = jnp.dot((k * kdf).T.astype(BF16), v, preferred_element_type=F32)
        kvb = jnp.dot((k * kdb).T.astype(BF16), v, preferred_element_type=F32)
        sf_ref[c] = jnp.where(block_diag, kvf, 0.0)
        sb_ref[c] = jnp.where(block_diag, kvb, 0.0)
        return carry

    lax.fori_loop(0, n_chunks, kv_body, 0, unroll=8)

    def fwd_scan(c, s):
        kv = sf_ref[c]
        sf_ref[c] = s
        return s * gcf + kv

    lax.fori_loop(0, n_chunks, fwd_scan, jnp.zeros((c_len, LANES), F32))

    def bwd_scan(i, s):
        c = n_chunks - 1 - i
        kv = sb_ref[c]
        sb_ref[c] = s
        return s * gcb + kv

    lax.fori_loop(0, n_chunks, bwd_scan, jnp.zeros((c_len, LANES), F32))

    gng = gng_ref[...]
    gnb = gnb_ref[...]
    inv_dv = 1.0 / RET_DK

    def out_body(c, carry):
        rows = rows_of(c)
        q = q_ref[0, rows, :].astype(F32)
        k = k_ref[0, rows, :]
        v = v_ref[0, rows, :].astype(F32)
        s0 = lax.dot_general((q * m_lo).astype(BF16), k, NT_DIMS, preferred_element_type=F32)
        s1 = lax.dot_general((q * m_hi).astype(BF16), k, NT_DIMS, preferred_element_type=F32)
        o = jnp.dot((s0 * d0).astype(BF16), (v * m_lo).astype(BF16), preferred_element_type=F32)
        o = o + jnp.dot((s1 * d1).astype(BF16), (v * m_hi).astype(BF16), preferred_element_type=F32)
        o = o + jnp.dot((q * qdf).astype(BF16), sf_ref[c].astype(BF16), preferred_element_type=F32)
        o = o + jnp.dot((q * qdb).astype(BF16), sb_ref[c].astype(BF16), preferred_element_type=F32)
        s_lo = jnp.sum(o * m_lo, axis=-1, keepdims=True)
        s_hi = jnp.sum(o * m_hi, axis=-1, keepdims=True)
        d = o - jnp.where(lo_half, s_lo, s_hi) * inv_dv
        dd = d * d
        v_lo = jnp.sum(dd * m_lo, axis=-1, keepdims=True)
        v_hi = jnp.sum(dd * m_hi, axis=-1, keepdims=True)
        var = jnp.where(lo_half, v_lo, v_hi) * inv_dv
        y = d * lax.rsqrt(var + EPS) * gng + gnb
        o_ref[0, rows, :] = (y * jax.nn.silu(gate_ref[0, rows, :])).astype(BF16)
        return carry

    lax.fori_loop(0, n_chunks, out_body, 0, unroll=8)


def _retention(qkv, gate, dec, gng, gnb):
    bsz, t, _ = qkv.shape
    n_chunks = t // CHUNK
    n_pairs = RET_WIDTH // LANES

    def col(blk):
        return pl.BlockSpec((1, t, LANES), lambda b, p: (b, 0, blk + p))

    return pl.pallas_call(
        functools.partial(_ret_kernel, n_chunks=n_chunks),
        grid=(bsz, n_pairs),
        in_specs=[col(RQ_BLK), col(RK_BLK), col(RV_BLK),
                  pl.BlockSpec((1, t, LANES), lambda b, p: (b, 0, p)),
                  pl.BlockSpec((1, 4, LANES), lambda b, p: (p, 0, 0)),
                  pl.BlockSpec((1, LANES), lambda b, p: (0, p)),
                  pl.BlockSpec((1, LANES), lambda b, p: (0, p))],
        out_specs=pl.BlockSpec((1, t, LANES), lambda b, p: (b, 0, p)),
        out_shape=jax.ShapeDtypeStruct((bsz, t, RET_WIDTH), BF16),
        scratch_shapes=[pltpu.VMEM((n_chunks, CHUNK, LANES), F32),
                        pltpu.VMEM((n_chunks, CHUNK, LANES), F32)],
        compiler_params=_params("parallel", "parallel"),
        name="retention",
    )(qkv, qkv, qkv, gate, dec, gng, gnb)


def _split3(x):
    x1 = x.astype(BF16).astype(F32)
    r = x - x1
    x2 = r.astype(BF16).astype(F32)
    x3 = (r - x2).astype(BF16).astype(F32)
    return x1, x2, x3


def _diff_kernel(slopes_ref, q_ref, k_ref, v_ref, kfeat_ref, lam_ref, g_ref, o_ref,
                 vt_ref, srel_ref, s_even_ref, s_odd_ref, acc_ref, *, n_kv, lam0):
    h = pl.program_id(1)
    qi = pl.program_id(2)
    c = slopes_ref[h] * LOG2E
    jd = (qi * Q_TILE) // KV_TILE

    @pl.when(qi == 0)
    def _():
        ones_row = (lax.broadcasted_iota(jnp.int32, (VT_PAD, KV_TILE), 0) == 0).astype(F32)
        for j in range(n_kv):
            vt = v_ref[0, j * KV_TILE:(j + 1) * KV_TILE, :].astype(F32).T
            vt_ref[j] = jnp.concatenate([vt, ones_row], axis=0).astype(BF16)
        kio = lax.broadcasted_iota(jnp.int32, (KV_TILE, Q_TILE), 0)
        qio = lax.broadcasted_iota(jnp.int32, (KV_TILE, Q_TILE), 1)
        srel_ref[...] = c * (kio - qio).astype(F32)

    qt = (q_ref[0].astype(F32) * (LOG2E * DIFF_DH ** -0.5)).T
    feat = lax.broadcasted_iota(jnp.int32, (LANES, 1), 0)
    q_maps = (jnp.where(feat < DIFF_DH, qt, 0.0).astype(BF16),
              jnp.where(feat < DIFF_DH, 0.0, qt).astype(BF16))

    qpos = (qi * Q_TILE + lax.broadcasted_iota(jnp.int32, (1, Q_TILE), 1)).astype(F32)
    c_row = jnp.full((1, Q_TILE), c, F32)
    c_parts = _split3(c_row)
    t_parts = _split3(-(c_row * qpos))
    coef = [float(FEAT_RADIX) * cp for cp in c_parts] + list(c_parts) + list(t_parts)
    rowi = lax.broadcasted_iota(jnp.int32, (AUG_ROWS, Q_TILE), 0)
    aug = jnp.zeros((AUG_ROWS, Q_TILE), F32)
    for r, row in enumerate(coef):
        aug = jnp.where(rowi == r, row, aug)
    tail = jnp.concatenate([aug.astype(BF16), jnp.zeros((LANES - AUG_ROWS, Q_TILE), BF16)], axis=0)
    q_aug = tuple(jnp.concatenate([qm, tail], axis=0) for qm in q_maps)

    lv = lam_ref[...]
    lam = (jnp.exp(jnp.sum(lv[0:1] * lv[1:2], axis=-1, keepdims=True))
           - jnp.exp(jnp.sum(lv[2:3] * lv[3:4], axis=-1, keepdims=True)) + lam0)

    acc_ref[...] = jnp.zeros(acc_ref.shape, F32)

    def tile_of(pos):
        return jnp.where(pos == 0, jd, jnp.where(pos - 1 < jd, pos - 1, pos))

    def key_rows(j):
        return pl.ds(pl.multiple_of(j * KV_TILE, KV_TILE), KV_TILE)

    def finish_scores(s_maps, dst_ref):
        col_max = []
        for mp in range(2):
            dst_ref[mp] = s_maps[mp]
            col_max.append(jnp.max(s_maps[mp], axis=0, keepdims=True))
        return tuple(col_max)

    def diag_score_stage(dst_ref):
        k_t = k_ref[0, key_rows(jd), :]
        shift = c * (jd * KV_TILE - qi * Q_TILE).astype(F32)
        bias = -jnp.abs(srel_ref[...] + shift)
        return finish_scores([jnp.dot(k_t, q_maps[mp], preferred_element_type=F32) + bias
                              for mp in range(2)], dst_ref)

    def score_stage(pos, dst_ref):
        j = tile_of(pos)
        side = (j > jd).astype(jnp.int32)
        lhs = jnp.concatenate([k_ref[0, key_rows(j), :], kfeat_ref[side, key_rows(j), :]], axis=1)
        return finish_scores([jnp.dot(lhs, q_aug[mp], preferred_element_type=F32)
                              for mp in range(2)], dst_ref)

    def softmax_stage(pos, src_ref, col_max, m_run):
        vt_t = vt_ref[tile_of(pos)]
        new = []
        for mp in range(2):
            m_new = jnp.maximum(m_run[mp], col_max[mp])
            alpha = jnp.exp2(m_run[mp] - m_new)
            p = jnp.exp2(src_ref[mp] - m_new).astype(BF16)
            acc_ref[mp] = alpha * acc_ref[mp] + jnp.dot(vt_t, p, preferred_element_type=F32)
            new.append(m_new)
        return tuple(new)

    def pair_body(i, carry):
        m_run, cm_even = carry
        cm_odd = score_stage(2 * i + 1, s_odd_ref)
        m_run = softmax_stage(2 * i, s_even_ref, cm_even, m_run)
        cm_even = score_stage(2 * i + 2, s_even_ref)
        m_run = softmax_stage(2 * i + 1, s_odd_ref, cm_odd, m_run)
        return m_run, cm_even

    m_init = jnp.full((1, Q_TILE), -1e30, F32)
    m_run, cm_even = lax.fori_loop(0, n_kv // 2 - 1, pair_body,
                                   ((m_init, m_init), diag_score_stage(s_even_ref)))
    cm_odd = score_stage(n_kv - 1, s_odd_ref)
    m_run = softmax_stage(n_kv - 2, s_even_ref, cm_even, m_run)
    softmax_stage(n_kv - 1, s_odd_ref, cm_odd, m_run)

    a0 = acc_ref[0]
    a1 = acc_ref[1]
    att = (a0[:DIFF_DV] / a0[DIFF_DV:DIFF_DV + 1]
           - lam * (a1[:DIFF_DV] / a1[DIFF_DV:DIFF_DV + 1]))
    ms = jnp.mean(att * att, axis=0, keepdims=True)
    att = (att * lax.rsqrt(ms + EPS)).T
    o_ref[0] = (att * g_ref[...] * (1.0 - lam0)).astype(BF16)


def _key_features(t):
    pos = jnp.arange(t, dtype=jnp.int32)
    hi = (pos // FEAT_RADIX).astype(F32)
    lo = (pos % FEAT_RADIX).astype(F32)
    one = jnp.ones((t,), F32)
    cols = jnp.stack([hi, hi, hi, lo, lo, lo, one, one, one], axis=1)
    cols = jnp.pad(cols, ((0, 0), (0, LANES - cols.shape[1])))
    return jnp.stack([cols, -cols]).astype(BF16)


def _diff_attention(qkv, slopes, lam_rows, subln_g, lam0):
    bsz, t, _ = qkv.shape
    n_kv = t // KV_TILE
    return pl.pallas_call(
        functools.partial(_diff_kernel, n_kv=n_kv, lam0=lam0),
        grid_spec=pltpu.PrefetchScalarGridSpec(
            num_scalar_prefetch=1,
            grid=(bsz, DIFF_HEADS, t // Q_TILE),
            in_specs=[pl.BlockSpec((1, Q_TILE, LANES), lambda b, h, i, s: (b, i, DQ_BLK + h)),
                      pl.BlockSpec((1, t, LANES), lambda b, h, i, s: (b, 0, DK_BLK + h)),
                      pl.BlockSpec((1, t, LANES), lambda b, h, i, s: (b, 0, DV_BLK + h)),
                      pl.BlockSpec((2, t, LANES), lambda b, h, i, s: (0, 0, 0),
                                   pipeline_mode=pl.Buffered(1)),
                      pl.BlockSpec((8, LANES), lambda b, h, i, s: (0, 0)),
                      pl.BlockSpec((1, LANES), lambda b, h, i, s: (0, 0))],
            out_specs=pl.BlockSpec((1, Q_TILE, LANES), lambda b, h, i, s: (b, i, h)),
            scratch_shapes=[pltpu.VMEM((n_kv, DIFF_DV + VT_PAD, KV_TILE), BF16),
                            pltpu.VMEM((KV_TILE, Q_TILE), F32),
                            pltpu.VMEM((2, KV_TILE, Q_TILE), F32),
                            pltpu.VMEM((2, KV_TILE, Q_TILE), F32),
                            pltpu.VMEM((2, DIFF_DV + VT_PAD, Q_TILE), F32)]),
        out_shape=jax.ShapeDtypeStruct((bsz, t, DIFF_WIDTH), BF16),
        compiler_params=_params("parallel", "parallel", "arbitrary"),
        name="diff_attention",
    )(slopes, qkv, qkv, qkv, _key_features(t), lam_rows, subln_g)


def _outproj_ln_kernel(x_ref, r_ref, d_ref, w_ref, g_ref, b_ref, o_ref):
    mix = jnp.dot(r_ref[...], w_ref[0:RET_WIDTH, :], preferred_element_type=F32)
    mix = mix + jnp.dot(d_ref[...], w_ref[RET_WIDTH:, :], preferred_element_type=F32)
    o_ref[...] = _layer_norm(ALPHA * x_ref[...] + mix, g_ref[...], b_ref[...])


def _outproj_ln(x, ret, dif, w, g, b):
    m = x.shape[0]
    row = pl.BlockSpec((ROW_TILE, D_MODEL), lambda i: (i, 0))
    half = pl.BlockSpec((ROW_TILE, RET_WIDTH), lambda i: (i, 0))
    return pl.pallas_call(
        _outproj_ln_kernel,
        grid=(m // ROW_TILE,),
        in_specs=[row, half, half, _resident(w.shape),
                  _resident((1, D_MODEL)), _resident((1, D_MODEL))],
        out_specs=row,
        out_shape=jax.ShapeDtypeStruct((m, D_MODEL), F32),
        compiler_params=_params("parallel"),
        name="outproj_ln",
    )(x, ret, dif, w, g, b)


def _lambda_init(layer):
    return 0.8 - 0.6 * math.exp(-0.3 * layer)


def _trunk(x, p):
    bsz, t, _ = x.shape
    m = bsz * t
    xf = x.reshape(m, D_MODEL)
    for l in range(DEPTH):
        xf = _ffn_ln(xf, p["ffn1_w13"][l], p["ffn1_w2"][l], p["ln1_g"][l], p["ln1_b"][l])
        qkv, gate = _inproj(xf, p["w_in"][l])
        qkv = qkv.reshape(bsz, t, IN_WIDTH)
        ret = _retention(qkv, gate.reshape(bsz, t, RET_WIDTH), p["ret_dec"][l],
                         p["ret_gn_g"][l], p["ret_gn_b"][l])
        dif = _diff_attention(qkv, p["slopes"], p["lam_rows"][l], p["diff_subln_g"][l],
                              _lambda_init(l))
        xf = _outproj_ln(xf, ret.reshape(m, RET_WIDTH), dif.reshape(m, DIFF_WIDTH),
                         p["w_out"][l], p["ln2_g"][l], p["ln2_b"][l])
        xf = _ffn_ln(xf, p["ffn2_w13"][l], p["ffn2_w2"][l], p["ln3_g"][l], p["ln3_b"][l])
    return xf.reshape(bsz, t, D_MODEL)


def kernel(x_prompt, x_sample, w_in, w_out, ret_decay_f, ret_decay_b, ret_gn_g, ret_gn_b,
           diff_lq1, diff_lk1, diff_lq2, diff_lk2, diff_subln_g,
           ffn1_w13, ffn1_w2, ffn2_w13, ffn2_w2,
           ln1_g, ln1_b, ln2_g, ln2_b, ln3_g, ln3_b):
    n_pairs = RET_WIDTH // LANES

    def pair_rows(dec):
        return jnp.broadcast_to(dec.reshape(DEPTH, n_pairs, 2, 1), (DEPTH, n_pairs, 2, LANES))

    row = lambda a: a.reshape(DEPTH, 1, -1)
    lam_rows = jnp.stack([diff_lq1, diff_lk1, diff_lq2, diff_lk2], axis=1)
    lam_rows = jnp.pad(lam_rows, ((0, 0), (0, 4), (0, LANES - DIFF_DH)))
    p = {
        "w_in": w_in.astype(BF16), "w_out": w_out.astype(BF16),
        "ffn1_w13": ffn1_w13.astype(BF16), "ffn1_w2": ffn1_w2.astype(BF16),
        "ffn2_w13": ffn2_w13.astype(BF16), "ffn2_w2": ffn2_w2.astype(BF16),
        "ret_dec": jnp.concatenate([pair_rows(ret_decay_f), pair_rows(ret_decay_b)], axis=2),
        "ret_gn_g": row(ret_gn_g), "ret_gn_b": row(ret_gn_b),
        "lam_rows": lam_rows, "diff_subln_g": row(diff_subln_g),
        "slopes": jnp.asarray([2.0 ** (-8.0 * (h + 1) / DIFF_HEADS) for h in range(DIFF_HEADS)], F32),
        "ln1_g": row(ln1_g), "ln1_b": row(ln1_b), "ln2_g": row(ln2_g), "ln2_b": row(ln2_b),
        "ln3_g": row(ln3_g), "ln3_b": row(ln3_b),
    }
    return _trunk(x_prompt, p), _trunk(x_sample, p)
```

```python
import functools
import math

import jax
import jax.numpy as jnp
from jax import lax
from jax.experimental import pallas as pl
from jax.experimental.pallas import tpu as pltpu

D_MODEL = 1024
DEPTH = 2
RET_HEADS = 8
RET_DK = 64
RET_WIDTH = 512
DIFF_HEADS = 4
DIFF_DH = 64
DIFF_DV = 128
DIFF_WIDTH = 512
IN_WIDTH = 3584
D_FF = 2816
CHUNK = 128
EPS = 1e-5
ALPHA = (2 * DEPTH) ** 0.25

LANES = 128
FF_CHUNK = 1408
ROW_TILE = 512
Q_TILE = 512
KV_TILE = 512
VT_PAD = 16
AUG_ROWS = 16
FEAT_RADIX = 64
LOG2E = math.log2(math.e)
VMEM_LIMIT = 56 * 1024 * 1024

RQ_BLK, RK_BLK, RV_BLK, RG_BLK, DQ_BLK, DK_BLK, DV_BLK = 0, 4, 8, 12, 16, 20, 24

BF16 = jnp.bfloat16
F32 = jnp.float32
NT_DIMS = (((1,), (1,)), ((), ()))


def _resident(shape):
    nd = len(shape)
    return pl.BlockSpec(shape, lambda *_: (0,) * nd, pipeline_mode=pl.Buffered(1))


def _layer_norm(y, g, b):
    mu = jnp.mean(y, axis=-1, keepdims=True)
    d = y - mu
    var = jnp.mean(d * d, axis=-1, keepdims=True)
    return d * lax.rsqrt(var + EPS) * g + b


def _params(*sem):
    return pltpu.CompilerParams(dimension_semantics=sem, vmem_limit_bytes=VMEM_LIMIT)


def _ffn_ln_kernel(x_ref, w13_ref, w2_ref, g_ref, b_ref, o_ref):
    x = x_ref[...]
    xb = x.astype(BF16)
    acc = None
    for lo in range(0, D_FF, FF_CHUNK):
        a = jnp.dot(xb, w13_ref[:, lo:lo + FF_CHUNK], preferred_element_type=F32)
        b = jnp.dot(xb, w13_ref[:, D_FF + lo:D_FF + lo + FF_CHUNK], preferred_element_type=F32)
        h = (jax.nn.silu(a) * b).astype(BF16)
        part = jnp.dot(h, w2_ref[lo:lo + FF_CHUNK, :], preferred_element_type=F32)
        acc = part if acc is None else acc + part
    o_ref[...] = _layer_norm(ALPHA * x + 0.5 * acc, g_ref[...], b_ref[...])


def _ffn_ln(x, w13, w2, g, b):
    m = x.shape[0]
    row = pl.BlockSpec((ROW_TILE, D_MODEL), lambda i: (i, 0))
    return pl.pallas_call(
        _ffn_ln_kernel,
        grid=(m // ROW_TILE,),
        in_specs=[row, _resident(w13.shape), _resident(w2.shape),
                  _resident((1, D_MODEL)), _resident((1, D_MODEL))],
        out_specs=row,
        out_shape=jax.ShapeDtypeStruct((m, D_MODEL), F32),
        compiler_params=_params("parallel"),
        name="ffn_ln",
    )(x, w13, w2, g, b)


def _inproj_kernel(x_ref, w_ref, o_ref, gate_ref):
    y = jnp.dot(x_ref[...].astype(BF16), w_ref[...], preferred_element_type=F32)
    o_ref[...] = y.astype(BF16)
    gate_ref[...] = y[:, RG_BLK * LANES:RG_BLK * LANES + RET_WIDTH]


def _inproj(x, w):
    m = x.shape[0]
    return pl.pallas_call(
        _inproj_kernel,
        grid=(m // ROW_TILE,),
        in_specs=[pl.BlockSpec((ROW_TILE, D_MODEL), lambda i: (i, 0)), _resident(w.shape)],
        out_specs=[pl.BlockSpec((ROW_TILE, IN_WIDTH), lambda i: (i, 0)),
                   pl.BlockSpec((ROW_TILE, RET_WIDTH), lambda i: (i, 0))],
        out_shape=[jax.ShapeDtypeStruct((m, IN_WIDTH), BF16),
                   jax.ShapeDtypeStruct((m, RET_WIDTH), F32)],
        compiler_params=_params("parallel"),
        name="inproj",
    )(x, w)


def _ret_kernel(q_ref, k_ref, v_ref, gate_ref, dec_ref, gng_ref, gnb_ref, o_ref,
                sf_ref, sb_ref, *, n_chunks):
    c_len = CHUNK
    k_scale = RET_DK ** -0.5
    lane = lax.broadcasted_iota(jnp.int32, (1, LANES), 1)
    lo_half = lane < RET_DK
    m_lo = lo_half.astype(F32)
    m_hi = 1.0 - m_lo
    lg = jax.nn.log_sigmoid(dec_ref[0])
    lgf0, lgf1, lgb0, lgb1 = lg[0:1], lg[1:2], lg[2:3], lg[3:4]
    lgf = jnp.where(lo_half, lgf0, lgf1)
    lgb = jnp.where(lo_half, lgb0, lgb1)

    ri = lax.broadcasted_iota(jnp.int32, (c_len, c_len), 0)
    ci = lax.broadcasted_iota(jnp.int32, (c_len, c_len), 1)
    dist = (ri - ci).astype(F32)
    causal = dist >= 0.0
    block_diag = (ri < RET_DK) == (ci < RET_DK)

    def decay_tile(lf, lb):
        return jnp.where(causal, jnp.exp(lf * jnp.maximum(dist, 0.0)),
                         jnp.exp(lb * jnp.maximum(-dist, 0.0))) * k_scale

    d0 = decay_tile(lgf0, lgb0)
    d1 = decay_tile(lgf1, lgb1)
    row = lax.broadcasted_iota(jnp.int32, (c_len, LANES), 0).astype(F32)
    qdf = jnp.exp(lgf * (row + 1.0))
    qdb = jnp.exp(lgb * (c_len - row))
    kdf = jnp.exp(lgf * (c_len - 1.0 - row)) * k_scale
    kdb = jnp.exp(lgb * row) * k_scale
    gcf = jnp.exp(lgf * c_len)
    gcb = jnp.exp(lgb * c_len)

    def rows_of(c):
        return pl.ds(pl.multiple_of(c * c_len, c_len), c_len)

    def kv_body(c, carry):
        rows = rows_of(c)
        k = k_ref[0, rows, :].astype(F32)
        v = v_ref[0, rows, :]
        kvf = jnp.dot((k * kdf).T.astype(BF16), v, preferred_element_type=F32)
        kvb = jnp.dot((k * kdb).T.astype(BF16), v, preferred_element_type=F32)
        sf_ref[c] = jnp.where(block_diag, kvf, 0.0)
        sb_ref[c] = jnp.where(block_diag, kvb, 0.0)
        return carry

    lax.fori_loop(0, n_chunks, kv_body, 0, unroll=8)

    def fwd_scan(c, s):
        kv = sf_ref[c]
        sf_ref[c] = s
        return s * gcf + kv

    lax.fori_loop(0, n_chunks, fwd_scan, jnp.zeros((c_len, LANES), F32))

    def bwd_scan(i, s):
        c = n_chunks - 1 - i
        kv = sb_ref[c]
        sb_ref[c] = s
        return s * gcb + kv

    lax.fori_loop(0, n_chunks, bwd_scan, jnp.zeros((c_len, LANES), F32))

    gng = gng_ref[...]
    gnb = gnb_ref[...]
    inv_dv = 1.0 / RET_DK

    def out_body(c, carry):
        rows = rows_of(c)
        q = q_ref[0, rows, :].astype(F32)
        k = k_ref[0, rows, :]
        v = v_ref[0, rows, :].astype(F32)
        s0 = lax.dot_general((q * m_lo).astype(BF16), k, NT_DIMS, preferred_element_type=F32)
        s1 = lax.dot_general((q * m_hi).astype(BF16), k, NT_DIMS, preferred_element_type=F32)
        o = jnp.dot((s0 * d0).astype(BF16), (v * m_lo).astype(BF16), preferred_element_type=F32)
        o = o + jnp.dot((s1 * d1).astype(BF16), (v * m_hi).astype(BF16), preferred_element_type=F32)
        o = o + jnp.dot((q * qdf).astype(BF16), sf_ref[c].astype(BF16), preferred_element_type=F32)
        o = o + jnp.dot((q * qdb).astype(BF16), sb_ref[c].astype(BF16), preferred_element_type=F32)
        s_lo = jnp.sum(o * m_lo, axis=-1, keepdims=True)
        s_hi = jnp.sum(o * m_hi, axis=-1, keepdims=True)
        d = o - jnp.where(lo_half, s_lo, s_hi) * inv_dv
        dd = d * d
        v_lo = jnp.sum(dd * m_lo, axis=-1, keepdims=True)
        v_hi = jnp.sum(dd * m_hi, axis=-1, keepdims=True)
        var = jnp.where(lo_half, v_lo, v_hi) * inv_dv
        y = d * lax.rsqrt(var + EPS) * gng + gnb
        o_ref[0, rows, :] = (y * jax.nn.silu(gate_ref[0, rows, :])).astype(BF16)
        return carry

    lax.fori_loop(0, n_chunks, out_body, 0, unroll=8)


def _retention(qkv, gate, dec, gng, gnb):
    bsz, t, _ = qkv.shape
    n_chunks = t // CHUNK
    n_pairs = RET_WIDTH // LANES

    def col(blk):
        return pl.BlockSpec((1, t, LANES), lambda b, p: (b, 0, blk + p))

    return pl.pallas_call(
        functools.partial(_ret_kernel, n_chunks=n_chunks),
        grid=(bsz, n_pairs),
        in_specs=[col(RQ_BLK), col(RK_BLK), col(RV_BLK),
                  pl.BlockSpec((1, t, LANES), lambda b, p: (b, 0, p)),
                  pl.BlockSpec((1, 4, LANES), lambda b, p: (p, 0, 0)),
                  pl.BlockSpec((1, LANES), lambda b, p: (0, p)),
                  pl.BlockSpec((1, LANES), lambda b, p: (0, p))],
        out_specs=pl.BlockSpec((1, t, LANES), lambda b, p: (b, 0, p)),
        out_shape=jax.ShapeDtypeStruct((bsz, t, RET_WIDTH), BF16),
        scratch_shapes=[pltpu.VMEM((n_chunks, CHUNK, LANES), F32),
                        pltpu.VMEM((n_chunks, CHUNK, LANES), F32)],
        compiler_params=_params("parallel", "parallel"),
        name="retention",
    )(qkv, qkv, qkv, gate, dec, gng, gnb)


def _split3(x):
    x1 = x.astype(BF16).astype(F32)
    r = x - x1
    x2 = r.astype(BF16).astype(F32)
    x3 = (r - x2).astype(BF16).astype(F32)
    return x1, x2, x3


def _diff_kernel(slopes_ref, trips_ref, q_ref, k_ref, v_ref, kfeat_ref, lam_ref, g_ref, o_ref,
                 vt_ref, srel_ref, s_even_ref, s_odd_ref, acc_ref, qaug_ref, *, n_kv, n_q, lam0):
    c = slopes_ref[pl.program_id(1)] * LOG2E

    ones_row = (lax.broadcasted_iota(jnp.int32, (VT_PAD, KV_TILE), 0) == 0).astype(F32)
    for j in range(n_kv):
        vt = v_ref[0, j * KV_TILE:(j + 1) * KV_TILE, :].astype(F32).T
        vt_ref[j] = jnp.concatenate([vt, ones_row], axis=0).astype(BF16)
    kio = lax.broadcasted_iota(jnp.int32, (KV_TILE, Q_TILE), 0)
    qio = lax.broadcasted_iota(jnp.int32, (KV_TILE, Q_TILE), 1)
    srel_ref[...] = c * (kio - qio).astype(F32)

    lv = lam_ref[...]
    lam = (jnp.exp(jnp.sum(lv[0:1] * lv[1:2], axis=-1, keepdims=True))
           - jnp.exp(jnp.sum(lv[2:3] * lv[3:4], axis=-1, keepdims=True)) + lam0)
    out_gain = g_ref[...] * (1.0 - lam0)

    feat = lax.broadcasted_iota(jnp.int32, (LANES, 1), 0)
    rowi = lax.broadcasted_iota(jnp.int32, (AUG_ROWS, Q_TILE), 0)
    lane_q = lax.broadcasted_iota(jnp.int32, (1, Q_TILE), 1)
    c_row = jnp.full((1, Q_TILE), c, F32)
    c_parts = _split3(c_row)
    bufs = (s_even_ref, s_odd_ref)

    def query_rows(qs):
        return pl.ds(pl.multiple_of(qs * Q_TILE, Q_TILE), Q_TILE)

    def key_rows(j):
        return pl.ds(pl.multiple_of(j * KV_TILE, KV_TILE), KV_TILE)

    def diag_tile(qs):
        return (qs * Q_TILE) // KV_TILE

    def tile_of(qs, pos):
        jd = diag_tile(qs)
        return jnp.where(pos == 0, jd, jnp.where(pos - 1 < jd, pos - 1, pos))

    def prepare_queries(qs, slot):
        qt = (q_ref[0, query_rows(qs), :].astype(F32) * (LOG2E * DIFF_DH ** -0.5)).T
        qpos = (qs * Q_TILE + lane_q).astype(F32)
        t_parts = _split3(-(c_row * qpos))
        coef = [float(FEAT_RADIX) * cp for cp in c_parts] + list(c_parts) + list(t_parts)
        aug = jnp.zeros((AUG_ROWS, Q_TILE), F32)
        for r, row in enumerate(coef):
            aug = jnp.where(rowi == r, row, aug)
        tail = jnp.concatenate([aug.astype(BF16),
                                jnp.zeros((LANES - AUG_ROWS, Q_TILE), BF16)], axis=0)
        qaug_ref[slot, 0] = jnp.concatenate([jnp.where(feat < DIFF_DH, qt, 0.0).astype(BF16), tail], axis=0)
        qaug_ref[slot, 1] = jnp.concatenate([jnp.where(feat < DIFF_DH, 0.0, qt).astype(BF16), tail], axis=0)

    def finish_scores(s_maps, dst_ref):
        col_max = []
        for mp in range(2):
            dst_ref[mp] = s_maps[mp]
            col_max.append(jnp.max(s_maps[mp], axis=0, keepdims=True))
        return tuple(col_max)

    def diag_score_stage(qs, slot, dst_ref):
        jd = diag_tile(qs)
        k_t = k_ref[0, key_rows(jd), :]
        shift = c * jnp.asarray(jd * KV_TILE - qs * Q_TILE, F32)
        bias = -jnp.abs(srel_ref[...] + shift)
        return finish_scores([jnp.dot(k_t, qaug_ref[slot, mp, 0:LANES, :],
                                      preferred_element_type=F32) + bias
                              for mp in range(2)], dst_ref)

    def score_stage(qs, slot, pos, dst_ref):
        j = tile_of(qs, pos)
        side = (j > diag_tile(qs)).astype(jnp.int32)
        lhs = jnp.concatenate([k_ref[0, key_rows(j), :], kfeat_ref[side, key_rows(j), :]], axis=1)
        return finish_scores([jnp.dot(lhs, qaug_ref[slot, mp], preferred_element_type=F32)
                              for mp in range(2)], dst_ref)

    def softmax_stage(qs, slot, pos, src_ref, col_max, m_run):
        vt_t = vt_ref[tile_of(qs, pos)]
        new = []
        for mp in range(2):
            m_new = jnp.maximum(m_run[mp], col_max[mp])
            alpha = jnp.exp2(m_run[mp] - m_new)
            p = jnp.exp2(src_ref[mp] - m_new).astype(BF16)
            acc_ref[slot, mp] = (alpha * acc_ref[slot, mp]
                                 + jnp.dot(vt_t, p, preferred_element_type=F32))
            new.append(m_new)
        return tuple(new)

    def write_output(qs, slot):
        a0 = acc_ref[slot, 0]
        a1 = acc_ref[slot, 1]
        att = (a0[:DIFF_DV] / a0[DIFF_DV:DIFF_DV + 1]
               - lam * (a1[:DIFF_DV] / a1[DIFF_DV:DIFF_DV + 1]))
        ms = jnp.mean(att * att, axis=0, keepdims=True)
        att = (att * lax.rsqrt(ms + EPS)).T
        o_ref[0, query_rows(qs), :] = (att * out_gain).astype(BF16)

    def query_tile_body(qs, cm):
        slot = qs % 2
        prv = jnp.maximum(qs - 1, 0)
        nxt = jnp.minimum(qs + 1, n_q - 1)
        acc_ref[slot] = jnp.zeros(acc_ref.shape[1:], F32)
        m_init = jnp.full((1, Q_TILE), -1e30, F32)

        def pair_body(i, carry):
            m_run, cm_even = carry
            cm_odd = score_stage(qs, slot, 2 * i + 1, s_odd_ref)
            m_run = softmax_stage(qs, slot, 2 * i, s_even_ref, cm_even, m_run)
            cm_even = score_stage(qs, slot, 2 * i + 2, s_even_ref)
            m_run = softmax_stage(qs, slot, 2 * i + 1, s_odd_ref, cm_odd, m_run)
            return m_run, cm_even

        m_run, cm_even = lax.fori_loop(0, trips_ref[0], pair_body, ((m_init, m_init), cm))
        write_output(prv, 1 - slot)
        cm_odd = score_stage(qs, slot, n_kv - 1, s_odd_ref)
        m_run = softmax_stage(qs, slot, n_kv - 2, s_even_ref, cm_even, m_run)
        prepare_queries(nxt, 1 - slot)
        cm_next = diag_score_stage(nxt, 1 - slot, s_even_ref)
        softmax_stage(qs, slot, n_kv - 1, s_odd_ref, cm_odd, m_run)
        return cm_next

    acc_ref[...] = jnp.ones(acc_ref.shape, F32)
    prepare_queries(0, 0)
    lax.fori_loop(0, n_q, query_tile_body, diag_score_stage(0, 0, bufs[0]))
    write_output(n_q - 1, (n_q - 1) % 2)


def _key_features(t):
    pos = jnp.arange(t, dtype=jnp.int32)
    hi = (pos // FEAT_RADIX).astype(F32)
    lo = (pos % FEAT_RADIX).astype(F32)
    one = jnp.ones((t,), F32)
    cols = jnp.stack([hi, hi, hi, lo, lo, lo, one, one, one], axis=1)
    cols = jnp.pad(cols, ((0, 0), (0, LANES - cols.shape[1])))
    return jnp.stack([cols, -cols]).astype(BF16)


def _diff_attention(qkv, slopes, lam_rows, subln_g, lam0):
    bsz, t, _ = qkv.shape
    n_kv = t // KV_TILE
    assert n_kv % 2 == 0 and KV_TILE % Q_TILE == 0 and t <= FEAT_RADIX * 256

    def col(blk):
        return pl.BlockSpec((1, t, LANES), lambda b, h, *_: (b, 0, blk + h))

    return pl.pallas_call(
        functools.partial(_diff_kernel, n_kv=n_kv, n_q=t // Q_TILE, lam0=lam0),
        grid_spec=pltpu.PrefetchScalarGridSpec(
            num_scalar_prefetch=2,
            grid=(bsz, DIFF_HEADS),
            in_specs=[col(DQ_BLK), col(DK_BLK), col(DV_BLK),
                      pl.BlockSpec((2, t, LANES), lambda b, h, *_: (0, 0, 0),
                                   pipeline_mode=pl.Buffered(1)),
                      pl.BlockSpec((8, LANES), lambda b, h, *_: (0, 0)),
                      pl.BlockSpec((1, LANES), lambda b, h, *_: (0, 0))],
            out_specs=pl.BlockSpec((1, t, LANES), lambda b, h, *_: (b, 0, h)),
            scratch_shapes=[pltpu.VMEM((n_kv, DIFF_DV + VT_PAD, KV_TILE), BF16),
                            pltpu.VMEM((KV_TILE, Q_TILE), F32),
                            pltpu.VMEM((2, KV_TILE, Q_TILE), F32),
                            pltpu.VMEM((2, KV_TILE, Q_TILE), F32),
                            pltpu.VMEM((2, 2, DIFF_DV + VT_PAD, Q_TILE), F32),
                            pltpu.VMEM((2, 2, 2 * LANES, Q_TILE), BF16)]),
        out_shape=jax.ShapeDtypeStruct((bsz, t, DIFF_WIDTH), BF16),
        compiler_params=_params("parallel", "parallel"),
        name="diff_attention",
    )(slopes, jnp.full((1,), n_kv // 2 - 1, jnp.int32), qkv, qkv, qkv, _key_features(t),
      lam_rows, subln_g)


def _outproj_ln_kernel(x_ref, r_ref, d_ref, w_ref, g_ref, b_ref, o_ref):
    mix = jnp.dot(r_ref[...], w_ref[0:RET_WIDTH, :], preferred_element_type=F32)
    mix = mix + jnp.dot(d_ref[...], w_ref[RET_WIDTH:, :], preferred_element_type=F32)
    o_ref[...] = _layer_norm(ALPHA * x_ref[...] + mix, g_ref[...], b_ref[...])


def _outproj_ln(x, ret, dif, w, g, b):
    m = x.shape[0]
    row = pl.BlockSpec((ROW_TILE, D_MODEL), lambda i: (i, 0))
    half = pl.BlockSpec((ROW_TILE, RET_WIDTH), lambda i: (i, 0))
    return pl.pallas_call(
        _outproj_ln_kernel,
        grid=(m // ROW_TILE,),
        in_specs=[row, half, half, _resident(w.shape),
                  _resident((1, D_MODEL)), _resident((1, D_MODEL))],
        out_specs=row,
        out_shape=jax.ShapeDtypeStruct((m, D_MODEL), F32),
        compiler_params=_params("parallel"),
        name="outproj_ln",
    )(x, ret, dif, w, g, b)


def _lambda_init(layer):
    return 0.8 - 0.6 * math.exp(-0.3 * layer)


def _trunk(x, p):
    bsz, t, _ = x.shape
    m = bsz * t
    xf = x.reshape(m, D_MODEL)
    for l in range(DEPTH):
        xf = _ffn_ln(xf, p["ffn1_w13"][l], p["ffn1_w2"][l], p["ln1_g"][l], p["ln1_b"][l])
        qkv, gate = _inproj(xf, p["w_in"][l])
        qkv = qkv.reshape(bsz, t, IN_WIDTH)
        ret = _retention(qkv, gate.reshape(bsz, t, RET_WIDTH), p["ret_dec"][l],
                         p["ret_gn_g"][l], p["ret_gn_b"][l])
        dif = _diff_attention(qkv, p["slopes"], p["lam_rows"][l], p["diff_subln_g"][l],
                              _lambda_init(l))
        xf = _outproj_ln(xf, ret.reshape(m, RET_WIDTH), dif.reshape(m, DIFF_WIDTH),
                         p["w_out"][l], p["ln2_g"][l], p["ln2_b"][l])
        xf = _ffn_ln(xf, p["ffn2_w13"][l], p["ffn2_w2"][l], p["ln3_g"][l], p["ln3_b"][l])
    return xf.reshape(bsz, t, D_MODEL)


def kernel(x_prompt, x_sample, w_in, w_out, ret_decay_f, ret_decay_b, ret_gn_g, ret_gn_b,
           diff_lq1, diff_lk1, diff_lq2, diff_lk2, diff_subln_g,
           ffn1_w13, ffn1_w2, ffn2_w13, ffn2_w2,
           ln1_g, ln1_b, ln2_g, ln2_b, ln3_g, ln3_b):
    n_pairs = RET_WIDTH // LANES

    def pair_rows(dec):
        return jnp.broadcast_to(dec.reshape(DEPTH, n_pairs, 2, 1), (DEPTH, n_pairs, 2, LANES))

    row = lambda a: a.reshape(DEPTH, 1, -1)
    lam_rows = jnp.stack([diff_lq1, diff_lk1, diff_lq2, diff_lk2], axis=1)
    lam_rows = jnp.pad(lam_rows, ((0, 0), (0, 4), (0, LANES - DIFF_DH)))
    p = {
        "w_in": w_in.astype(BF16), "w_out": w_out.astype(BF16),
        "ffn1_w13": ffn1_w13.astype(BF16), "ffn1_w2": ffn1_w2.astype(BF16),
        "ffn2_w13": ffn2_w13.astype(BF16), "ffn2_w2": ffn2_w2.astype(BF16),
        "ret_dec": jnp.concatenate([pair_rows(ret_decay_f), pair_rows(ret_decay_b)], axis=2),
        "ret_gn_g": row(ret_gn_g), "ret_gn_b": row(ret_gn_b),
        "lam_rows": lam_rows, "diff_subln_g": row(diff_subln_g),
        "slopes": jnp.asarray([2.0 ** (-8.0 * (h + 1) / DIFF_HEADS) for h in range(DIFF_HEADS)], F32),
        "ln1_g": row(ln1_g), "ln1_b": row(ln1_b), "ln2_g": row(ln2_g), "ln2_b": row(ln2_b),
        "ln3_g": row(ln3_g), "ln3_b": row(ln3_b),
    }
    return _trunk(x_prompt, p), _trunk(x_sample, p)
```

```python
import functools
import math

import jax
import jax.numpy as jnp
from jax import lax
from jax.experimental import pallas as pl
from jax.experimental.pallas import tpu as pltpu

D_MODEL = 1024
DEPTH = 2
RET_HEADS = 8
RET_DK = 64
RET_WIDTH = 512
DIFF_HEADS = 4
DIFF_DH = 64
DIFF_DV = 128
DIFF_WIDTH = 512
IN_WIDTH = 3584
D_FF = 2816
CHUNK = 128
EPS = 1e-5
ALPHA = (2 * DEPTH) ** 0.25

LANES = 128
FF_CHUNK = 1408
ROW_TILE = 512
Q_TILE = 512
KV_TILE = 512
VT_PAD = 16
AUG_ROWS = 16
FEAT_RADIX = 64
LOG2E = math.log2(math.e)
VMEM_LIMIT = 56 * 1024 * 1024

RQ_BLK, RK_BLK, RV_BLK, RG_BLK, DQ_BLK, DK_BLK, DV_BLK = 0, 4, 8, 12, 16, 20, 24

BF16 = jnp.bfloat16
F32 = jnp.float32
NT_DIMS = (((1,), (1,)), ((), ()))


def _resident(shape):
    nd = len(shape)
    return pl.BlockSpec(shape, lambda *_: (0,) * nd, pipeline_mode=pl.Buffered(1))


def _layer_norm(y, g, b):
    mu = jnp.mean(y, axis=-1, keepdims=True)
    d = y - mu
    var = jnp.mean(d * d, axis=-1, keepdims=True)
    return d * lax.rsqrt(var + EPS) * g + b


def _params(*sem):
    return pltpu.CompilerParams(dimension_semantics=sem, vmem_limit_bytes=VMEM_LIMIT)


def _ffn_residual_ln(x, w13_ref, w2_ref, g_ref, b_ref):
    xb = x.astype(BF16)
    acc = None
    for lo in range(0, D_FF, FF_CHUNK):
        a = jnp.dot(xb, w13_ref[:, lo:lo + FF_CHUNK], preferred_element_type=F32)
        b = jnp.dot(xb, w13_ref[:, D_FF + lo:D_FF + lo + FF_CHUNK], preferred_element_type=F32)
        h = (jax.nn.silu(a) * b).astype(BF16)
        part = jnp.dot(h, w2_ref[lo:lo + FF_CHUNK, :], preferred_element_type=F32)
        acc = part if acc is None else acc + part
    return _layer_norm(ALPHA * x + 0.5 * acc, g_ref[...], b_ref[...])


def _ffn_ln_kernel(x_ref, w13_ref, w2_ref, g_ref, b_ref, o_ref):
    o_ref[...] = _ffn_residual_ln(x_ref[...], w13_ref, w2_ref, g_ref, b_ref)


def _mix_ffn_ln_kernel(x_ref, r_ref, d_ref, wo_ref, gm_ref, bm_ref,
                       w13_ref, w2_ref, g_ref, b_ref, o_ref):
    mix = jnp.dot(r_ref[...], wo_ref[0:RET_WIDTH, :], preferred_element_type=F32)
    mix = mix + jnp.dot(d_ref[...], wo_ref[RET_WIDTH:, :], preferred_element_type=F32)
    x = _layer_norm(ALPHA * x_ref[...] + mix, gm_ref[...], bm_ref[...])
    o_ref[...] = _ffn_residual_ln(x, w13_ref, w2_ref, g_ref, b_ref)


def _ffn_ln(x, w13, w2, g, b, mix=None):
    m = x.shape[0]
    row = pl.BlockSpec((ROW_TILE, D_MODEL), lambda i: (i, 0))
    vec = _resident((1, D_MODEL))
    ffn_specs = [_resident(w13.shape), _resident(w2.shape), vec, vec]
    if mix is None:
        body, args, specs = _ffn_ln_kernel, (x, w13, w2, g, b), [row] + ffn_specs
    else:
        ret, dif, w_out, gm, bm = mix
        half = pl.BlockSpec((ROW_TILE, RET_WIDTH), lambda i: (i, 0))
        body = _mix_ffn_ln_kernel
        args = (x, ret, dif, w_out, gm, bm, w13, w2, g, b)
        specs = [row, half, half, _resident(w_out.shape), vec, vec] + ffn_specs
    return pl.pallas_call(
        body,
        grid=(m // ROW_TILE,),
        in_specs=specs,
        out_specs=row,
        out_shape=jax.ShapeDtypeStruct((m, D_MODEL), F32),
        compiler_params=_params("parallel"),
        name="ffn_ln" if mix is None else "mix_ffn_ln",
    )(*args)


def _inproj_kernel(x_ref, w_ref, o_ref, gate_ref):
    y = jnp.dot(x_ref[...].astype(BF16), w_ref[...], preferred_element_type=F32)
    o_ref[...] = y.astype(BF16)
    gate_ref[...] = y[:, RG_BLK * LANES:RG_BLK * LANES + RET_WIDTH]


def _inproj(x, w):
    m = x.shape[0]
    return pl.pallas_call(
        _inproj_kernel,
        grid=(m // ROW_TILE,),
        in_specs=[pl.BlockSpec((ROW_TILE, D_MODEL), lambda i: (i, 0)), _resident(w.shape)],
        out_specs=[pl.BlockSpec((ROW_TILE, IN_WIDTH), lambda i: (i, 0)),
                   pl.BlockSpec((ROW_TILE, RET_WIDTH), lambda i: (i, 0))],
        out_shape=[jax.ShapeDtypeStruct((m, IN_WIDTH), BF16),
                   jax.ShapeDtypeStruct((m, RET_WIDTH), F32)],
        compiler_params=_params("parallel"),
        name="inproj",
    )(x, w)


def _ret_kernel(q_ref, k_ref, v_ref, gate_ref, dec_ref, gng_ref, gnb_ref, o_ref,
                sf_ref, sb_ref, *, n_chunks):
    c_len = CHUNK
    k_scale = RET_DK ** -0.5
    lane = lax.broadcasted_iota(jnp.int32, (1, LANES), 1)
    lo_half = lane < RET_DK
    m_lo = lo_half.astype(F32)
    m_hi = 1.0 - m_lo
    lg = jax.nn.log_sigmoid(dec_ref[0])
    lgf0, lgf1, lgb0, lgb1 = lg[0:1], lg[1:2], lg[2:3], lg[3:4]
    lgf = jnp.where(lo_half, lgf0, lgf1)
    lgb = jnp.where(lo_half, lgb0, lgb1)

    ri = lax.broadcasted_iota(jnp.int32, (c_len, c_len), 0)
    ci = lax.broadcasted_iota(jnp.int32, (c_len, c_len), 1)
    dist = (ri - ci).astype(F32)
    causal = dist >= 0.0
    block_diag = (ri < RET_DK) == (ci < RET_DK)

    def decay_tile(lf, lb):
        return jnp.where(causal, jnp.exp(lf * jnp.maximum(dist, 0.0)),
                         jnp.exp(lb * jnp.maximum(-dist, 0.0))) * k_scale

    d0 = decay_tile(lgf0, lgb0)
    d1 = decay_tile(lgf1, lgb1)
    row = lax.broadcasted_iota(jnp.int32, (c_len, LANES), 0).astype(F32)
    qdf = jnp.exp(lgf * (row + 1.0))
    qdb = jnp.exp(lgb * (c_len - row))
    kdf = jnp.exp(lgf * (c_len - 1.0 - row)) * k_scale
    kdb = jnp.exp(lgb * row) * k_scale
    gcf = jnp.exp(lgf * c_len)
    gcb = jnp.exp(lgb * c_len)

    def rows_of(c):
        return pl.ds(pl.multiple_of(c * c_len, c_len), c_len)

    def kv_body(c, carry):
        rows = rows_of(c)
        k = k_ref[0, rows, :].astype(F32)
        v = v_ref[0, rows, :]
        kvf = jnp.dot((k * kdf).T.astype(BF16), v, preferred_element_type=F32)
        kvb = jnp.dot((k * kdb).T.astype(BF16), v, preferred_element_type=F32)
        sf_ref[c] = jnp.where(block_diag, kvf, 0.0)
        sb_ref[c] = jnp.where(block_diag, kvb, 0.0)
        return carry

    lax.fori_loop(0, n_chunks, kv_body, 0, unroll=8)

    def fwd_scan(c, s):
        kv = sf_ref[c]
        sf_ref[c] = s
        return s * gcf + kv

    lax.fori_loop(0, n_chunks, fwd_scan, jnp.zeros((c_len, LANES), F32))

    def bwd_scan(i, s):
        c = n_chunks - 1 - i
        kv = sb_ref[c]
        sb_ref[c] = s
        return s * gcb + kv

    lax.fori_loop(0, n_chunks, bwd_scan, jnp.zeros((c_len, LANES), F32))

    gng = gng_ref[...]
    gnb = gnb_ref[...]
    inv_dv = 1.0 / RET_DK

    def out_body(c, carry):
        rows = rows_of(c)
        q = q_ref[0, rows, :].astype(F32)
        k = k_ref[0, rows, :]
        v = v_ref[0, rows, :].astype(F32)
        s0 = lax.dot_general((q * m_lo).astype(BF16), k, NT_DIMS, preferred_element_type=F32)
        s1 = lax.dot_general((q * m_hi).astype(BF16), k, NT_DIMS, preferred_element_type=F32)
        o = jnp.dot((s0 * d0).astype(BF16), (v * m_lo).astype(BF16), preferred_element_type=F32)
        o = o + jnp.dot((s1 * d1).astype(BF16), (v * m_hi).astype(BF16), preferred_element_type=F32)
        o = o + jnp.dot((q * qdf).astype(BF16), sf_ref[c].astype(BF16), preferred_element_type=F32)
        o = o + jnp.dot((q * qdb).astype(BF16), sb_ref[c].astype(BF16), preferred_element_type=F32)
        s_lo = jnp.sum(o * m_lo, axis=-1, keepdims=True)
        s_hi = jnp.sum(o * m_hi, axis=-1, keepdims=True)
        d = o - jnp.where(lo_half, s_lo, s_hi) * inv_dv
        dd = d * d
        v_lo = jnp.sum(dd * m_lo, axis=-1, keepdims=True)
        v_hi = jnp.sum(dd * m_hi, axis=-1, keepdims=True)
        var = jnp.where(lo_half, v_lo, v_hi) * inv_dv
        y = d * lax.rsqrt(var + EPS) * gng + gnb
        o_ref[0, rows, :] = (y * jax.nn.silu(gate_ref[0, rows, :])).astype(BF16)
        return carry

    lax.fori_loop(0, n_chunks, out_body, 0, unroll=8)


def _retention(qkv, gate, dec, gng, gnb):
    bsz, t, _ = qkv.shape
    n_chunks = t // CHUNK
    n_pairs = RET_WIDTH // LANES

    def col(blk):
        return pl.BlockSpec((1, t, LANES), lambda b, p: (b, 0, blk + p))

    return pl.pallas_call(
        functools.partial(_ret_kernel, n_chunks=n_chunks),
        grid=(bsz, n_pairs),
        in_specs=[col(RQ_BLK), col(RK_BLK), col(RV_BLK),
                  pl.BlockSpec((1, t, LANES), lambda b, p: (b, 0, p)),
                  pl.BlockSpec((1, 4, LANES), lambda b, p: (p, 0, 0)),
                  pl.BlockSpec((1, LANES), lambda b, p: (0, p)),
                  pl.BlockSpec((1, LANES), lambda b, p: (0, p))],
        out_specs=pl.BlockSpec((1, t, LANES), lambda b, p: (b, 0, p)),
        out_shape=jax.ShapeDtypeStruct((bsz, t, RET_WIDTH), BF16),
        scratch_shapes=[pltpu.VMEM((n_chunks, CHUNK, LANES), F32),
                        pltpu.VMEM((n_chunks, CHUNK, LANES), F32)],
        compiler_params=_params("parallel", "parallel"),
        name="retention",
    )(qkv, qkv, qkv, gate, dec, gng, gnb)


def _split3(x):
    x1 = x.astype(BF16).astype(F32)
    r = x - x1
    x2 = r.astype(BF16).astype(F32)
    x3 = (r - x2).astype(BF16).astype(F32)
    return x1, x2, x3


def _diff_kernel(slopes_ref, trips_ref, q_ref, k_ref, v_ref, kfeat_ref, lam_ref, g_ref, o_ref,
                 vt_ref, dbias_ref, s_even_ref, s_odd_ref, acc_ref, qaug_ref, *, n_kv, n_q, lam0):
    c = slopes_ref[pl.program_id(1)] * LOG2E

    ones_row = (lax.broadcasted_iota(jnp.int32, (VT_PAD, KV_TILE), 0) == 0).astype(F32)
    for j in range(n_kv):
        vt = v_ref[0, j * KV_TILE:(j + 1) * KV_TILE, :].astype(F32).T
        vt_ref[j] = jnp.concatenate([vt, ones_row], axis=0).astype(BF16)
    kio = lax.broadcasted_iota(jnp.int32, (KV_TILE, Q_TILE), 0)
    qio = lax.broadcasted_iota(jnp.int32, (KV_TILE, Q_TILE), 1)
    dbias_ref[...] = -(c * jnp.abs(kio - qio).astype(F32))

    lv = lam_ref[...]
    lam = (jnp.exp(jnp.sum(lv[0:1] * lv[1:2], axis=-1, keepdims=True))
           - jnp.exp(jnp.sum(lv[2:3] * lv[3:4], axis=-1, keepdims=True)) + lam0)
    out_gain = g_ref[...] * (1.0 - lam0)

    feat = lax.broadcasted_iota(jnp.int32, (LANES, 1), 0)
    rowi = lax.broadcasted_iota(jnp.int32, (AUG_ROWS, Q_TILE), 0)
    lane_q = lax.broadcasted_iota(jnp.int32, (1, Q_TILE), 1)
    c_row = jnp.full((1, Q_TILE), c, F32)
    c_parts = _split3(c_row)
    bufs = (s_even_ref, s_odd_ref)

    def query_rows(qs):
        return pl.ds(pl.multiple_of(qs * Q_TILE, Q_TILE), Q_TILE)

    def key_rows(j):
        return pl.ds(pl.multiple_of(j * KV_TILE, KV_TILE), KV_TILE)

    def diag_tile(qs):
        return qs

    def tile_of(qs, pos):
        jd = diag_tile(qs)
        return jnp.where(pos == 0, jd, jnp.where(pos - 1 < jd, pos - 1, pos))

    def prepare_queries(qs, slot):
        qt = (q_ref[0, query_rows(qs), :].astype(F32) * (LOG2E * DIFF_DH ** -0.5)).T
        qpos = (qs * Q_TILE + lane_q).astype(F32)
        t_parts = _split3(-(c_row * qpos))
        coef = [float(FEAT_RADIX) * cp for cp in c_parts] + list(c_parts) + list(t_parts)
        aug = jnp.zeros((AUG_ROWS, Q_TILE), F32)
        for r, row in enumerate(coef):
            aug = jnp.where(rowi == r, row, aug)
        tail = jnp.concatenate([aug.astype(BF16),
                                jnp.zeros((LANES - AUG_ROWS, Q_TILE), BF16)], axis=0)
        qaug_ref[slot, 0] = jnp.concatenate([jnp.where(feat < DIFF_DH, qt, 0.0).astype(BF16), tail], axis=0)
        qaug_ref[slot, 1] = jnp.concatenate([jnp.where(feat < DIFF_DH, 0.0, qt).astype(BF16), tail], axis=0)

    def diag_scores(qs, slot):
        k_t = k_ref[0, key_rows(diag_tile(qs)), :]
        return lambda mp: jnp.dot(k_t, qaug_ref[slot, mp, 0:LANES, :],
                                  preferred_element_type=F32) + dbias_ref[...]

    def offdiag_scores(qs, slot, pos):
        j = tile_of(qs, pos)
        side = (j > diag_tile(qs)).astype(jnp.int32)
        lhs = jnp.concatenate([k_ref[0, key_rows(j), :], kfeat_ref[side, key_rows(j), :]], axis=1)
        return lambda mp: jnp.dot(lhs, qaug_ref[slot, mp], preferred_element_type=F32)

    def score_part(scores, mp, dst_ref):
        s = scores(mp)
        dst_ref[mp] = s
        return jnp.max(s, axis=0, keepdims=True)

    def softmax_part(slot, mp, vt_t, src_ref, col_max, m_old):
        m_new = jnp.maximum(m_old, col_max)
        alpha = jnp.exp2(m_old - m_new)
        p = jnp.exp2(src_ref[mp] - m_new).astype(BF16)
        acc_ref[slot, mp] = (alpha * acc_ref[slot, mp]
                             + jnp.dot(vt_t, p, preferred_element_type=F32))
        return m_new

    def stage_pair(scores, dst_ref, qs, slot, pos, src_ref, col_max, m_run):
        vt_t = vt_ref[tile_of(qs, pos)]
        new_cm, new_m = [], []
        for mp in range(2):
            new_cm.append(score_part(scores, mp, dst_ref))
            new_m.append(softmax_part(slot, mp, vt_t, src_ref, col_max[mp], m_run[mp]))
        return tuple(new_cm), tuple(new_m)

    def write_output(qs, slot):
        a0 = acc_ref[slot, 0]
        a1 = acc_ref[slot, 1]
        att = (a0[:DIFF_DV] / a0[DIFF_DV:DIFF_DV + 1]
               - lam * (a1[:DIFF_DV] / a1[DIFF_DV:DIFF_DV + 1]))
        ms = jnp.mean(att * att, axis=0, keepdims=True)
        att = (att * lax.rsqrt(ms + EPS)).T
        o_ref[0, query_rows(qs), :] = (att * out_gain).astype(BF16)

    def query_tile_body(qs, cm):
        slot = qs % 2
        prv = jnp.maximum(qs - 1, 0)
        nxt = jnp.minimum(qs + 1, n_q - 1)
        acc_ref[slot] = jnp.zeros(acc_ref.shape[1:], F32)
        m_init = jnp.full((1, Q_TILE), -1e30, F32)

        def pair_body(i, carry):
            m_run, cm_even = carry
            cm_odd, m_run = stage_pair(offdiag_scores(qs, slot, 2 * i + 1), s_odd_ref,
                                       qs, slot, 2 * i, s_even_ref, cm_even, m_run)
            cm_even, m_run = stage_pair(offdiag_scores(qs, slot, 2 * i + 2), s_even_ref,
                                        qs, slot, 2 * i + 1, s_odd_ref, cm_odd, m_run)
            return m_run, cm_even

        m_run, cm_even = lax.fori_loop(0, trips_ref[0], pair_body, ((m_init, m_init), cm))
        write_output(prv, 1 - slot)
        cm_odd, m_run = stage_pair(offdiag_scores(qs, slot, n_kv - 1), s_odd_ref,
                                   qs, slot, n_kv - 2, s_even_ref, cm_even, m_run)
        prepare_queries(nxt, 1 - slot)
        cm_next, _ = stage_pair(diag_scores(nxt, 1 - slot), s_even_ref,
                                qs, slot, n_kv - 1, s_odd_ref, cm_odd, m_run)
        return cm_next

    acc_ref[...] = jnp.ones(acc_ref.shape, F32)
    prepare_queries(0, 0)
    first = diag_scores(0, 0)
    lax.fori_loop(0, n_q, query_tile_body,
                  tuple(score_part(first, mp, s_even_ref) for mp in range(2)))
    write_output(n_q - 1, (n_q - 1) % 2)


def _key_features(t):
    pos = jnp.arange(t, dtype=jnp.int32)
    hi = (pos // FEAT_RADIX).astype(F32)
    lo = (pos % FEAT_RADIX).astype(F32)
    one = jnp.ones((t,), F32)
    cols = jnp.stack([hi, hi, hi, lo, lo, lo, one, one, one], axis=1)
    cols = jnp.pad(cols, ((0, 0), (0, LANES - cols.shape[1])))
    return jnp.stack([cols, -cols]).astype(BF16)


def _diff_attention(qkv, slopes, lam_rows, subln_g, lam0):
    bsz, t, _ = qkv.shape
    n_kv = t // KV_TILE
    assert n_kv % 2 == 0 and KV_TILE == Q_TILE and t <= FEAT_RADIX * 256

    def col(blk):
        return pl.BlockSpec((1, t, LANES), lambda b, h, *_: (b, 0, blk + h))

    return pl.pallas_call(
        functools.partial(_diff_kernel, n_kv=n_kv, n_q=t // Q_TILE, lam0=lam0),
        grid_spec=pltpu.PrefetchScalarGridSpec(
            num_scalar_prefetch=2,
            grid=(bsz, DIFF_HEADS),
            in_specs=[col(DQ_BLK), col(DK_BLK), col(DV_BLK),
                      pl.BlockSpec((2, t, LANES), lambda b, h, *_: (0, 0, 0),
                                   pipeline_mode=pl.Buffered(1)),
                      pl.BlockSpec((8, LANES), lambda b, h, *_: (0, 0)),
                      pl.BlockSpec((1, LANES), lambda b, h, *_: (0, 0))],
            out_specs=pl.BlockSpec((1, t, LANES), lambda b, h, *_: (b, 0, h)),
            scratch_shapes=[pltpu.VMEM((n_kv, DIFF_DV + VT_PAD, KV_TILE), BF16),
                            pltpu.VMEM((KV_TILE, Q_TILE), F32),
                            pltpu.VMEM((2, KV_TILE, Q_TILE), F32),
                            pltpu.VMEM((2, KV_TILE, Q_TILE), F32),
                            pltpu.VMEM((2, 2, DIFF_DV + VT_PAD, Q_TILE), F32),
                            pltpu.VMEM((2, 2, 2 * LANES, Q_TILE), BF16)]),
        out_shape=jax.ShapeDtypeStruct((bsz, t, DIFF_WIDTH), BF16),
        compiler_params=_params("parallel", "parallel"),
        name="diff_attention",
    )(slopes, jnp.full((1,), n_kv // 2 - 1, jnp.int32), qkv, qkv, qkv, _key_features(t),
      lam_rows, subln_g)


def _lambda_init(layer):
    return 0.8 - 0.6 * math.exp(-0.3 * layer)


def _trunk(x, p):
    bsz, t, _ = x.shape
    m = bsz * t
    xf = x.reshape(m, D_MODEL)
    for l in range(DEPTH):
        xf = _ffn_ln(xf, p["ffn1_w13"][l], p["ffn1_w2"][l], p["ln1_g"][l], p["ln1_b"][l])
        qkv, gate = _inproj(xf, p["w_in"][l])
        qkv = qkv.reshape(bsz, t, IN_WIDTH)
        ret = _retention(qkv, gate.reshape(bsz, t, RET_WIDTH), p["ret_dec"][l],
                         p["ret_gn_g"][l], p["ret_gn_b"][l])
        dif = _diff_attention(qkv, p["slopes"], p["lam_rows"][l], p["diff_subln_g"][l],
                              _lambda_init(l))
        xf = _ffn_ln(xf, p["ffn2_w13"][l], p["ffn2_w2"][l], p["ln3_g"][l], p["ln3_b"][l],
                     mix=(ret.reshape(m, RET_WIDTH), dif.reshape(m, DIFF_WIDTH),
                          p["w_out"][l], p["ln2_g"][l], p["ln2_b"][l]))
    return xf.reshape(bsz, t, D_MODEL)


def kernel(x_prompt, x_sample, w_in, w_out, ret_decay_f, ret_decay_b, ret_gn_g, ret_gn_b,
           diff_lq1, diff_lk1, diff_lq2, diff_lk2, diff_subln_g,
           ffn1_w13, ffn1_w2, ffn2_w13, ffn2_w2,
           ln1_g, ln1_b, ln2_g, ln2_b, ln3_g, ln3_b):
    n_pairs = RET_WIDTH // LANES

    def pair_rows(dec):
        return jnp.broadcast_to(dec.reshape(DEPTH, n_pairs, 2, 1), (DEPTH, n_pairs, 2, LANES))

    row = lambda a: a.reshape(DEPTH, 1, -1)
    lam_rows = jnp.stack([diff_lq1, diff_lk1, diff_lq2, diff_lk2], axis=1)
    lam_rows = jnp.pad(lam_rows, ((0, 0), (0, 4), (0, LANES - DIFF_DH)))
    p = {
        "w_in": w_in.astype(BF16), "w_out": w_out.astype(BF16),
        "ffn1_w13": ffn1_w13.astype(BF16), "ffn1_w2": ffn1_w2.astype(BF16),
        "ffn2_w13": ffn2_w13.astype(BF16), "ffn2_w2": ffn2_w2.astype(BF16),
        "ret_dec": jnp.concatenate([pair_rows(ret_decay_f), pair_rows(ret_decay_b)], axis=2),
        "ret_gn_g": row(ret_gn_g), "ret_gn_b": row(ret_gn_b),
        "lam_rows": lam_rows, "diff_subln_g": row(diff_subln_g),
        "slopes": jnp.asarray([2.0 ** (-8.0 * (h + 1) / DIFF_HEADS) for h in range(DIFF_HEADS)], F32),
        "ln1_g": row(ln1_g), "ln1_b": row(ln1_b), "ln2_g": row(ln2_g), "ln2_b": row(ln2_b),
        "ln3_g": row(ln3_g), "ln3_b": row(ln3_b),
    }
    return _trunk(x_prompt, p), _trunk(x_sample, p)
```

```python
import functools
import math

import jax
import jax.numpy as jnp
from jax import lax
from jax.experimental import pallas as pl
from jax.experimental.pallas import tpu as pltpu

D_MODEL = 1024
DEPTH = 2
RET_HEADS = 8
RET_DK = 64
RET_WIDTH = 512
DIFF_HEADS = 4
DIFF_DH = 64
DIFF_DV = 128
DIFF_WIDTH = 512
IN_WIDTH = 3584
D_FF = 2816
CHUNK = 128
EPS = 1e-5
ALPHA = (2 * DEPTH) ** 0.25

LANES = 128
FF_CHUNK = 1408
ROW_TILE = 512
FFN_ROW_TILE = 1024
ROW_SUB = 256
Q_TILE = 512
KV_TILE = 512
VT_PAD = 16
AUG_ROWS = 16
FEAT_RADIX = 64
LOG2E = math.log2(math.e)
VMEM_LIMIT = 56 * 1024 * 1024

RQ_BLK, RK_BLK, RV_BLK, RG_BLK, DQ_BLK, DK_BLK, DV_BLK = 0, 4, 8, 12, 16, 20, 24

BF16 = jnp.bfloat16
F32 = jnp.float32
NT_DIMS = (((1,), (1,)), ((), ()))


def _resident(shape):
    nd = len(shape)
    return pl.BlockSpec(shape, lambda *_: (0,) * nd, pipeline_mode=pl.Buffered(1))


def _layer_norm(y, g, b):
    mu = jnp.mean(y, axis=-1, keepdims=True)
    d = y - mu
    var = jnp.mean(d * d, axis=-1, keepdims=True)
    return d * lax.rsqrt(var + EPS) * g + b


def _params(*sem):
    return pltpu.CompilerParams(dimension_semantics=sem, vmem_limit_bytes=VMEM_LIMIT)


def _ffn_residual_ln(x, w13_ref, w2_ref, g_ref, b_ref):
    xb = x.astype(BF16)
    acc = None
    for lo in range(0, D_FF, FF_CHUNK):
        a = jnp.dot(xb, w13_ref[:, lo:lo + FF_CHUNK], preferred_element_type=F32)
        b = jnp.dot(xb, w13_ref[:, D_FF + lo:D_FF + lo + FF_CHUNK], preferred_element_type=F32)
        h = (jax.nn.silu(a) * b).astype(BF16)
        part = jnp.dot(h, w2_ref[lo:lo + FF_CHUNK, :], preferred_element_type=F32)
        acc = part if acc is None else acc + part
    return _layer_norm(ALPHA * x + 0.5 * acc, g_ref[...], b_ref[...])


def _ffn_ln_kernel(x_ref, w13_ref, w2_ref, g_ref, b_ref, o_ref):
    for r in range(0, FFN_ROW_TILE, ROW_SUB):
        o_ref[r:r + ROW_SUB, :] = _ffn_residual_ln(x_ref[r:r + ROW_SUB, :], w13_ref, w2_ref,
                                                   g_ref, b_ref)


def _mix_ffn_ln_kernel(x_ref, r_ref, d_ref, wo_ref, gm_ref, bm_ref,
                       w13_ref, w2_ref, g_ref, b_ref, o_ref):
    def mixed(r):
        rows = slice(r, r + ROW_SUB)
        mix = jnp.dot(r_ref[rows, :], wo_ref[0:RET_WIDTH, :], preferred_element_type=F32)
        mix = mix + jnp.dot(d_ref[rows, :], wo_ref[RET_WIDTH:, :], preferred_element_type=F32)
        return _layer_norm(ALPHA * x_ref[rows, :] + mix, gm_ref[...], bm_ref[...])

    x = mixed(0)
    for r in range(0, FFN_ROW_TILE, ROW_SUB):
        x_next = mixed(r + ROW_SUB) if r + ROW_SUB < FFN_ROW_TILE else None
        o_ref[r:r + ROW_SUB, :] = _ffn_residual_ln(x, w13_ref, w2_ref, g_ref, b_ref)
        x = x_next


def _ffn_ln(x, w13, w2, g, b, mix=None):
    m = x.shape[0]
    row = pl.BlockSpec((FFN_ROW_TILE, D_MODEL), lambda i: (i, 0))
    vec = _resident((1, D_MODEL))
    ffn_specs = [_resident(w13.shape), _resident(w2.shape), vec, vec]
    if mix is None:
        body, args, specs = _ffn_ln_kernel, (x, w13, w2, g, b), [row] + ffn_specs
    else:
        ret, dif, w_out, gm, bm = mix
        half = pl.BlockSpec((FFN_ROW_TILE, RET_WIDTH), lambda i: (i, 0))
        body = _mix_ffn_ln_kernel
        args = (x, ret, dif, w_out, gm, bm, w13, w2, g, b)
        specs = [row, half, half, _resident(w_out.shape), vec, vec] + ffn_specs
    return pl.pallas_call(
        body,
        grid=(m // FFN_ROW_TILE,),
        in_specs=specs,
        out_specs=row,
        out_shape=jax.ShapeDtypeStruct((m, D_MODEL), F32),
        compiler_params=_params("parallel"),
        name="ffn_ln" if mix is None else "mix_ffn_ln",
    )(*args)


def _inproj_kernel(x_ref, w_ref, o_ref, gate_ref):
    y = jnp.dot(x_ref[...].astype(BF16), w_ref[...], preferred_element_type=F32)
    o_ref[...] = y.astype(BF16)
    gate_ref[...] = y[:, RG_BLK * LANES:RG_BLK * LANES + RET_WIDTH]


def _inproj(x, w):
    m = x.shape[0]
    return pl.pallas_call(
        _inproj_kernel,
        grid=(m // ROW_TILE,),
        in_specs=[pl.BlockSpec((ROW_TILE, D_MODEL), lambda i: (i, 0)), _resident(w.shape)],
        out_specs=[pl.BlockSpec((ROW_TILE, IN_WIDTH), lambda i: (i, 0)),
                   pl.BlockSpec((ROW_TILE, RET_WIDTH), lambda i: (i, 0))],
        out_shape=[jax.ShapeDtypeStruct((m, IN_WIDTH), BF16),
                   jax.ShapeDtypeStruct((m, RET_WIDTH), F32)],
        compiler_params=_params("parallel"),
        name="inproj",
    )(x, w)


def _ret_kernel(q_ref, k_ref, v_ref, gate_ref, dec_ref, gng_ref, gnb_ref, o_ref,
                sf_ref, sb_ref, *, n_chunks):
    c_len = CHUNK
    k_scale = RET_DK ** -0.5
    lane = lax.broadcasted_iota(jnp.int32, (1, LANES), 1)
    lo_half = lane < RET_DK
    m_lo = lo_half.astype(F32)
    m_hi = 1.0 - m_lo
    lg = jax.nn.log_sigmoid(dec_ref[0])
    lgf0, lgf1, lgb0, lgb1 = lg[0:1], lg[1:2], lg[2:3], lg[3:4]
    lgf = jnp.where(lo_half, lgf0, lgf1)
    lgb = jnp.where(lo_half, lgb0, lgb1)

    ri = lax.broadcasted_iota(jnp.int32, (c_len, c_len), 0)
    ci = lax.broadcasted_iota(jnp.int32, (c_len, c_len), 1)
    dist = (ri - ci).astype(F32)
    causal = dist >= 0.0
    block_diag = (ri < RET_DK) == (ci < RET_DK)

    def decay_tile(lf, lb):
        return jnp.where(causal, jnp.exp(lf * jnp.maximum(dist, 0.0)),
                         jnp.exp(lb * jnp.maximum(-dist, 0.0))) * k_scale

    d0 = decay_tile(lgf0, lgb0)
    d1 = decay_tile(lgf1, lgb1)
    row = lax.broadcasted_iota(jnp.int32, (c_len, LANES), 0).astype(F32)
    qdf = jnp.exp(lgf * (row + 1.0))
    qdb = jnp.exp(lgb * (c_len - row))
    kdf = jnp.exp(lgf * (c_len - 1.0 - row)) * k_scale
    kdb = jnp.exp(lgb * row) * k_scale
    gcf = jnp.exp(lgf * c_len)
    gcb = jnp.exp(lgb * c_len)

    def rows_of(c):
        return pl.ds(pl.multiple_of(c * c_len, c_len), c_len)

    def kv_body(c, carry):
        rows = rows_of(c)
        k = k_ref[0, rows, :].astype(F32)
        v = v_ref[0, rows, :]
        kvf = jnp.dot((k * kdf).T.astype(BF16), v, preferred_element_type=F32)
        kvb = jnp.dot((k * kdb).T.astype(BF16), v, preferred_element_type=F32)
        sf_ref[c] = jnp.where(block_diag, kvf, 0.0)
        sb_ref[c] = jnp.where(block_diag, kvb, 0.0)
        return carry

    lax.fori_loop(0, n_chunks, kv_body, 0, unroll=8)

    def fwd_scan(c, s):
        kv = sf_ref[c]
        sf_ref[c] = s
        return s * gcf + kv

    lax.fori_loop(0, n_chunks, fwd_scan, jnp.zeros((c_len, LANES), F32))

    def bwd_scan(i, s):
        c = n_chunks - 1 - i
        kv = sb_ref[c]
        sb_ref[c] = s
        return s * gcb + kv

    lax.fori_loop(0, n_chunks, bwd_scan, jnp.zeros((c_len, LANES), F32))

    gng = gng_ref[...]
    gnb = gnb_ref[...]
    inv_dv = 1.0 / RET_DK

    def out_body(c, carry):
        rows = rows_of(c)
        q = q_ref[0, rows, :].astype(F32)
        k = k_ref[0, rows, :]
        v = v_ref[0, rows, :].astype(F32)
        s0 = lax.dot_general((q * m_lo).astype(BF16), k, NT_DIMS, preferred_element_type=F32)
        s1 = lax.dot_general((q * m_hi).astype(BF16), k, NT_DIMS, preferred_element_type=F32)
        o = jnp.dot((s0 * d0).astype(BF16), (v * m_lo).astype(BF16), preferred_element_type=F32)
        o = o + jnp.dot((s1 * d1).astype(BF16), (v * m_hi).astype(BF16), preferred_element_type=F32)
        o = o + jnp.dot((q * qdf).astype(BF16), sf_ref[c].astype(BF16), preferred_element_type=F32)
        o = o + jnp.dot((q * qdb).astype(BF16), sb_ref[c].astype(BF16), preferred_element_type=F32)
        s_lo = jnp.sum(o * m_lo, axis=-1, keepdims=True)
        s_hi = jnp.sum(o * m_hi, axis=-1, keepdims=True)
        d = o - jnp.where(lo_half, s_lo, s_hi) * inv_dv
        dd = d * d
        v_lo = jnp.sum(dd * m_lo, axis=-1, keepdims=True)
        v_hi = jnp.sum(dd * m_hi, axis=-1, keepdims=True)
        var = jnp.where(lo_half, v_lo, v_hi) * inv_dv
        y = d * lax.rsqrt(var + EPS) * gng + gnb
        o_ref[0, rows, :] = (y * jax.nn.silu(gate_ref[0, rows, :])).astype(BF16)
        return carry

    lax.fori_loop(0, n_chunks, out_body, 0, unroll=8)


def _retention(qkv, gate, dec, gng, gnb):
    bsz, t, _ = qkv.shape
    n_chunks = t // CHUNK
    n_pairs = RET_WIDTH // LANES

    def col(blk):
        return pl.BlockSpec((1, t, LANES), lambda b, p: (b, 0, blk + p))

    return pl.pallas_call(
        functools.partial(_ret_kernel, n_chunks=n_chunks),
        grid=(bsz, n_pairs),
        in_specs=[col(RQ_BLK), col(RK_BLK), col(RV_BLK),
                  pl.BlockSpec((1, t, LANES), lambda b, p: (b, 0, p)),
                  pl.BlockSpec((1, 4, LANES), lambda b, p: (p, 0, 0)),
                  pl.BlockSpec((1, LANES), lambda b, p: (0, p)),
                  pl.BlockSpec((1, LANES), lambda b, p: (0, p))],
        out_specs=pl.BlockSpec((1, t, LANES), lambda b, p: (b, 0, p)),
        out_shape=jax.ShapeDtypeStruct((bsz, t, RET_WIDTH), BF16),
        scratch_shapes=[pltpu.VMEM((n_chunks, CHUNK, LANES), F32),
                        pltpu.VMEM((n_chunks, CHUNK, LANES), F32)],
        compiler_params=_params("parallel", "parallel"),
        name="retention",
    )(qkv, qkv, qkv, gate, dec, gng, gnb)


def _split3(x):
    x1 = x.astype(BF16).astype(F32)
    r = x - x1
    x2 = r.astype(BF16).astype(F32)
    x3 = (r - x2).astype(BF16).astype(F32)
    return x1, x2, x3


def _diff_kernel(slopes_ref, trips_ref, q_ref, k_ref, v_ref, kfeat_ref, lam_ref, g_ref, o_ref,
                 vt_ref, dbias_ref, s_even_ref, s_odd_ref, acc_ref, qaug_ref, *, n_kv, n_q, lam0):
    c = slopes_ref[pl.program_id(1)] * LOG2E

    ones_row = (lax.broadcasted_iota(jnp.int32, (VT_PAD, KV_TILE), 0) == 0).astype(F32)
    for j in range(n_kv):
        vt = v_ref[0, j * KV_TILE:(j + 1) * KV_TILE, :].astype(F32).T
        vt_ref[j] = jnp.concatenate([vt, ones_row], axis=0).astype(BF16)
    kio = lax.broadcasted_iota(jnp.int32, (KV_TILE, Q_TILE), 0)
    qio = lax.broadcasted_iota(jnp.int32, (KV_TILE, Q_TILE), 1)
    dbias_ref[...] = -(c * jnp.abs(kio - qio).astype(F32))

    lv = lam_ref[...]
    lam = (jnp.exp(jnp.sum(lv[0:1] * lv[1:2], axis=-1, keepdims=True))
           - jnp.exp(jnp.sum(lv[2:3] * lv[3:4], axis=-1, keepdims=True)) + lam0)
    out_gain = g_ref[...] * (1.0 - lam0)

    feat = lax.broadcasted_iota(jnp.int32, (LANES, 1), 0)
    rowi = lax.broadcasted_iota(jnp.int32, (AUG_ROWS, Q_TILE), 0)
    lane_q = lax.broadcasted_iota(jnp.int32, (1, Q_TILE), 1)
    c_row = jnp.full((1, Q_TILE), c, F32)
    c_parts = _split3(c_row)
    bufs = (s_even_ref, s_odd_ref)

    def query_rows(qs):
        return pl.ds(pl.multiple_of(qs * Q_TILE, Q_TILE), Q_TILE)

    def key_rows(j):
        return pl.ds(pl.multiple_of(j * KV_TILE, KV_TILE), KV_TILE)

    def diag_tile(qs):
        return qs

    def tile_of(qs, pos):
        jd = diag_tile(qs)
        return jnp.where(pos == 0, jd, jnp.where(pos - 1 < jd, pos - 1, pos))

    def prepare_queries(qs, slot):
        qt = (q_ref[0, query_rows(qs), :].astype(F32) * (LOG2E * DIFF_DH ** -0.5)).T
        qpos = (qs * Q_TILE + lane_q).astype(F32)
        t_parts = _split3(-(c_row * qpos))
        coef = [float(FEAT_RADIX) * cp for cp in c_parts] + list(c_parts) + list(t_parts)
        aug = jnp.zeros((AUG_ROWS, Q_TILE), F32)
        for r, row in enumerate(coef):
            aug = jnp.where(rowi == r, row, aug)
        tail = jnp.concatenate([aug.astype(BF16),
                                jnp.zeros((LANES - AUG_ROWS, Q_TILE), BF16)], axis=0)
        qaug_ref[slot, 0] = jnp.concatenate([jnp.where(feat < DIFF_DH, qt, 0.0).astype(BF16), tail], axis=0)
        qaug_ref[slot, 1] = jnp.concatenate([jnp.where(feat < DIFF_DH, 0.0, qt).astype(BF16), tail], axis=0)

    def diag_scores(qs, slot):
        k_t = k_ref[0, key_rows(diag_tile(qs)), :]
        return lambda mp: jnp.dot(k_t, qaug_ref[slot, mp, 0:LANES, :],
                                  preferred_element_type=F32) + dbias_ref[...]

    def offdiag_scores(qs, slot, pos):
        j = tile_of(qs, pos)
        side = (j > diag_tile(qs)).astype(jnp.int32)
        lhs = jnp.concatenate([k_ref[0, key_rows(j), :], kfeat_ref[side, key_rows(j), :]], axis=1)
        return lambda mp: jnp.dot(lhs, qaug_ref[slot, mp], preferred_element_type=F32)

    def score_part(scores, mp, dst_ref):
        s = scores(mp)
        dst_ref[mp] = s
        return jnp.max(s, axis=0, keepdims=True)

    def softmax_part(slot, mp, vt_t, src_ref, col_max, m_old):
        m_new = jnp.maximum(m_old, col_max)
        alpha = jnp.exp2(m_old - m_new)
        p = jnp.exp2(src_ref[mp] - m_new).astype(BF16)
        acc_ref[slot, mp] = (alpha * acc_ref[slot, mp]
                             + jnp.dot(vt_t, p, preferred_element_type=F32))
        return m_new

    def stage_pair(scores, dst_ref, qs, slot, pos, src_ref, col_max, m_run):
        vt_t = vt_ref[tile_of(qs, pos)]
        new_cm, new_m = [], []
        for mp in range(2):
            new_cm.append(score_part(scores, mp, dst_ref))
            new_m.append(softmax_part(slot, mp, vt_t, src_ref, col_max[mp], m_run[mp]))
        return tuple(new_cm), tuple(new_m)

    def write_output(qs, slot):
        a0 = acc_ref[slot, 0]
        a1 = acc_ref[slot, 1]
        att = (a0[:DIFF_DV] / a0[DIFF_DV:DIFF_DV + 1]
               - lam * (a1[:DIFF_DV] / a1[DIFF_DV:DIFF_DV + 1]))
        ms = jnp.mean(att * att, axis=0, keepdims=True)
        att = (att * lax.rsqrt(ms + EPS)).T
        o_ref[0, query_rows(qs), :] = (att * out_gain).astype(BF16)

    def query_tile_body(qs, cm):
        slot = qs % 2
        prv = jnp.maximum(qs - 1, 0)
        nxt = jnp.minimum(qs + 1, n_q - 1)
        acc_ref[slot] = jnp.zeros(acc_ref.shape[1:], F32)
        m_init = jnp.full((1, Q_TILE), -1e30, F32)

        def pair_body(i, carry):
            m_run, cm_even = carry
            cm_odd, m_run = stage_pair(offdiag_scores(qs, slot, 2 * i + 1), s_odd_ref,
                                       qs, slot, 2 * i, s_even_ref, cm_even, m_run)
            cm_even, m_run = stage_pair(offdiag_scores(qs, slot, 2 * i + 2), s_even_ref,
                                        qs, slot, 2 * i + 1, s_odd_ref, cm_odd, m_run)
            return m_run, cm_even

        m_run, cm_even = lax.fori_loop(0, trips_ref[0], pair_body, ((m_init, m_init), cm))
        write_output(prv, 1 - slot)
        cm_odd, m_run = stage_pair(offdiag_scores(qs, slot, n_kv - 1), s_odd_ref,
                                   qs, slot, n_kv - 2, s_even_ref, cm_even, m_run)
        prepare_queries(nxt, 1 - slot)
        cm_next, _ = stage_pair(diag_scores(nxt, 1 - slot), s_even_ref,
                                qs, slot, n_kv - 1, s_odd_ref, cm_odd, m_run)
        return cm_next

    acc_ref[...] = jnp.ones(acc_ref.shape, F32)
    prepare_queries(0, 0)
    first = diag_scores(0, 0)
    lax.fori_loop(0, n_q, query_tile_body,
                  tuple(score_part(first, mp, s_even_ref) for mp in range(2)))
    write_output(n_q - 1, (n_q - 1) % 2)


def _key_features(t):
    pos = jnp.arange(t, dtype=jnp.int32)
    hi = (pos // FEAT_RADIX).astype(F32)
    lo = (pos % FEAT_RADIX).astype(F32)
    one = jnp.ones((t,), F32)
    cols = jnp.stack([hi, hi, hi, lo, lo, lo, one, one, one], axis=1)
    cols = jnp.pad(cols, ((0, 0), (0, LANES - cols.shape[1])))
    return jnp.stack([cols, -cols]).astype(BF16)


def _diff_attention(qkv, slopes, lam_rows, subln_g, lam0):
    bsz, t, _ = qkv.shape
    n_kv = t // KV_TILE
    assert n_kv % 2 == 0 and KV_TILE == Q_TILE and t <= FEAT_RADIX * 256

    def col(blk):
        return pl.BlockSpec((1, t, LANES), lambda b, h, *_: (b, 0, blk + h))

    return pl.pallas_call(
        functools.partial(_diff_kernel, n_kv=n_kv, n_q=t // Q_TILE, lam0=lam0),
        grid_spec=pltpu.PrefetchScalarGridSpec(
            num_scalar_prefetch=2,
            grid=(bsz, DIFF_HEADS),
            in_specs=[col(DQ_BLK), col(DK_BLK), col(DV_BLK),
                      pl.BlockSpec((2, t, LANES), lambda b, h, *_: (0, 0, 0),
                                   pipeline_mode=pl.Buffered(1)),
                      pl.BlockSpec((8, LANES), lambda b, h, *_: (0, 0)),
                      pl.BlockSpec((1, LANES), lambda b, h, *_: (0, 0))],
            out_specs=pl.BlockSpec((1, t, LANES), lambda b, h, *_: (b, 0, h)),
            scratch_shapes=[pltpu.VMEM((n_kv, DIFF_DV + VT_PAD, KV_TILE), BF16),
                            pltpu.VMEM((KV_TILE, Q_TILE), F32),
                            pltpu.VMEM((2, KV_TILE, Q_TILE), F32),
                            pltpu.VMEM((2, KV_TILE, Q_TILE), F32),
                            pltpu.VMEM((2, 2, DIFF_DV + VT_PAD, Q_TILE), F32),
                            pltpu.VMEM((2, 2, 2 * LANES, Q_TILE), BF16)]),
        out_shape=jax.ShapeDtypeStruct((bsz, t, DIFF_WIDTH), BF16),
        compiler_params=_params("parallel", "parallel"),
        name="diff_attention",
    )(slopes, jnp.full((1,), n_kv // 2 - 1, jnp.int32), qkv, qkv, qkv, _key_features(t),
      lam_rows, subln_g)


def _lambda_init(layer):
    return 0.8 - 0.6 * math.exp(-0.3 * layer)


def _trunk(x, p):
    bsz, t, _ = x.shape
    m = bsz * t
    xf = x.reshape(m, D_MODEL)
    for l in range(DEPTH):
        xf = _ffn_ln(xf, p["ffn1_w13"][l], p["ffn1_w2"][l], p["ln1_g"][l], p["ln1_b"][l])
        qkv, gate = _inproj(xf, p["w_in"][l])
        qkv = qkv.reshape(bsz, t, IN_WIDTH)
        ret = _retention(qkv, gate.reshape(bsz, t, RET_WIDTH), p["ret_dec"][l],
                         p["ret_gn_g"][l], p["ret_gn_b"][l])
        dif = _diff_attention(qkv, p["slopes"], p["lam_rows"][l], p["diff_subln_g"][l],
                              _lambda_init(l))
        xf = _ffn_ln(xf, p["ffn2_w13"][l], p["ffn2_w2"][l], p["ln3_g"][l], p["ln3_b"][l],
                     mix=(ret.reshape(m, RET_WIDTH), dif.reshape(m, DIFF_WIDTH),
                          p["w_out"][l], p["ln2_g"][l], p["ln2_b"][l]))
    return xf.reshape(bsz, t, D_MODEL)


def kernel(x_prompt, x_sample, w_in, w_out, ret_decay_f, ret_decay_b, ret_gn_g, ret_gn_b,
           diff_lq1, diff_lk1, diff_lq2, diff_lk2, diff_subln_g,
           ffn1_w13, ffn1_w2, ffn2_w13, ffn2_w2,
           ln1_g, ln1_b, ln2_g, ln2_b, ln3_g, ln3_b):
    n_pairs = RET_WIDTH // LANES

    def pair_rows(dec):
        return jnp.broadcast_to(dec.reshape(DEPTH, n_pairs, 2, 1), (DEPTH, n_pairs, 2, LANES))

    row = lambda a: a.reshape(DEPTH, 1, -1)
    lam_rows = jnp.stack([diff_lq1, diff_lk1, diff_lq2, diff_lk2], axis=1)
    lam_rows = jnp.pad(lam_rows, ((0, 0), (0, 4), (0, LANES - DIFF_DH)))
    p = {
        "w_in": w_in.astype(BF16), "w_out": w_out.astype(BF16),
        "ffn1_w13": ffn1_w13.astype(BF16), "ffn1_w2": ffn1_w2.astype(BF16),
        "ffn2_w13": ffn2_w13.astype(BF16), "ffn2_w2": ffn2_w2.astype(BF16),
        "ret_dec": jnp.concatenate([pair_rows(ret_decay_f), pair_rows(ret_decay_b)], axis=2),
        "ret_gn_g": row(ret_gn_g), "ret_gn_b": row(ret_gn_b),
        "lam_rows": lam_rows, "diff_subln_g": row(diff_subln_g),
        "slopes": jnp.asarray([2.0 ** (-8.0 * (h + 1) / DIFF_HEADS) for h in range(DIFF_HEADS)], F32),
        "ln1_g": row(ln1_g), "ln1_b": row(ln1_b), "ln2_g": row(ln2_g), "ln2_b": row(ln2_b),
        "ln3_g": row(ln3_g), "ln3_b": row(ln3_b),
    }
    return _trunk(x_prompt, p), _trunk(x_sample, p)
```

```python
import functools
import math

import jax
import jax.numpy as jnp
from jax import lax
from jax.experimental import pallas as pl
from jax.experimental.pallas import tpu as pltpu

D_MODEL = 1024
DEPTH = 2
RET_HEADS = 8
RET_DK = 64
RET_WIDTH = 512
DIFF_HEADS = 4
DIFF_DH = 64
DIFF_DV = 128
DIFF_WIDTH = 512
IN_WIDTH = 3584
D_FF = 2816
CHUNK = 128
EPS = 1e-5
ALPHA = (2 * DEPTH) ** 0.25

LANES = 128
FF_CHUNK = D_FF
ROW_TILE = 512
FFN_ROW_TILE = 1024
ROW_SUB = 256
Q_TILE = 512
KV_TILE = 512
VT_PAD = 16
AUG_ROWS = 16
FEAT_RADIX = 64
LOG2E = math.log2(math.e)
VMEM_LIMIT = 56 * 1024 * 1024

RQ_BLK, RK_BLK, RV_BLK, RG_BLK, DQ_BLK, DK_BLK, DV_BLK = 0, 4, 8, 12, 16, 20, 24

BF16 = jnp.bfloat16
F32 = jnp.float32
NT_DIMS = (((1,), (1,)), ((), ()))


def _resident(shape):
    nd = len(shape)
    return pl.BlockSpec(shape, lambda *_: (0,) * nd, pipeline_mode=pl.Buffered(1))


def _layer_norm(y, g, b):
    mu = jnp.mean(y, axis=-1, keepdims=True)
    d = y - mu
    var = jnp.mean(d * d, axis=-1, keepdims=True)
    return d * lax.rsqrt(var + EPS) * g + b


def _params(*sem):
    return pltpu.CompilerParams(dimension_semantics=sem, vmem_limit_bytes=VMEM_LIMIT)


def _ffn_residual_ln(x, w13_ref, w2_ref, g_ref, b_ref):
    xb = x.astype(BF16)
    acc = None
    for lo in range(0, D_FF, FF_CHUNK):
        a = jnp.dot(xb, w13_ref[:, lo:lo + FF_CHUNK], preferred_element_type=F32)
        b = jnp.dot(xb, w13_ref[:, D_FF + lo:D_FF + lo + FF_CHUNK], preferred_element_type=F32)
        h = (jax.nn.silu(a) * b).astype(BF16)
        part = jnp.dot(h, w2_ref[lo:lo + FF_CHUNK, :], preferred_element_type=F32)
        acc = part if acc is None else acc + part
    return _layer_norm(ALPHA * x + 0.5 * acc, g_ref[...], b_ref[...])


def _ffn_ln_kernel(x_ref, w13_ref, w2_ref, g_ref, b_ref, o_ref):
    for r in range(0, FFN_ROW_TILE, ROW_SUB):
        o_ref[r:r + ROW_SUB, :] = _ffn_residual_ln(x_ref[r:r + ROW_SUB, :], w13_ref, w2_ref,
                                                   g_ref, b_ref)


def _mix_ffn_ln_kernel(x_ref, r_ref, d_ref, wo_ref, gm_ref, bm_ref,
                       w13_ref, w2_ref, g_ref, b_ref, o_ref):
    def mixed(r):
        rows = slice(r, r + ROW_SUB)
        mix = jnp.dot(r_ref[rows, :], wo_ref[0:RET_WIDTH, :], preferred_element_type=F32)
        mix = mix + jnp.dot(d_ref[rows, :], wo_ref[RET_WIDTH:, :], preferred_element_type=F32)
        return _layer_norm(ALPHA * x_ref[rows, :] + mix, gm_ref[...], bm_ref[...])

    x = mixed(0)
    for r in range(0, FFN_ROW_TILE, ROW_SUB):
        x_next = mixed(r + ROW_SUB) if r + ROW_SUB < FFN_ROW_TILE else None
        o_ref[r:r + ROW_SUB, :] = _ffn_residual_ln(x, w13_ref, w2_ref, g_ref, b_ref)
        x = x_next


def _ffn_ln(x, w13, w2, g, b, mix=None):
    m = x.shape[0]
    row = pl.BlockSpec((FFN_ROW_TILE, D_MODEL), lambda i: (i, 0))
    vec = _resident((1, D_MODEL))
    ffn_specs = [_resident(w13.shape), _resident(w2.shape), vec, vec]
    if mix is None:
        body, args, specs = _ffn_ln_kernel, (x, w13, w2, g, b), [row] + ffn_specs
    else:
        ret, dif, w_out, gm, bm = mix
        half = pl.BlockSpec((FFN_ROW_TILE, RET_WIDTH), lambda i: (i, 0))
        body = _mix_ffn_ln_kernel
        args = (x, ret, dif, w_out, gm, bm, w13, w2, g, b)
        specs = [row, half, half, _resident(w_out.shape), vec, vec] + ffn_specs
    return pl.pallas_call(
        body,
        grid=(m // FFN_ROW_TILE,),
        in_specs=specs,
        out_specs=row,
        out_shape=jax.ShapeDtypeStruct((m, D_MODEL), F32),
        compiler_params=_params("parallel"),
        name="ffn_ln" if mix is None else "mix_ffn_ln",
    )(*args)


def _inproj_kernel(x_ref, w_ref, o_ref, gate_ref):
    y = jnp.dot(x_ref[...].astype(BF16), w_ref[...], preferred_element_type=F32)
    o_ref[...] = y.astype(BF16)
    gate_ref[...] = y[:, RG_BLK * LANES:RG_BLK * LANES + RET_WIDTH]


def _inproj(x, w):
    m = x.shape[0]
    return pl.pallas_call(
        _inproj_kernel,
        grid=(m // ROW_TILE,),
        in_specs=[pl.BlockSpec((ROW_TILE, D_MODEL), lambda i: (i, 0)), _resident(w.shape)],
        out_specs=[pl.BlockSpec((ROW_TILE, IN_WIDTH), lambda i: (i, 0)),
                   pl.BlockSpec((ROW_TILE, RET_WIDTH), lambda i: (i, 0))],
        out_shape=[jax.ShapeDtypeStruct((m, IN_WIDTH), BF16),
                   jax.ShapeDtypeStruct((m, RET_WIDTH), F32)],
        compiler_params=_params("parallel"),
        name="inproj",
    )(x, w)


def _ret_kernel(q_ref, k_ref, v_ref, gate_ref, dec_ref, gng_ref, gnb_ref, o_ref,
                sf_ref, sb_ref, *, n_chunks):
    c_len = CHUNK
    k_scale = RET_DK ** -0.5
    lane = lax.broadcasted_iota(jnp.int32, (1, LANES), 1)
    lo_half = lane < RET_DK
    m_lo = lo_half.astype(F32)
    m_hi = 1.0 - m_lo
    lg = jax.nn.log_sigmoid(dec_ref[0])
    lgf0, lgf1, lgb0, lgb1 = lg[0:1], lg[1:2], lg[2:3], lg[3:4]
    lgf = jnp.where(lo_half, lgf0, lgf1)
    lgb = jnp.where(lo_half, lgb0, lgb1)

    ri = lax.broadcasted_iota(jnp.int32, (c_len, c_len), 0)
    ci = lax.broadcasted_iota(jnp.int32, (c_len, c_len), 1)
    dist = (ri - ci).astype(F32)
    causal = dist >= 0.0
    block_diag = (ri < RET_DK) == (ci < RET_DK)

    def decay_tile(lf, lb):
        return jnp.where(causal, jnp.exp(lf * jnp.maximum(dist, 0.0)),
                         jnp.exp(lb * jnp.maximum(-dist, 0.0))) * k_scale

    d0 = decay_tile(lgf0, lgb0)
    d1 = decay_tile(lgf1, lgb1)
    row = lax.broadcasted_iota(jnp.int32, (c_len, LANES), 0).astype(F32)
    qdf = jnp.exp(lgf * (row + 1.0))
    qdb = jnp.exp(lgb * (c_len - row))
    kdf = jnp.exp(lgf * (c_len - 1.0 - row)) * k_scale
    kdb = jnp.exp(lgb * row) * k_scale
    gcf = jnp.exp(lgf * c_len)
    gcb = jnp.exp(lgb * c_len)

    def rows_of(c):
        return pl.ds(pl.multiple_of(c * c_len, c_len), c_len)

    def kv_body(c, carry):
        rows = rows_of(c)
        k = k_ref[0, rows, :].astype(F32)
        v = v_ref[0, rows, :]
        kvf = jnp.dot((k * kdf).T.astype(BF16), v, preferred_element_type=F32)
        kvb = jnp.dot((k * kdb).T.astype(BF16), v, preferred_element_type=F32)
        sf_ref[c] = jnp.where(block_diag, kvf, 0.0)
        sb_ref[c] = jnp.where(block_diag, kvb, 0.0)
        return carry

    lax.fori_loop(0, n_chunks, kv_body, 0, unroll=8)

    def fwd_scan(c, s):
        kv = sf_ref[c]
        sf_ref[c] = s
        return s * gcf + kv

    lax.fori_loop(0, n_chunks, fwd_scan, jnp.zeros((c_len, LANES), F32))

    def bwd_scan(i, s):
        c = n_chunks - 1 - i
        kv = sb_ref[c]
        sb_ref[c] = s
        return s * gcb + kv

    lax.fori_loop(0, n_chunks, bwd_scan, jnp.zeros((c_len, LANES), F32))

    gng = gng_ref[...]
    gnb = gnb_ref[...]
    inv_dv = 1.0 / RET_DK

    def out_body(c, carry):
        rows = rows_of(c)
        q = q_ref[0, rows, :].astype(F32)
        k = k_ref[0, rows, :]
        v = v_ref[0, rows, :].astype(F32)
        s0 = lax.dot_general((q * m_lo).astype(BF16), k, NT_DIMS, preferred_element_type=F32)
        s1 = lax.dot_general((q * m_hi).astype(BF16), k, NT_DIMS, preferred_element_type=F32)
        o = jnp.dot((s0 * d0).astype(BF16), (v * m_lo).astype(BF16), preferred_element_type=F32)
        o = o + jnp.dot((s1 * d1).astype(BF16), (v * m_hi).astype(BF16), preferred_element_type=F32)
        o = o + jnp.dot((q * qdf).astype(BF16), sf_ref[c].astype(BF16), preferred_element_type=F32)
        o = o + jnp.dot((q * qdb).astype(BF16), sb_ref[c].astype(BF16), preferred_element_type=F32)
        s_lo = jnp.sum(o * m_lo, axis=-1, keepdims=True)
        s_hi = jnp.sum(o * m_hi, axis=-1, keepdims=True)
        d = o - jnp.where(lo_half, s_lo, s_hi) * inv_dv
        dd = d * d
        v_lo = jnp.sum(dd * m_lo, axis=-1, keepdims=True)
        v_hi = jnp.sum(dd * m_hi, axis=-1, keepdims=True)
        var = jnp.where(lo_half, v_lo, v_hi) * inv_dv
        y = d * lax.rsqrt(var + EPS) * gng + gnb
        o_ref[0, rows, :] = (y * jax.nn.silu(gate_ref[0, rows, :])).astype(BF16)
        return carry

    lax.fori_loop(0, n_chunks, out_body, 0, unroll=8)


def _retention(qkv, gate, dec, gng, gnb):
    bsz, t, _ = qkv.shape
    n_chunks = t // CHUNK
    n_pairs = RET_WIDTH // LANES

    def col(blk):
        return pl.BlockSpec((1, t, LANES), lambda b, p: (b, 0, blk + p))

    return pl.pallas_call(
        functools.partial(_ret_kernel, n_chunks=n_chunks),
        grid=(bsz, n_pairs),
        in_specs=[col(RQ_BLK), col(RK_BLK), col(RV_BLK),
                  pl.BlockSpec((1, t, LANES), lambda b, p: (b, 0, p)),
                  pl.BlockSpec((1, 4, LANES), lambda b, p: (p, 0, 0)),
                  pl.BlockSpec((1, LANES), lambda b, p: (0, p)),
                  pl.BlockSpec((1, LANES), lambda b, p: (0, p))],
        out_specs=pl.BlockSpec((1, t, LANES), lambda b, p: (b, 0, p)),
        out_shape=jax.ShapeDtypeStruct((bsz, t, RET_WIDTH), BF16),
        scratch_shapes=[pltpu.VMEM((n_chunks, CHUNK, LANES), F32),
                        pltpu.VMEM((n_chunks, CHUNK, LANES), F32)],
        compiler_params=_params("parallel", "parallel"),
        name="retention",
    )(qkv, qkv, qkv, gate, dec, gng, gnb)


def _split3(x):
    x1 = x.astype(BF16).astype(F32)
    r = x - x1
    x2 = r.astype(BF16).astype(F32)
    x3 = (r - x2).astype(BF16).astype(F32)
    return x1, x2, x3


def _diff_kernel(slopes_ref, trips_ref, q_ref, k_ref, v_ref, kfeat_ref, lam_ref, g_ref, o_ref,
                 vt_ref, dbias_ref, s_even_ref, s_odd_ref, acc_ref, qaug_ref, *, n_kv, n_q, lam0):
    c = slopes_ref[pl.program_id(1)] * LOG2E

    ones_row = (lax.broadcasted_iota(jnp.int32, (VT_PAD, KV_TILE), 0) == 0).astype(F32)
    for j in range(n_kv):
        vt = v_ref[0, j * KV_TILE:(j + 1) * KV_TILE, :].astype(F32).T
        vt_ref[j] = jnp.concatenate([vt, ones_row], axis=0).astype(BF16)
    kio = lax.broadcasted_iota(jnp.int32, (KV_TILE, Q_TILE), 0)
    qio = lax.broadcasted_iota(jnp.int32, (KV_TILE, Q_TILE), 1)
    dbias_ref[...] = -(c * jnp.abs(kio - qio).astype(F32))

    lv = lam_ref[...]
    lam = (jnp.exp(jnp.sum(lv[0:1] * lv[1:2], axis=-1, keepdims=True))
           - jnp.exp(jnp.sum(lv[2:3] * lv[3:4], axis=-1, keepdims=True)) + lam0)
    out_gain = g_ref[...] * (1.0 - lam0)

    feat = lax.broadcasted_iota(jnp.int32, (LANES, 1), 0)
    rowi = lax.broadcasted_iota(jnp.int32, (AUG_ROWS, Q_TILE), 0)
    lane_q = lax.broadcasted_iota(jnp.int32, (1, Q_TILE), 1)
    c_row = jnp.full((1, Q_TILE), c, F32)
    c_parts = _split3(c_row)
    bufs = (s_even_ref, s_odd_ref)

    def query_rows(qs):
        return pl.ds(pl.multiple_of(qs * Q_TILE, Q_TILE), Q_TILE)

    def key_rows(j):
        return pl.ds(pl.multiple_of(j * KV_TILE, KV_TILE), KV_TILE)

    def diag_tile(qs):
        return qs

    def tile_of(qs, pos):
        jd = diag_tile(qs)
        return jnp.where(pos == 0, jd, jnp.where(pos - 1 < jd, pos - 1, pos))

    def prepare_queries(qs, slot):
        qt = (q_ref[0, query_rows(qs), :].astype(F32) * (LOG2E * DIFF_DH ** -0.5)).T
        qpos = (qs * Q_TILE + lane_q).astype(F32)
        t_parts = _split3(-(c_row * qpos))
        coef = [float(FEAT_RADIX) * cp for cp in c_parts] + list(c_parts) + list(t_parts)
        aug = jnp.zeros((AUG_ROWS, Q_TILE), F32)
        for r, row in enumerate(coef):
            aug = jnp.where(rowi == r, row, aug)
        tail = jnp.concatenate([aug.astype(BF16),
                                jnp.zeros((LANES - AUG_ROWS, Q_TILE), BF16)], axis=0)
        qaug_ref[slot, 0] = jnp.concatenate([jnp.where(feat < DIFF_DH, qt, 0.0).astype(BF16), tail], axis=0)
        qaug_ref[slot, 1] = jnp.concatenate([jnp.where(feat < DIFF_DH, 0.0, qt).astype(BF16), tail], axis=0)

    def diag_scores(qs, slot):
        k_t = k_ref[0, key_rows(diag_tile(qs)), :]
        return lambda mp: jnp.dot(k_t, qaug_ref[slot, mp, 0:LANES, :],
                                  preferred_element_type=F32) + dbias_ref[...]

    def offdiag_scores(qs, slot, pos):
        j = tile_of(qs, pos)
        side = (j > diag_tile(qs)).astype(jnp.int32)
        lhs = jnp.concatenate([k_ref[0, key_rows(j), :], kfeat_ref[side, key_rows(j), :]], axis=1)
        return lambda mp: jnp.dot(lhs, qaug_ref[slot, mp], preferred_element_type=F32)

    def score_part(scores, mp, dst_ref):
        s = scores(mp)
        dst_ref[mp] = s
        return jnp.max(s, axis=0, keepdims=True)

    def softmax_part(slot, mp, vt_t, src_ref, col_max, m_old):
        m_new = jnp.maximum(m_old, col_max)
        alpha = jnp.exp2(m_old - m_new)
        p = jnp.exp2(src_ref[mp] - m_new).astype(BF16)
        acc_ref[slot, mp] = (alpha * acc_ref[slot, mp]
                             + jnp.dot(vt_t, p, preferred_element_type=F32))
        return m_new

    def stage_pair(scores, dst_ref, qs, slot, pos, src_ref, col_max, m_run):
        vt_t = vt_ref[tile_of(qs, pos)]
        new_cm, new_m = [], []
        for mp in range(2):
            new_cm.append(score_part(scores, mp, dst_ref))
            new_m.append(softmax_part(slot, mp, vt_t, src_ref, col_max[mp], m_run[mp]))
        return tuple(new_cm), tuple(new_m)

    def write_output(qs, slot):
        a0 = acc_ref[slot, 0]
        a1 = acc_ref[slot, 1]
        att = (a0[:DIFF_DV] / a0[DIFF_DV:DIFF_DV + 1]
               - lam * (a1[:DIFF_DV] / a1[DIFF_DV:DIFF_DV + 1]))
        ms = jnp.mean(att * att, axis=0, keepdims=True)
        att = (att * lax.rsqrt(ms + EPS)).T
        o_ref[0, query_rows(qs), :] = (att * out_gain).astype(BF16)

    def query_tile_body(qs, cm):
        slot = qs % 2
        prv = jnp.maximum(qs - 1, 0)
        nxt = jnp.minimum(qs + 1, n_q - 1)
        acc_ref[slot] = jnp.zeros(acc_ref.shape[1:], F32)
        m_init = jnp.full((1, Q_TILE), -1e30, F32)

        def pair_body(i, carry):
            m_run, cm_even = carry
            cm_odd, m_run = stage_pair(offdiag_scores(qs, slot, 2 * i + 1), s_odd_ref,
                                       qs, slot, 2 * i, s_even_ref, cm_even, m_run)
            cm_even, m_run = stage_pair(offdiag_scores(qs, slot, 2 * i + 2), s_even_ref,
                                        qs, slot, 2 * i + 1, s_odd_ref, cm_odd, m_run)
            return m_run, cm_even

        m_run, cm_even = lax.fori_loop(0, trips_ref[0], pair_body, ((m_init, m_init), cm))
        write_output(prv, 1 - slot)
        cm_odd, m_run = stage_pair(offdiag_scores(qs, slot, n_kv - 1), s_odd_ref,
                                   qs, slot, n_kv - 2, s_even_ref, cm_even, m_run)
        prepare_queries(nxt, 1 - slot)
        cm_next, _ = stage_pair(diag_scores(nxt, 1 - slot), s_even_ref,
                                qs, slot, n_kv - 1, s_odd_ref, cm_odd, m_run)
        return cm_next

    acc_ref[...] = jnp.ones(acc_ref.shape, F32)
    prepare_queries(0, 0)
    first = diag_scores(0, 0)
    lax.fori_loop(0, n_q, query_tile_body,
                  tuple(score_part(first, mp, s_even_ref) for mp in range(2)))
    write_output(n_q - 1, (n_q - 1) % 2)


def _key_features(t):
    pos = jnp.arange(t, dtype=jnp.int32)
    hi = (pos // FEAT_RADIX).astype(F32)
    lo = (pos % FEAT_RADIX).astype(F32)
    one = jnp.ones((t,), F32)
    cols = jnp.stack([hi, hi, hi, lo, lo, lo, one, one, one], axis=1)
    cols = jnp.pad(cols, ((0, 0), (0, LANES - cols.shape[1])))
    return jnp.stack([cols, -cols]).astype(BF16)


def _diff_attention(qkv, slopes, lam_rows, subln_g, lam0):
    bsz, t, _ = qkv.shape
    n_kv = t // KV_TILE
    assert n_kv % 2 == 0 and KV_TILE == Q_TILE and t <= FEAT_RADIX * 256

    def col(blk):
        return pl.BlockSpec((1, t, LANES), lambda b, h, *_: (b, 0, blk + h))

    return pl.pallas_call(
        functools.partial(_diff_kernel, n_kv=n_kv, n_q=t // Q_TILE, lam0=lam0),
        grid_spec=pltpu.PrefetchScalarGridSpec(
            num_scalar_prefetch=2,
            grid=(bsz, DIFF_HEADS),
            in_specs=[col(DQ_BLK), col(DK_BLK), col(DV_BLK),
                      pl.BlockSpec((2, t, LANES), lambda b, h, *_: (0, 0, 0),
                                   pipeline_mode=pl.Buffered(1)),
                      pl.BlockSpec((8, LANES), lambda b, h, *_: (0, 0)),
                      pl.BlockSpec((1, LANES), lambda b, h, *_: (0, 0))],
            out_specs=pl.BlockSpec((1, t, LANES), lambda b, h, *_: (b, 0, h)),
            scratch_shapes=[pltpu.VMEM((n_kv, DIFF_DV + VT_PAD, KV_TILE), BF16),
                            pltpu.VMEM((KV_TILE, Q_TILE), F32),
                            pltpu.VMEM((2, KV_TILE, Q_TILE), F32),
                            pltpu.VMEM((2, KV_TILE, Q_TILE), F32),
                            pltpu.VMEM((2, 2, DIFF_DV + VT_PAD, Q_TILE), F32),
                            pltpu.VMEM((2, 2, 2 * LANES, Q_TILE), BF16)]),
        out_shape=jax.ShapeDtypeStruct((bsz, t, DIFF_WIDTH), BF16),
        compiler_params=_params("parallel", "parallel"),
        name="diff_attention",
    )(slopes, jnp.full((1,), n_kv // 2 - 1, jnp.int32), qkv, qkv, qkv, _key_features(t),
      lam_rows, subln_g)


def _lambda_init(layer):
    return 0.8 - 0.6 * math.exp(-0.3 * layer)


def _trunk(x, p):
    bsz, t, _ = x.shape
    m = bsz * t
    xf = x.reshape(m, D_MODEL)
    for l in range(DEPTH):
        xf = _ffn_ln(xf, p["ffn1_w13"][l], p["ffn1_w2"][l], p["ln1_g"][l], p["ln1_b"][l])
        qkv, gate = _inproj(xf, p["w_in"][l])
        qkv = qkv.reshape(bsz, t, IN_WIDTH)
        ret = _retention(qkv, gate.reshape(bsz, t, RET_WIDTH), p["ret_dec"][l],
                         p["ret_gn_g"][l], p["ret_gn_b"][l])
        dif = _diff_attention(qkv, p["slopes"], p["lam_rows"][l], p["diff_subln_g"][l],
                              _lambda_init(l))
        xf = _ffn_ln(xf, p["ffn2_w13"][l], p["ffn2_w2"][l], p["ln3_g"][l], p["ln3_b"][l],
                     mix=(ret.reshape(m, RET_WIDTH), dif.reshape(m, DIFF_WIDTH),
                          p["w_out"][l], p["ln2_g"][l], p["ln2_b"][l]))
    return xf.reshape(bsz, t, D_MODEL)


def kernel(x_prompt, x_sample, w_in, w_out, ret_decay_f, ret_decay_b, ret_gn_g, ret_gn_b,
           diff_lq1, diff_lk1, diff_lq2, diff_lk2, diff_subln_g,
           ffn1_w13, ffn1_w2, ffn2_w13, ffn2_w2,
           ln1_g, ln1_b, ln2_g, ln2_b, ln3_g, ln3_b):
    n_pairs = RET_WIDTH // LANES

    def pair_rows(dec):
        return jnp.broadcast_to(dec.reshape(DEPTH, n_pairs, 2, 1), (DEPTH, n_pairs, 2, LANES))

    row = lambda a: a.reshape(DEPTH, 1, -1)
    lam_rows = jnp.stack([diff_lq1, diff_lk1, diff_lq2, diff_lk2], axis=1)
    lam_rows = jnp.pad(lam_rows, ((0, 0), (0, 4), (0, LANES - DIFF_DH)))
    p = {
        "w_in": w_in.astype(BF16), "w_out": w_out.astype(BF16),
        "ffn1_w13": ffn1_w13.astype(BF16), "ffn1_w2": ffn1_w2.astype(BF16),
        "ffn2_w13": ffn2_w13.astype(BF16), "ffn2_w2": ffn2_w2.astype(BF16),
        "ret_dec": jnp.concatenate([pair_rows(ret_decay_f), pair_rows(ret_decay_b)], axis=2),
        "ret_gn_g": row(ret_gn_g), "ret_gn_b": row(ret_gn_b),
        "lam_rows": lam_rows, "diff_subln_g": row(diff_subln_g),
        "slopes": jnp.asarray([2.0 ** (-8.0 * (h + 1) / DIFF_HEADS) for h in range(DIFF_HEADS)], F32),
        "ln1_g": row(ln1_g), "ln1_b": row(ln1_b), "ln2_g": row(ln2_g), "ln2_b": row(ln2_b),
        "ln3_g": row(ln3_g), "ln3_b": row(ln3_b),
    }
    return _trunk(x_prompt, p), _trunk(x_sample, p)
```

```python
import functools
import math

import jax
import jax.numpy as jnp
from jax import lax
from jax.experimental import pallas as pl
from jax.experimental.pallas import tpu as pltpu

D_MODEL = 1024
DEPTH = 2
RET_HEADS = 8
RET_DK = 64
RET_WIDTH = 512
DIFF_HEADS = 4
DIFF_DH = 64
DIFF_DV = 128
DIFF_WIDTH = 512
IN_WIDTH = 3584
D_FF = 2816
CHUNK = 128
EPS = 1e-5
ALPHA = (2 * DEPTH) ** 0.25

LANES = 128
FF_CHUNK = D_FF
ROW_TILE = 512
FFN_ROW_TILE = 1024
ROW_SUB = 256
Q_TILE = 512
KV_TILE = 512
VT_PAD = 16
AUG_ROWS = 16
FEAT_RADIX = 64
LOG2E = math.log2(math.e)
VMEM_LIMIT = 56 * 1024 * 1024

RQ_BLK, RK_BLK, RV_BLK, RG_BLK, DQ_BLK, DK_BLK, DV_BLK = 0, 4, 8, 12, 16, 20, 24

BF16 = jnp.bfloat16
F32 = jnp.float32
NT_DIMS = (((1,), (1,)), ((), ()))


def _resident(shape):
    nd = len(shape)
    return pl.BlockSpec(shape, lambda *_: (0,) * nd, pipeline_mode=pl.Buffered(1))


def _layer_norm(y, g, b):
    mu = jnp.mean(y, axis=-1, keepdims=True)
    d = y - mu
    var = jnp.mean(d * d, axis=-1, keepdims=True)
    return d * lax.rsqrt(var + EPS) * g + b


def _params(*sem):
    return pltpu.CompilerParams(dimension_semantics=sem, vmem_limit_bytes=VMEM_LIMIT)


def _ffn_residual_ln(x, w13_ref, w2_ref, g_ref, b_ref):
    xb = x.astype(BF16)
    acc = None
    for lo in range(0, D_FF, FF_CHUNK):
        a = jnp.dot(xb, w13_ref[:, lo:lo + FF_CHUNK], preferred_element_type=F32)
        b = jnp.dot(xb, w13_ref[:, D_FF + lo:D_FF + lo + FF_CHUNK], preferred_element_type=F32)
        h = (jax.nn.silu(a) * b).astype(BF16)
        part = jnp.dot(h, w2_ref[lo:lo + FF_CHUNK, :], preferred_element_type=F32)
        acc = part if acc is None else acc + part
    return _layer_norm(ALPHA * x + 0.5 * acc, g_ref[...], b_ref[...])


def _ffn_ln_kernel(x_ref, w13_ref, w2_ref, g_ref, b_ref, o_ref):
    for r in range(0, FFN_ROW_TILE, ROW_SUB):
        o_ref[r:r + ROW_SUB, :] = _ffn_residual_ln(x_ref[r:r + ROW_SUB, :], w13_ref, w2_ref,
                                                   g_ref, b_ref)


def _mix_ffn_ln_kernel(x_ref, r_ref, d_ref, wo_ref, gm_ref, bm_ref,
                       w13_ref, w2_ref, g_ref, b_ref, o_ref):
    def mixed(r):
        rows = slice(r, r + ROW_SUB)
        mix = jnp.dot(r_ref[rows, :], wo_ref[0:RET_WIDTH, :], preferred_element_type=F32)
        mix = mix + jnp.dot(d_ref[rows, :], wo_ref[RET_WIDTH:, :], preferred_element_type=F32)
        return _layer_norm(ALPHA * x_ref[rows, :] + mix, gm_ref[...], bm_ref[...])

    x = mixed(0)
    for r in range(0, FFN_ROW_TILE, ROW_SUB):
        x_next = mixed(r + ROW_SUB) if r + ROW_SUB < FFN_ROW_TILE else None
        o_ref[r:r + ROW_SUB, :] = _ffn_residual_ln(x, w13_ref, w2_ref, g_ref, b_ref)
        x = x_next


def _ffn_ln(x, w13, w2, g, b, mix=None):
    m = x.shape[0]
    row = pl.BlockSpec((FFN_ROW_TILE, D_MODEL), lambda i: (i, 0))
    vec = _resident((1, D_MODEL))
    ffn_specs = [_resident(w13.shape), _resident(w2.shape), vec, vec]
    if mix is None:
        body, args, specs = _ffn_ln_kernel, (x, w13, w2, g, b), [row] + ffn_specs
    else:
        ret, dif, w_out, gm, bm = mix
        half = pl.BlockSpec((FFN_ROW_TILE, RET_WIDTH), lambda i: (i, 0))
        body = _mix_ffn_ln_kernel
        args = (x, ret, dif, w_out, gm, bm, w13, w2, g, b)
        specs = [row, half, half, _resident(w_out.shape), vec, vec] + ffn_specs
    return pl.pallas_call(
        body,
        grid=(m // FFN_ROW_TILE,),
        in_specs=specs,
        out_specs=row,
        out_shape=jax.ShapeDtypeStruct((m, D_MODEL), F32),
        compiler_params=_params("parallel"),
        name="ffn_ln" if mix is None else "mix_ffn_ln",
    )(*args)


def _inproj_kernel(x_ref, w_ref, o_ref, gate_ref):
    y = jnp.dot(x_ref[...].astype(BF16), w_ref[...], preferred_element_type=F32)
    o_ref[...] = y.astype(BF16)
    gate_ref[...] = y[:, RG_BLK * LANES:RG_BLK * LANES + RET_WIDTH]


def _inproj(x, w):
    m = x.shape[0]
    return pl.pallas_call(
        _inproj_kernel,
        grid=(m // ROW_TILE,),
        in_specs=[pl.BlockSpec((ROW_TILE, D_MODEL), lambda i: (i, 0)), _resident(w.shape)],
        out_specs=[pl.BlockSpec((ROW_TILE, IN_WIDTH), lambda i: (i, 0)),
                   pl.BlockSpec((ROW_TILE, RET_WIDTH), lambda i: (i, 0))],
        out_shape=[jax.ShapeDtypeStruct((m, IN_WIDTH), BF16),
                   jax.ShapeDtypeStruct((m, RET_WIDTH), F32)],
        compiler_params=_params("parallel"),
        name="inproj",
    )(x, w)


def _ret_kernel(q_ref, k_ref, v_ref, gate_ref, dec_ref, gng_ref, gnb_ref, o_ref,
                sf_ref, sb_ref, *, n_chunks):
    c_len = CHUNK
    k_scale = RET_DK ** -0.5
    lane = lax.broadcasted_iota(jnp.int32, (1, LANES), 1)
    lo_half = lane < RET_DK
    m_lo = lo_half.astype(F32)
    m_hi = 1.0 - m_lo
    lg = jax.nn.log_sigmoid(dec_ref[0])
    lgf0, lgf1, lgb0, lgb1 = lg[0:1], lg[1:2], lg[2:3], lg[3:4]
    lgf = jnp.where(lo_half, lgf0, lgf1)
    lgb = jnp.where(lo_half, lgb0, lgb1)

    ri = lax.broadcasted_iota(jnp.int32, (c_len, c_len), 0)
    ci = lax.broadcasted_iota(jnp.int32, (c_len, c_len), 1)
    dist = (ri - ci).astype(F32)
    causal = dist >= 0.0
    block_diag = (ri < RET_DK) == (ci < RET_DK)

    def decay_tile(lf, lb):
        return jnp.where(causal, jnp.exp(lf * jnp.maximum(dist, 0.0)),
                         jnp.exp(lb * jnp.maximum(-dist, 0.0))) * k_scale

    d0 = decay_tile(lgf0, lgb0)
    d1 = decay_tile(lgf1, lgb1)
    row = lax.broadcasted_iota(jnp.int32, (c_len, LANES), 0).astype(F32)
    qdf = jnp.exp(lgf * (row + 1.0))
    qdb = jnp.exp(lgb * (c_len - row))
    kdf = jnp.exp(lgf * (c_len - 1.0 - row)) * k_scale
    kdb = jnp.exp(lgb * row) * k_scale
    gcf = jnp.exp(lgf * c_len)
    gcb = jnp.exp(lgb * c_len)

    def rows_of(c):
        return pl.ds(pl.multiple_of(c * c_len, c_len), c_len)

    def kv_body(c, carry):
        rows = rows_of(c)
        k = k_ref[0, rows, :].astype(F32)
        v = v_ref[0, rows, :]
        kvf = jnp.dot((k * kdf).T.astype(BF16), v, preferred_element_type=F32)
        kvb = jnp.dot((k * kdb).T.astype(BF16), v, preferred_element_type=F32)
        sf_ref[c] = jnp.where(block_diag, kvf, 0.0)
        sb_ref[c] = jnp.where(block_diag, kvb, 0.0)
        return carry

    lax.fori_loop(0, n_chunks, kv_body, 0, unroll=16)

    def fwd_scan(c, s):
        kv = sf_ref[c]
        sf_ref[c] = s
        return s * gcf + kv

    lax.fori_loop(0, n_chunks, fwd_scan, jnp.zeros((c_len, LANES), F32))

    def bwd_scan(i, s):
        c = n_chunks - 1 - i
        kv = sb_ref[c]
        sb_ref[c] = s
        return s * gcb + kv

    lax.fori_loop(0, n_chunks, bwd_scan, jnp.zeros((c_len, LANES), F32))

    gng = gng_ref[...]
    gnb = gnb_ref[...]
    inv_dv = 1.0 / RET_DK

    def out_body(c, carry):
        rows = rows_of(c)
        q = q_ref[0, rows, :].astype(F32)
        k = k_ref[0, rows, :]
        v = v_ref[0, rows, :].astype(F32)
        s0 = lax.dot_general((q * m_lo).astype(BF16), k, NT_DIMS, preferred_element_type=F32)
        s1 = lax.dot_general((q * m_hi).astype(BF16), k, NT_DIMS, preferred_element_type=F32)
        o = jnp.dot((s0 * d0).astype(BF16), (v * m_lo).astype(BF16), preferred_element_type=F32)
        o = o + jnp.dot((s1 * d1).astype(BF16), (v * m_hi).astype(BF16), preferred_element_type=F32)
        o = o + jnp.dot((q * qdf).astype(BF16), sf_ref[c].astype(BF16), preferred_element_type=F32)
        o = o + jnp.dot((q * qdb).astype(BF16), sb_ref[c].astype(BF16), preferred_element_type=F32)
        s_lo = jnp.sum(o * m_lo, axis=-1, keepdims=True)
        s_hi = jnp.sum(o * m_hi, axis=-1, keepdims=True)
        d = o - jnp.where(lo_half, s_lo, s_hi) * inv_dv
        dd = d * d
        v_lo = jnp.sum(dd * m_lo, axis=-1, keepdims=True)
        v_hi = jnp.sum(dd * m_hi, axis=-1, keepdims=True)
        var = jnp.where(lo_half, v_lo, v_hi) * inv_dv
        y = d * lax.rsqrt(var + EPS) * gng + gnb
        o_ref[0, rows, :] = (y * jax.nn.silu(gate_ref[0, rows, :])).astype(BF16)
        return carry

    lax.fori_loop(0, n_chunks, out_body, 0, unroll=16)


def _retention(qkv, gate, dec, gng, gnb):
    bsz, t, _ = qkv.shape
    n_chunks = t // CHUNK
    n_pairs = RET_WIDTH // LANES

    def col(blk):
        return pl.BlockSpec((1, t, LANES), lambda b, p: (b, 0, blk + p))

    return pl.pallas_call(
        functools.partial(_ret_kernel, n_chunks=n_chunks),
        grid=(bsz, n_pairs),
        in_specs=[col(RQ_BLK), col(RK_BLK), col(RV_BLK),
                  pl.BlockSpec((1, t, LANES), lambda b, p: (b, 0, p)),
                  pl.BlockSpec((1, 4, LANES), lambda b, p: (p, 0, 0)),
                  pl.BlockSpec((1, LANES), lambda b, p: (0, p)),
                  pl.BlockSpec((1, LANES), lambda b, p: (0, p))],
        out_specs=pl.BlockSpec((1, t, LANES), lambda b, p: (b, 0, p)),
        out_shape=jax.ShapeDtypeStruct((bsz, t, RET_WIDTH), BF16),
        scratch_shapes=[pltpu.VMEM((n_chunks, CHUNK, LANES), F32),
                        pltpu.VMEM((n_chunks, CHUNK, LANES), F32)],
        compiler_params=_params("parallel", "parallel"),
        name="retention",
    )(qkv, qkv, qkv, gate, dec, gng, gnb)


def _split3(x):
    x1 = x.astype(BF16).astype(F32)
    r = x - x1
    x2 = r.astype(BF16).astype(F32)
    x3 = (r - x2).astype(BF16).astype(F32)
    return x1, x2, x3


def _diff_kernel(slopes_ref, trips_ref, q_ref, k_ref, v_ref, kfeat_ref, lam_ref, g_ref, o_ref,
                 vt_ref, dbias_ref, s_even_ref, s_odd_ref, acc_ref, qaug_ref, *, n_kv, n_q, lam0):
    c = slopes_ref[pl.program_id(1)] * LOG2E

    ones_row = (lax.broadcasted_iota(jnp.int32, (VT_PAD, KV_TILE), 0) == 0).astype(F32)
    for j in range(n_kv):
        vt = v_ref[0, j * KV_TILE:(j + 1) * KV_TILE, :].astype(F32).T
        vt_ref[j] = jnp.concatenate([vt, ones_row], axis=0).astype(BF16)
    kio = lax.broadcasted_iota(jnp.int32, (KV_TILE, Q_TILE), 0)
    qio = lax.broadcasted_iota(jnp.int32, (KV_TILE, Q_TILE), 1)
    dbias_ref[...] = -(c * jnp.abs(kio - qio).astype(F32))

    lv = lam_ref[...]
    lam = (jnp.exp(jnp.sum(lv[0:1] * lv[1:2], axis=-1, keepdims=True))
           - jnp.exp(jnp.sum(lv[2:3] * lv[3:4], axis=-1, keepdims=True)) + lam0)
    out_gain = g_ref[...] * (1.0 - lam0)

    feat = lax.broadcasted_iota(jnp.int32, (LANES, 1), 0)
    rowi = lax.broadcasted_iota(jnp.int32, (AUG_ROWS, Q_TILE), 0)
    lane_q = lax.broadcasted_iota(jnp.int32, (1, Q_TILE), 1)
    c_row = jnp.full((1, Q_TILE), c, F32)
    c_parts = _split3(c_row)
    bufs = (s_even_ref, s_odd_ref)

    def query_rows(qs):
        return pl.ds(pl.multiple_of(qs * Q_TILE, Q_TILE), Q_TILE)

    def key_rows(j):
        return pl.ds(pl.multiple_of(j * KV_TILE, KV_TILE), KV_TILE)

    def diag_tile(qs):
        return qs

    def tile_of(qs, pos):
        jd = diag_tile(qs)
        return jnp.where(pos == 0, jd, jnp.where(pos - 1 < jd, pos - 1, pos))

    def prepare_queries(qs, slot):
        qt = (q_ref[0, query_rows(qs), :].astype(F32) * (LOG2E * DIFF_DH ** -0.5)).T
        qpos = (qs * Q_TILE + lane_q).astype(F32)
        t_parts = _split3(-(c_row * qpos))
        coef = [float(FEAT_RADIX) * cp for cp in c_parts] + list(c_parts) + list(t_parts)
        aug = jnp.zeros((AUG_ROWS, Q_TILE), F32)
        for r, row in enumerate(coef):
            aug = jnp.where(rowi == r, row, aug)
        tail = jnp.concatenate([aug.astype(BF16),
                                jnp.zeros((LANES - AUG_ROWS, Q_TILE), BF16)], axis=0)
        qaug_ref[slot, 0] = jnp.concatenate([jnp.where(feat < DIFF_DH, qt, 0.0).astype(BF16), tail], axis=0)
        qaug_ref[slot, 1] = jnp.concatenate([jnp.where(feat < DIFF_DH, 0.0, qt).astype(BF16), tail], axis=0)

    def diag_scores(qs, slot):
        k_t = k_ref[0, key_rows(diag_tile(qs)), :]
        return lambda mp: jnp.dot(k_t, qaug_ref[slot, mp, 0:LANES, :],
                                  preferred_element_type=F32) + dbias_ref[...]

    def offdiag_scores(qs, slot, pos):
        j = tile_of(qs, pos)
        side = (j > diag_tile(qs)).astype(jnp.int32)
        lhs = jnp.concatenate([k_ref[0, key_rows(j), :], kfeat_ref[side, key_rows(j), :]], axis=1)
        return lambda mp: jnp.dot(lhs, qaug_ref[slot, mp], preferred_element_type=F32)

    def score_part(scores, mp, dst_ref):
        s = scores(mp)
        dst_ref[mp] = s
        return jnp.max(s, axis=0, keepdims=True)

    def softmax_part(slot, mp, vt_t, src_ref, col_max, m_old):
        m_new = jnp.maximum(m_old, col_max)
        alpha = jnp.exp2(m_old - m_new)
        p = jnp.exp2(src_ref[mp] - m_new).astype(BF16)
        acc_ref[slot, mp] = (alpha * acc_ref[slot, mp]
                             + jnp.dot(vt_t, p, preferred_element_type=F32))
        return m_new

    def stage_pair(scores, dst_ref, qs, slot, pos, src_ref, col_max, m_run):
        vt_t = vt_ref[tile_of(qs, pos)]
        new_cm, new_m = [], []
        for mp in range(2):
            new_cm.append(score_part(scores, mp, dst_ref))
            new_m.append(softmax_part(slot, mp, vt_t, src_ref, col_max[mp], m_run[mp]))
        return tuple(new_cm), tuple(new_m)

    def write_output(qs, slot):
        a0 = acc_ref[slot, 0]
        a1 = acc_ref[slot, 1]
        att = (a0[:DIFF_DV] / a0[DIFF_DV:DIFF_DV + 1]
               - lam * (a1[:DIFF_DV] / a1[DIFF_DV:DIFF_DV + 1]))
        ms = jnp.mean(att * att, axis=0, keepdims=True)
        att = (att * lax.rsqrt(ms + EPS)).T
        o_ref[0, query_rows(qs), :] = (att * out_gain).astype(BF16)

    def query_tile_body(qs, cm):
        slot = qs % 2
        prv = jnp.maximum(qs - 1, 0)
        nxt = jnp.minimum(qs + 1, n_q - 1)
        acc_ref[slot] = jnp.zeros(acc_ref.shape[1:], F32)
        m_init = jnp.full((1, Q_TILE), -1e30, F32)

        def pair_body(i, carry):
            m_run, cm_even = carry
            cm_odd, m_run = stage_pair(offdiag_scores(qs, slot, 2 * i + 1), s_odd_ref,
                                       qs, slot, 2 * i, s_even_ref, cm_even, m_run)
            cm_even, m_run = stage_pair(offdiag_scores(qs, slot, 2 * i + 2), s_even_ref,
                                        qs, slot, 2 * i + 1, s_odd_ref, cm_odd, m_run)
            return m_run, cm_even

        m_run, cm_even = lax.fori_loop(0, trips_ref[0], pair_body, ((m_init, m_init), cm))
        write_output(prv, 1 - slot)
        cm_odd, m_run = stage_pair(offdiag_scores(qs, slot, n_kv - 1), s_odd_ref,
                                   qs, slot, n_kv - 2, s_even_ref, cm_even, m_run)
        prepare_queries(nxt, 1 - slot)
        cm_next, _ = stage_pair(diag_scores(nxt, 1 - slot), s_even_ref,
                                qs, slot, n_kv - 1, s_odd_ref, cm_odd, m_run)
        return cm_next

    acc_ref[...] = jnp.ones(acc_ref.shape, F32)
    prepare_queries(0, 0)
    first = diag_scores(0, 0)
    lax.fori_loop(0, n_q, query_tile_body,
                  tuple(score_part(first, mp, s_even_ref) for mp in range(2)))
    write_output(n_q - 1, (n_q - 1) % 2)


def _key_features(t):
    pos = jnp.arange(t, dtype=jnp.int32)
    hi = (pos // FEAT_RADIX).astype(F32)
    lo = (pos % FEAT_RADIX).astype(F32)
    one = jnp.ones((t,), F32)
    cols = jnp.stack([hi, hi, hi, lo, lo, lo, one, one, one], axis=1)
    cols = jnp.pad(cols, ((0, 0), (0, LANES - cols.shape[1])))
    return jnp.stack([cols, -cols]).astype(BF16)


def _diff_attention(qkv, slopes, lam_rows, subln_g, lam0):
    bsz, t, _ = qkv.shape
    n_kv = t // KV_TILE
    assert n_kv % 2 == 0 and KV_TILE == Q_TILE and t <= FEAT_RADIX * 256

    def col(blk):
        return pl.BlockSpec((1, t, LANES), lambda b, h, *_: (b, 0, blk + h))

    return pl.pallas_call(
        functools.partial(_diff_kernel, n_kv=n_kv, n_q=t // Q_TILE, lam0=lam0),
        grid_spec=pltpu.PrefetchScalarGridSpec(
            num_scalar_prefetch=2,
            grid=(bsz, DIFF_HEADS),
            in_specs=[col(DQ_BLK), col(DK_BLK), col(DV_BLK),
                      pl.BlockSpec((2, t, LANES), lambda b, h, *_: (0, 0, 0),
                                   pipeline_mode=pl.Buffered(1)),
                      pl.BlockSpec((8, LANES), lambda b, h, *_: (0, 0)),
                      pl.BlockSpec((1, LANES), lambda b, h, *_: (0, 0))],
            out_specs=pl.BlockSpec((1, t, LANES), lambda b, h, *_: (b, 0, h)),
            scratch_shapes=[pltpu.VMEM((n_kv, DIFF_DV + VT_PAD, KV_TILE), BF16),
                            pltpu.VMEM((KV_TILE, Q_TILE), F32),
                            pltpu.VMEM((2, KV_TILE, Q_TILE), F32),
                            pltpu.VMEM((2, KV_TILE, Q_TILE), F32),
                            pltpu.VMEM((2, 2, DIFF_DV + VT_PAD, Q_TILE), F32),
                            pltpu.VMEM((2, 2, 2 * LANES, Q_TILE), BF16)]),
        out_shape=jax.ShapeDtypeStruct((bsz, t, DIFF_WIDTH), BF16),
        compiler_params=_params("parallel", "parallel"),
        name="diff_attention",
    )(slopes, jnp.full((1,), n_kv // 2 - 1, jnp.int32), qkv, qkv, qkv, _key_features(t),
      lam_rows, subln_g)


def _lambda_init(layer):
    return 0.8 - 0.6 * math.exp(-0.3 * layer)


def _trunk(x, p):
    bsz, t, _ = x.shape
    m = bsz * t
    xf = x.reshape(m, D_MODEL)
    for l in range(DEPTH):
        xf = _ffn_ln(xf, p["ffn1_w13"][l], p["ffn1_w2"][l], p["ln1_g"][l], p["ln1_b"][l])
        qkv, gate = _inproj(xf, p["w_in"][l])
        qkv = qkv.reshape(bsz, t, IN_WIDTH)
        ret = _retention(qkv, gate.reshape(bsz, t, RET_WIDTH), p["ret_dec"][l],
                         p["ret_gn_g"][l], p["ret_gn_b"][l])
        dif = _diff_attention(qkv, p["slopes"], p["lam_rows"][l], p["diff_subln_g"][l],
                              _lambda_init(l))
        xf = _ffn_ln(xf, p["ffn2_w13"][l], p["ffn2_w2"][l], p["ln3_g"][l], p["ln3_b"][l],
                     mix=(ret.reshape(m, RET_WIDTH), dif.reshape(m, DIFF_WIDTH),
                          p["w_out"][l], p["ln2_g"][l], p["ln2_b"][l]))
    return xf.reshape(bsz, t, D_MODEL)


def kernel(x_prompt, x_sample, w_in, w_out, ret_decay_f, ret_decay_b, ret_gn_g, ret_gn_b,
           diff_lq1, diff_lk1, diff_lq2, diff_lk2, diff_subln_g,
           ffn1_w13, ffn1_w2, ffn2_w13, ffn2_w2,
           ln1_g, ln1_b, ln2_g, ln2_b, ln3_g, ln3_b):
    n_pairs = RET_WIDTH // LANES

    def pair_rows(dec):
        return jnp.broadcast_to(dec.reshape(DEPTH, n_pairs, 2, 1), (DEPTH, n_pairs, 2, LANES))

    row = lambda a: a.reshape(DEPTH, 1, -1)
    lam_rows = jnp.stack([diff_lq1, diff_lk1, diff_lq2, diff_lk2], axis=1)
    lam_rows = jnp.pad(lam_rows, ((0, 0), (0, 4), (0, LANES - DIFF_DH)))
    p = {
        "w_in": w_in.astype(BF16), "w_out": w_out.astype(BF16),
        "ffn1_w13": ffn1_w13.astype(BF16), "ffn1_w2": ffn1_w2.astype(BF16),
        "ffn2_w13": ffn2_w13.astype(BF16), "ffn2_w2": ffn2_w2.astype(BF16),
        "ret_dec": jnp.concatenate([pair_rows(ret_decay_f), pair_rows(ret_decay_b)], axis=2),
        "ret_gn_g": row(ret_gn_g), "ret_gn_b": row(ret_gn_b),
        "lam_rows": lam_rows, "diff_subln_g": row(diff_subln_g),
        "slopes": jnp.asarray([2.0 ** (-8.0 * (h + 1) / DIFF_HEADS) for h in range(DIFF_HEADS)], F32),
        "ln1_g": row(ln1_g), "ln1_b": row(ln1_b), "ln2_g": row(ln2_g), "ln2_b": row(ln2_b),
        "ln3_g": row(ln3_g), "ln3_b": row(ln3_b),
    }
    return _trunk(x_prompt, p), _trunk(x_sample, p)
```

```python
import functools
import math

import jax
import jax.numpy as jnp
from jax import lax
from jax.experimental import pallas as pl
from jax.experimental.pallas import tpu as pltpu

D_MODEL = 1024
DEPTH = 2
RET_HEADS = 8
RET_DK = 64
RET_WIDTH = 512
DIFF_HEADS = 4
DIFF_DH = 64
DIFF_DV = 128
DIFF_WIDTH = 512
IN_WIDTH = 3584
D_FF = 2816
CHUNK = 128
EPS = 1e-5
ALPHA = (2 * DEPTH) ** 0.25

LANES = 128
FF_CHUNK = D_FF
ROW_TILE = 512
FFN_ROW_TILE = 1024
ROW_SUB = 256
Q_TILE = 512
KV_TILE = 512
VT_PAD = 16
AUG_ROWS = 16
FEAT_RADIX = 64
LOG2E = math.log2(math.e)
VMEM_LIMIT = 56 * 1024 * 1024

RQ_BLK, RK_BLK, RV_BLK, RG_BLK, DQ_BLK, DK_BLK, DV_BLK = 0, 4, 8, 12, 16, 20, 24

BF16 = jnp.bfloat16
F32 = jnp.float32
NT_DIMS = (((1,), (1,)), ((), ()))


def _resident(shape):
    nd = len(shape)
    return pl.BlockSpec(shape, lambda *_: (0,) * nd, pipeline_mode=pl.Buffered(1))


def _layer_norm(y, g, b):
    mu = jnp.mean(y, axis=-1, keepdims=True)
    d = y - mu
    var = jnp.mean(d * d, axis=-1, keepdims=True)
    return d * lax.rsqrt(var + EPS) * g + b


def _params(*sem):
    return pltpu.CompilerParams(dimension_semantics=sem, vmem_limit_bytes=VMEM_LIMIT)


def _ffn_residual_ln(x, w13_ref, w2_ref, g_ref, b_ref):
    xb = x.astype(BF16)
    acc = None
    for lo in range(0, D_FF, FF_CHUNK):
        a = jnp.dot(xb, w13_ref[:, lo:lo + FF_CHUNK], preferred_element_type=F32)
        b = jnp.dot(xb, w13_ref[:, D_FF + lo:D_FF + lo + FF_CHUNK], preferred_element_type=F32)
        h = (jax.nn.silu(a) * b).astype(BF16)
        part = jnp.dot(h, w2_ref[lo:lo + FF_CHUNK, :], preferred_element_type=F32)
        acc = part if acc is None else acc + part
    return _layer_norm(ALPHA * x + 0.5 * acc, g_ref[...], b_ref[...])


def _ffn_ln_kernel(x_ref, w13_ref, w2_ref, g_ref, b_ref, o_ref):
    for r in range(0, FFN_ROW_TILE, ROW_SUB):
        o_ref[r:r + ROW_SUB, :] = _ffn_residual_ln(x_ref[r:r + ROW_SUB, :], w13_ref, w2_ref,
                                                   g_ref, b_ref)


def _mix_ffn_ln_kernel(x_ref, r_ref, dt_ref, wo_ref, gm_ref, bm_ref,
                       w13_ref, w2_ref, g_ref, b_ref, o_ref):
    def mixed(r):
        rows = slice(r, r + ROW_SUB)
        tile, col = divmod(r, Q_TILE)
        d = jnp.concatenate([dt_ref[0, h, tile, :, col:col + ROW_SUB].astype(F32).T.astype(BF16)
                             for h in range(DIFF_HEADS)], axis=1)
        mix = jnp.dot(r_ref[rows, :], wo_ref[0:RET_WIDTH, :], preferred_element_type=F32)
        mix = mix + jnp.dot(d, wo_ref[RET_WIDTH:, :], preferred_element_type=F32)
        return _layer_norm(ALPHA * x_ref[rows, :] + mix, gm_ref[...], bm_ref[...])

    x = mixed(0)
    for r in range(0, FFN_ROW_TILE, ROW_SUB):
        x_next = mixed(r + ROW_SUB) if r + ROW_SUB < FFN_ROW_TILE else None
        o_ref[r:r + ROW_SUB, :] = _ffn_residual_ln(x, w13_ref, w2_ref, g_ref, b_ref)
        x = x_next


def _ffn_ln(x, w13, w2, g, b, mix=None):
    m = x.shape[0]
    row = pl.BlockSpec((FFN_ROW_TILE, D_MODEL), lambda i: (i, 0))
    vec = _resident((1, D_MODEL))
    ffn_specs = [_resident(w13.shape), _resident(w2.shape), vec, vec]
    if mix is None:
        body, args, specs = _ffn_ln_kernel, (x, w13, w2, g, b), [row] + ffn_specs
    else:
        ret, dif, w_out, gm, bm = mix
        half = pl.BlockSpec((FFN_ROW_TILE, RET_WIDTH), lambda i: (i, 0))
        per_step = FFN_ROW_TILE // Q_TILE
        steps = dif.shape[2] // per_step
        dif_spec = pl.BlockSpec((1, DIFF_HEADS, per_step, DIFF_DV, Q_TILE),
                                lambda i: (i // steps, 0, i % steps, 0, 0))
        body = _mix_ffn_ln_kernel
        args = (x, ret, dif, w_out, gm, bm, w13, w2, g, b)
        specs = [row, half, dif_spec, _resident(w_out.shape), vec, vec] + ffn_specs
    return pl.pallas_call(
        body,
        grid=(m // FFN_ROW_TILE,),
        in_specs=specs,
        out_specs=row,
        out_shape=jax.ShapeDtypeStruct((m, D_MODEL), F32),
        compiler_params=_params("parallel"),
        name="ffn_ln" if mix is None else "mix_ffn_ln",
    )(*args)


def _inproj_kernel(x_ref, w_ref, o_ref, gate_ref, qt_ref, vt_ref):
    y = jnp.dot(x_ref[...].astype(BF16), w_ref[...], preferred_element_type=F32)
    o_ref[...] = y.astype(BF16)
    gate_ref[...] = y[:, RG_BLK * LANES:RG_BLK * LANES + RET_WIDTH]
    ones_row = (lax.broadcasted_iota(jnp.int32, (VT_PAD, ROW_TILE), 0) == 0).astype(F32)
    for h in range(DIFF_HEADS):
        q = y[:, (DQ_BLK + h) * LANES:(DQ_BLK + h + 1) * LANES] * (LOG2E * DIFF_DH ** -0.5)
        qt_ref[0, h, 0] = q.T.astype(BF16)
        v = y[:, (DV_BLK + h) * LANES:(DV_BLK + h + 1) * LANES]
        vt_ref[0, h, 0] = jnp.concatenate([v.T, ones_row], axis=0).astype(BF16)


def _inproj(x, w, bsz, t):
    m = x.shape[0]
    tiles = t // ROW_TILE

    def feature_major(rows):
        return pl.BlockSpec((1, DIFF_HEADS, 1, rows, ROW_TILE),
                            lambda i: (i // tiles, 0, i % tiles, 0, 0))

    return pl.pallas_call(
        _inproj_kernel,
        grid=(m // ROW_TILE,),
        in_specs=[pl.BlockSpec((ROW_TILE, D_MODEL), lambda i: (i, 0)), _resident(w.shape)],
        out_specs=[pl.BlockSpec((ROW_TILE, IN_WIDTH), lambda i: (i, 0)),
                   pl.BlockSpec((ROW_TILE, RET_WIDTH), lambda i: (i, 0)),
                   feature_major(DIFF_DV), feature_major(DIFF_DV + VT_PAD)],
        out_shape=[jax.ShapeDtypeStruct((m, IN_WIDTH), BF16),
                   jax.ShapeDtypeStruct((m, RET_WIDTH), F32),
                   jax.ShapeDtypeStruct((bsz, DIFF_HEADS, tiles, DIFF_DV, ROW_TILE), BF16),
                   jax.ShapeDtypeStruct((bsz, DIFF_HEADS, tiles, DIFF_DV + VT_PAD, ROW_TILE), BF16)],
        compiler_params=_params("parallel"),
        name="inproj",
    )(x, w)


def _ret_kernel(q_ref, k_ref, v_ref, gate_ref, dec_ref, gng_ref, gnb_ref, o_ref,
                sf_ref, sb_ref, *, n_chunks):
    c_len = CHUNK
    k_scale = RET_DK ** -0.5
    lane = lax.broadcasted_iota(jnp.int32, (1, LANES), 1)
    lo_half = lane < RET_DK
    m_lo = lo_half.astype(F32)
    m_hi = 1.0 - m_lo
    lg = jax.nn.log_sigmoid(dec_ref[0])
    lgf0, lgf1, lgb0, lgb1 = lg[0:1], lg[1:2], lg[2:3], lg[3:4]
    lgf = jnp.where(lo_half, lgf0, lgf1)
    lgb = jnp.where(lo_half, lgb0, lgb1)

    ri = lax.broadcasted_iota(jnp.int32, (c_len, c_len), 0)
    ci = lax.broadcasted_iota(jnp.int32, (c_len, c_len), 1)
    dist = (ri - ci).astype(F32)
    causal = dist >= 0.0
    block_diag = (ri < RET_DK) == (ci < RET_DK)

    def decay_tile(lf, lb):
        return jnp.where(causal, jnp.exp(lf * jnp.maximum(dist, 0.0)),
                         jnp.exp(lb * jnp.maximum(-dist, 0.0))) * k_scale

    d0 = decay_tile(lgf0, lgb0)
    d1 = decay_tile(lgf1, lgb1)
    row = lax.broadcasted_iota(jnp.int32, (c_len, LANES), 0).astype(F32)
    qdf = jnp.exp(lgf * (row + 1.0))
    qdb = jnp.exp(lgb * (c_len - row))
    kdf = jnp.exp(lgf * (c_len - 1.0 - row)) * k_scale
    kdb = jnp.exp(lgb * row) * k_scale
    gcf = jnp.exp(lgf * c_len)
    gcb = jnp.exp(lgb * c_len)

    def rows_of(c):
        return pl.ds(pl.multiple_of(c * c_len, c_len), c_len)

    def kv_body(c, carry):
        rows = rows_of(c)
        k = k_ref[0, rows, :].astype(F32)
        v = v_ref[0, rows, :]
        kvf = jnp.dot((k * kdf).T.astype(BF16), v, preferred_element_type=F32)
        kvb = jnp.dot((k * kdb).T.astype(BF16), v, preferred_element_type=F32)
        sf_ref[c] = jnp.where(block_diag, kvf, 0.0)
        sb_ref[c] = jnp.where(block_diag, kvb, 0.0)
        return carry

    lax.fori_loop(0, n_chunks, kv_body, 0, unroll=16)

    def fwd_scan(c, s):
        kv = sf_ref[c]
        sf_ref[c] = s
        return s * gcf + kv

    lax.fori_loop(0, n_chunks, fwd_scan, jnp.zeros((c_len, LANES), F32))

    def bwd_scan(i, s):
        c = n_chunks - 1 - i
        kv = sb_ref[c]
        sb_ref[c] = s
        return s * gcb + kv

    lax.fori_loop(0, n_chunks, bwd_scan, jnp.zeros((c_len, LANES), F32))

    gng = gng_ref[...]
    gnb = gnb_ref[...]
    inv_dv = 1.0 / RET_DK

    def out_body(c, carry):
        rows = rows_of(c)
        q = q_ref[0, rows, :].astype(F32)
        k = k_ref[0, rows, :]
        v = v_ref[0, rows, :].astype(F32)
        s0 = lax.dot_general((q * m_lo).astype(BF16), k, NT_DIMS, preferred_element_type=F32)
        s1 = lax.dot_general((q * m_hi).astype(BF16), k, NT_DIMS, preferred_element_type=F32)
        o = jnp.dot((s0 * d0).astype(BF16), (v * m_lo).astype(BF16), preferred_element_type=F32)
        o = o + jnp.dot((s1 * d1).astype(BF16), (v * m_hi).astype(BF16), preferred_element_type=F32)
        o = o + jnp.dot((q * qdf).astype(BF16), sf_ref[c].astype(BF16), preferred_element_type=F32)
        o = o + jnp.dot((q * qdb).astype(BF16), sb_ref[c].astype(BF16), preferred_element_type=F32)
        s_lo = jnp.sum(o * m_lo, axis=-1, keepdims=True)
        s_hi = jnp.sum(o * m_hi, axis=-1, keepdims=True)
        d = o - jnp.where(lo_half, s_lo, s_hi) * inv_dv
        dd = d * d
        v_lo = jnp.sum(dd * m_lo, axis=-1, keepdims=True)
        v_hi = jnp.sum(dd * m_hi, axis=-1, keepdims=True)
        var = jnp.where(lo_half, v_lo, v_hi) * inv_dv
        y = d * lax.rsqrt(var + EPS) * gng + gnb
        o_ref[0, rows, :] = (y * jax.nn.silu(gate_ref[0, rows, :])).astype(BF16)
        return carry

    lax.fori_loop(0, n_chunks, out_body, 0, unroll=16)


def _retention(qkv, gate, dec, gng, gnb):
    bsz, t, _ = qkv.shape
    n_chunks = t // CHUNK
    n_pairs = RET_WIDTH // LANES

    def col(blk):
        return pl.BlockSpec((1, t, LANES), lambda b, p: (b, 0, blk + p))

    return pl.pallas_call(
        functools.partial(_ret_kernel, n_chunks=n_chunks),
        grid=(bsz, n_pairs),
        in_specs=[col(RQ_BLK), col(RK_BLK), col(RV_BLK),
                  pl.BlockSpec((1, t, LANES), lambda b, p: (b, 0, p)),
                  pl.BlockSpec((1, 4, LANES), lambda b, p: (p, 0, 0)),
                  pl.BlockSpec((1, LANES), lambda b, p: (0, p)),
                  pl.BlockSpec((1, LANES), lambda b, p: (0, p))],
        out_specs=pl.BlockSpec((1, t, LANES), lambda b, p: (b, 0, p)),
        out_shape=jax.ShapeDtypeStruct((bsz, t, RET_WIDTH), BF16),
        scratch_shapes=[pltpu.VMEM((n_chunks, CHUNK, LANES), F32),
                        pltpu.VMEM((n_chunks, CHUNK, LANES), F32)],
        compiler_params=_params("parallel", "parallel"),
        name="retention",
    )(qkv, qkv, qkv, gate, dec, gng, gnb)


def _split3(x):
    x1 = x.astype(BF16).astype(F32)
    r = x - x1
    x2 = r.astype(BF16).astype(F32)
    x3 = (r - x2).astype(BF16).astype(F32)
    return x1, x2, x3


def _diff_kernel(slopes_ref, trips_ref, qt_ref, k_ref, vt_ref, kfeat_ref, lam_ref, g_ref, o_ref,
                 dbias_ref, s_even_ref, s_odd_ref, acc_ref, qaug_ref, *, n_kv, n_q, lam0):
    c = slopes_ref[pl.program_id(1)] * LOG2E

    kio = lax.broadcasted_iota(jnp.int32, (KV_TILE, Q_TILE), 0)
    qio = lax.broadcasted_iota(jnp.int32, (KV_TILE, Q_TILE), 1)
    dbias_ref[...] = -(c * jnp.abs(kio - qio).astype(F32))

    lv = lam_ref[...]
    lam = (jnp.exp(jnp.sum(lv[0:1] * lv[1:2], axis=-1, keepdims=True))
           - jnp.exp(jnp.sum(lv[2:3] * lv[3:4], axis=-1, keepdims=True)) + lam0)
    out_gain = g_ref[...] * (1.0 - lam0)

    feat = lax.broadcasted_iota(jnp.int32, (LANES, 1), 0)
    rowi = lax.broadcasted_iota(jnp.int32, (AUG_ROWS, Q_TILE), 0)
    lane_q = lax.broadcasted_iota(jnp.int32, (1, Q_TILE), 1)
    c_row = jnp.full((1, Q_TILE), c, F32)
    c_parts = _split3(c_row)
    bufs = (s_even_ref, s_odd_ref)

    def key_rows(j):
        return pl.ds(pl.multiple_of(j * KV_TILE, KV_TILE), KV_TILE)

    def diag_tile(qs):
        return qs

    def tile_of(qs, pos):
        jd = diag_tile(qs)
        return jnp.where(pos == 0, jd, jnp.where(pos - 1 < jd, pos - 1, pos))

    def prepare_queries(qs, slot):
        qt = qt_ref[0, 0, qs].astype(F32)
        qpos = (qs * Q_TILE + lane_q).astype(F32)
        t_parts = _split3(-(c_row * qpos))
        coef = [float(FEAT_RADIX) * cp for cp in c_parts] + list(c_parts) + list(t_parts)
        aug = jnp.zeros((AUG_ROWS, Q_TILE), F32)
        for r, row in enumerate(coef):
            aug = jnp.where(rowi == r, row, aug)
        tail = jnp.concatenate([aug.astype(BF16),
                                jnp.zeros((LANES - AUG_ROWS, Q_TILE), BF16)], axis=0)
        qaug_ref[slot, 0] = jnp.concatenate([jnp.where(feat < DIFF_DH, qt, 0.0).astype(BF16), tail], axis=0)
        qaug_ref[slot, 1] = jnp.concatenate([jnp.where(feat < DIFF_DH, 0.0, qt).astype(BF16), tail], axis=0)

    def diag_scores(qs, slot):
        k_t = k_ref[0, key_rows(diag_tile(qs)), :]
        return lambda mp: jnp.dot(k_t, qaug_ref[slot, mp, 0:LANES, :],
                                  preferred_element_type=F32) + dbias_ref[...]

    def offdiag_scores(qs, slot, pos):
        j = tile_of(qs, pos)
        side = (j > diag_tile(qs)).astype(jnp.int32)
        lhs = jnp.concatenate([k_ref[0, key_rows(j), :], kfeat_ref[side, key_rows(j), :]], axis=1)
        return lambda mp: jnp.dot(lhs, qaug_ref[slot, mp], preferred_element_type=F32)

    def score_part(scores, mp, dst_ref):
        s = scores(mp)
        dst_ref[mp] = s
        return jnp.max(s, axis=0, keepdims=True)

    def softmax_part(slot, mp, vt_t, src_ref, col_max, m_old):
        m_new = jnp.maximum(m_old, col_max)
        alpha = jnp.exp2(m_old - m_new)
        p = jnp.exp2(src_ref[mp] - m_new).astype(BF16)
        acc_ref[slot, mp] = (alpha * acc_ref[slot, mp]
                             + jnp.dot(vt_t, p, preferred_element_type=F32))
        return m_new

    def stage_pair(scores, dst_ref, qs, slot, pos, src_ref, col_max, m_run):
        vt_t = vt_ref[0, 0, tile_of(qs, pos)]
        new_cm, new_m = [], []
        for mp in range(2):
            new_cm.append(score_part(scores, mp, dst_ref))
            new_m.append(softmax_part(slot, mp, vt_t, src_ref, col_max[mp], m_run[mp]))
        return tuple(new_cm), tuple(new_m)

    def write_output(qs, slot):
        a0 = acc_ref[slot, 0]
        a1 = acc_ref[slot, 1]
        att = (a0[:DIFF_DV] / a0[DIFF_DV:DIFF_DV + 1]
               - lam * (a1[:DIFF_DV] / a1[DIFF_DV:DIFF_DV + 1]))
        ms = jnp.mean(att * att, axis=0, keepdims=True)
        o_ref[0, 0, qs] = (att * lax.rsqrt(ms + EPS) * out_gain).astype(BF16)

    def query_tile_body(qs, cm):
        slot = qs % 2
        prv = jnp.maximum(qs - 1, 0)
        nxt = jnp.minimum(qs + 1, n_q - 1)
        acc_ref[slot] = jnp.zeros(acc_ref.shape[1:], F32)
        m_init = jnp.full((1, Q_TILE), -1e30, F32)

        def pair_body(i, carry):
            m_run, cm_even = carry
            cm_odd, m_run = stage_pair(offdiag_scores(qs, slot, 2 * i + 1), s_odd_ref,
                                       qs, slot, 2 * i, s_even_ref, cm_even, m_run)
            cm_even, m_run = stage_pair(offdiag_scores(qs, slot, 2 * i + 2), s_even_ref,
                                        qs, slot, 2 * i + 1, s_odd_ref, cm_odd, m_run)
            return m_run, cm_even

        m_run, cm_even = lax.fori_loop(0, trips_ref[0], pair_body, ((m_init, m_init), cm))
        write_output(prv, 1 - slot)
        cm_odd, m_run = stage_pair(offdiag_scores(qs, slot, n_kv - 1), s_odd_ref,
                                   qs, slot, n_kv - 2, s_even_ref, cm_even, m_run)
        prepare_queries(nxt, 1 - slot)
        cm_next, _ = stage_pair(diag_scores(nxt, 1 - slot), s_even_ref,
                                qs, slot, n_kv - 1, s_odd_ref, cm_odd, m_run)
        return cm_next

    acc_ref[...] = jnp.ones(acc_ref.shape, F32)
    prepare_queries(0, 0)
    first = diag_scores(0, 0)
    lax.fori_loop(0, n_q, query_tile_body,
                  tuple(score_part(first, mp, s_even_ref) for mp in range(2)))
    write_output(n_q - 1, (n_q - 1) % 2)


def _key_features(t):
    pos = jnp.arange(t, dtype=jnp.int32)
    hi = (pos // FEAT_RADIX).astype(F32)
    lo = (pos % FEAT_RADIX).astype(F32)
    one = jnp.ones((t,), F32)
    cols = jnp.stack([hi, hi, hi, lo, lo, lo, one, one, one], axis=1)
    cols = jnp.pad(cols, ((0, 0), (0, LANES - cols.shape[1])))
    return jnp.stack([cols, -cols]).astype(BF16)


def _diff_attention(qkv, qt, vt, slopes, lam_rows, subln_g, lam0):
    bsz, t, _ = qkv.shape
    n_kv = t // KV_TILE
    assert n_kv % 2 == 0 and KV_TILE == Q_TILE == ROW_TILE and t <= FEAT_RADIX * 256

    def feature_major(tiles, rows):
        return pl.BlockSpec((1, 1, tiles, rows, Q_TILE), lambda b, h, *_: (b, h, 0, 0, 0))

    return pl.pallas_call(
        functools.partial(_diff_kernel, n_kv=n_kv, n_q=t // Q_TILE, lam0=lam0),
        grid_spec=pltpu.PrefetchScalarGridSpec(
            num_scalar_prefetch=2,
            grid=(bsz, DIFF_HEADS),
            in_specs=[feature_major(n_kv, DIFF_DV),
                      pl.BlockSpec((1, t, LANES), lambda b, h, *_: (b, 0, DK_BLK + h)),
                      feature_major(n_kv, DIFF_DV + VT_PAD),
                      pl.BlockSpec((2, t, LANES), lambda b, h, *_: (0, 0, 0),
                                   pipeline_mode=pl.Buffered(1)),
                      pl.BlockSpec((8, LANES), lambda b, h, *_: (0, 0)),
                      pl.BlockSpec((DIFF_DV, 1), lambda b, h, *_: (0, 0))],
            out_specs=feature_major(n_kv, DIFF_DV),
            scratch_shapes=[pltpu.VMEM((KV_TILE, Q_TILE), F32),
                            pltpu.VMEM((2, KV_TILE, Q_TILE), F32),
                            pltpu.VMEM((2, KV_TILE, Q_TILE), F32),
                            pltpu.VMEM((2, 2, DIFF_DV + VT_PAD, Q_TILE), F32),
                            pltpu.VMEM((2, 2, 2 * LANES, Q_TILE), BF16)]),
        out_shape=jax.ShapeDtypeStruct((bsz, DIFF_HEADS, n_kv, DIFF_DV, Q_TILE), BF16),
        compiler_params=_params("parallel", "parallel"),
        name="diff_attention",
    )(slopes, jnp.full((1,), n_kv // 2 - 1, jnp.int32), qt, qkv, vt, _key_features(t),
      lam_rows, subln_g)


def _lambda_init(layer):
    return 0.8 - 0.6 * math.exp(-0.3 * layer)


def _trunk(x, p):
    bsz, t, _ = x.shape
    m = bsz * t
    xf = x.reshape(m, D_MODEL)
    for l in range(DEPTH):
        xf = _ffn_ln(xf, p["ffn1_w13"][l], p["ffn1_w2"][l], p["ln1_g"][l], p["ln1_b"][l])
        qkv, gate, qt, vt = _inproj(xf, p["w_in"][l], bsz, t)
        qkv = qkv.reshape(bsz, t, IN_WIDTH)
        ret = _retention(qkv, gate.reshape(bsz, t, RET_WIDTH), p["ret_dec"][l],
                         p["ret_gn_g"][l], p["ret_gn_b"][l])
        dif = _diff_attention(qkv, qt, vt, p["slopes"], p["lam_rows"][l],
                              p["diff_subln_g"][l], _lambda_init(l))
        xf = _ffn_ln(xf, p["ffn2_w13"][l], p["ffn2_w2"][l], p["ln3_g"][l], p["ln3_b"][l],
                     mix=(ret.reshape(m, RET_WIDTH), dif,
                          p["w_out"][l], p["ln2_g"][l], p["ln2_b"][l]))
    return xf.reshape(bsz, t, D_MODEL)


def kernel(x_prompt, x_sample, w_in, w_out, ret_decay_f, ret_decay_b, ret_gn_g, ret_gn_b,
           diff_lq1, diff_lk1, diff_lq2, diff_lk2, diff_subln_g,
           ffn1_w13, ffn1_w2, ffn2_w13, ffn2_w2,
           ln1_g, ln1_b, ln2_g, ln2_b, ln3_g, ln3_b):
    n_pairs = RET_WIDTH // LANES

    def pair_rows(dec):
        return jnp.broadcast_to(dec.reshape(DEPTH, n_pairs, 2, 1), (DEPTH, n_pairs, 2, LANES))

    row = lambda a: a.reshape(DEPTH, 1, -1)
    lam_rows = jnp.stack([diff_lq1, diff_lk1, diff_lq2, diff_lk2], axis=1)
    lam_rows = jnp.pad(lam_rows, ((0, 0), (0, 4), (0, LANES - DIFF_DH)))
    p = {
        "w_in": w_in.astype(BF16), "w_out": w_out.astype(BF16),
        "ffn1_w13": ffn1_w13.astype(BF16), "ffn1_w2": ffn1_w2.astype(BF16),
        "ffn2_w13": ffn2_w13.astype(BF16), "ffn2_w2": ffn2_w2.astype(BF16),
        "ret_dec": jnp.concatenate([pair_rows(ret_decay_f), pair_rows(ret_decay_b)], axis=2),
        "ret_gn_g": row(ret_gn_g), "ret_gn_b": row(ret_gn_b),
        "lam_rows": lam_rows, "diff_subln_g": diff_subln_g.reshape(DEPTH, DIFF_DV, 1),
        "slopes": jnp.asarray([2.0 ** (-8.0 * (h + 1) / DIFF_HEADS) for h in range(DIFF_HEADS)], F32),
        "ln1_g": row(ln1_g), "ln1_b": row(ln1_b), "ln2_g": row(ln2_g), "ln2_b": row(ln2_b),
        "ln3_g": row(ln3_g), "ln3_b": row(ln3_b),
    }
    return _trunk(x_prompt, p), _trunk(x_sample, p)
```

```python
import functools
import math

import jax
import jax.numpy as jnp
from jax import lax
from jax.experimental import pallas as pl
from jax.experimental.pallas import tpu as pltpu

D_MODEL = 1024
DEPTH = 2
RET_HEADS = 8
RET_DK = 64
RET_WIDTH = 512
DIFF_HEADS = 4
DIFF_DH = 64
DIFF_DV = 128
DIFF_WIDTH = 512
IN_WIDTH = 3584
D_FF = 2816
CHUNK = 128
EPS = 1e-5
ALPHA = (2 * DEPTH) ** 0.25

LANES = 128
FF_CHUNK = D_FF
ROW_TILE = 512
FFN_ROW_TILE = 1024
ROW_SUB = 256
Q_TILE = 512
KV_TILE = 512
VT_PAD = 16
AUG_ROWS = 16
FEAT_RADIX = 64
LOG2E = math.log2(math.e)
VMEM_LIMIT = 56 * 1024 * 1024

RQ_COL, RK_COL, RV_COL, RG_COL, DQ_COL, DK_COL, DV_COL = 0, 512, 1024, 1536, 2048, 2560, 3072
RQ_BLK, RK_BLK, RV_BLK, RG_BLK, DK_BLK = 0, 4, 8, 12, 16
TOK_WIDTH = 2560

BF16 = jnp.bfloat16
F32 = jnp.float32
NT_DIMS = (((1,), (1,)), ((), ()))


def _resident(shape):
    nd = len(shape)
    return pl.BlockSpec(shape, lambda *_: (0,) * nd, pipeline_mode=pl.Buffered(1))


def _layer_norm(y, g, b):
    mu = jnp.mean(y, axis=-1, keepdims=True)
    d = y - mu
    var = jnp.mean(d * d, axis=-1, keepdims=True)
    return d * lax.rsqrt(var + EPS) * g + b


def _params(*sem):
    return pltpu.CompilerParams(dimension_semantics=sem, vmem_limit_bytes=VMEM_LIMIT)


def _ffn_residual_ln(x, w13_ref, w2_ref, g_ref, b_ref):
    xb = x.astype(BF16)
    acc = None
    for lo in range(0, D_FF, FF_CHUNK):
        a = jnp.dot(xb, w13_ref[:, lo:lo + FF_CHUNK], preferred_element_type=F32)
        b = jnp.dot(xb, w13_ref[:, D_FF + lo:D_FF + lo + FF_CHUNK], preferred_element_type=F32)
        h = (jax.nn.silu(a) * b).astype(BF16)
        part = jnp.dot(h, w2_ref[lo:lo + FF_CHUNK, :], preferred_element_type=F32)
        acc = part if acc is None else acc + part
    return _layer_norm(ALPHA * x + 0.5 * acc, g_ref[...], b_ref[...])


def _ffn_ln_kernel(x_ref, w13_ref, w2_ref, g_ref, b_ref, o_ref):
    for r in range(0, FFN_ROW_TILE, ROW_SUB):
        o_ref[r:r + ROW_SUB, :] = _ffn_residual_ln(x_ref[r:r + ROW_SUB, :], w13_ref, w2_ref,
                                                   g_ref, b_ref)


def _mix_ffn_ln_kernel(x_ref, r_ref, dt_ref, wo_ref, gm_ref, bm_ref,
                       w13_ref, w2_ref, g_ref, b_ref, o_ref):
    def mixed(r):
        rows = slice(r, r + ROW_SUB)
        tile, col = divmod(r, Q_TILE)
        d = jnp.concatenate([dt_ref[0, h, tile, :, col:col + ROW_SUB].astype(F32).T.astype(BF16)
                             for h in range(DIFF_HEADS)], axis=1)
        mix = jnp.dot(r_ref[rows, :], wo_ref[0:RET_WIDTH, :], preferred_element_type=F32)
        mix = mix + jnp.dot(d, wo_ref[RET_WIDTH:, :], preferred_element_type=F32)
        return _layer_norm(ALPHA * x_ref[rows, :] + mix, gm_ref[...], bm_ref[...])

    x = mixed(0)
    for r in range(0, FFN_ROW_TILE, ROW_SUB):
        x_next = mixed(r + ROW_SUB) if r + ROW_SUB < FFN_ROW_TILE else None
        o_ref[r:r + ROW_SUB, :] = _ffn_residual_ln(x, w13_ref, w2_ref, g_ref, b_ref)
        x = x_next


def _ffn_ln(x, w13, w2, g, b, mix=None):
    m = x.shape[0]
    row = pl.BlockSpec((FFN_ROW_TILE, D_MODEL), lambda i: (i, 0))
    vec = _resident((1, D_MODEL))
    ffn_specs = [_resident(w13.shape), _resident(w2.shape), vec, vec]
    if mix is None:
        body, args, specs = _ffn_ln_kernel, (x, w13, w2, g, b), [row] + ffn_specs
    else:
        ret, dif, w_out, gm, bm = mix
        half = pl.BlockSpec((FFN_ROW_TILE, RET_WIDTH), lambda i: (i, 0))
        per_step = FFN_ROW_TILE // Q_TILE
        steps = dif.shape[2] // per_step
        dif_spec = pl.BlockSpec((1, DIFF_HEADS, per_step, DIFF_DV, Q_TILE),
                                lambda i: (i // steps, 0, i % steps, 0, 0))
        body = _mix_ffn_ln_kernel
        args = (x, ret, dif, w_out, gm, bm, w13, w2, g, b)
        specs = [row, half, dif_spec, _resident(w_out.shape), vec, vec] + ffn_specs
    return pl.pallas_call(
        body,
        grid=(m // FFN_ROW_TILE,),
        in_specs=specs,
        out_specs=row,
        out_shape=jax.ShapeDtypeStruct((m, D_MODEL), F32),
        compiler_params=_params("parallel"),
        name="ffn_ln" if mix is None else "mix_ffn_ln",
    )(*args)


def _inproj_kernel(x_ref, wf_ref, wt_ref, o_ref, gate_ref, qt_ref, vt_ref):
    ones_row = (lax.broadcasted_iota(jnp.int32, (VT_PAD, ROW_SUB), 0) == 0).astype(F32)
    for r in range(0, ROW_TILE, ROW_SUB):
        rows = slice(r, r + ROW_SUB)
        xb = x_ref[rows, :].astype(BF16)
        yf = jnp.dot(xb, wf_ref[...], preferred_element_type=F32)
        yt = jnp.dot(xb, wt_ref[...], preferred_element_type=F32)
        for h in range(DIFF_HEADS):
            q = yf[:, h * LANES:(h + 1) * LANES] * (LOG2E * DIFF_DH ** -0.5)
            qt_ref[0, h, 0, :, rows] = q.T.astype(BF16)
            v = yf[:, DIFF_WIDTH + h * LANES:DIFF_WIDTH + (h + 1) * LANES]
            vt_ref[0, h, 0, :, rows] = jnp.concatenate([v.T, ones_row], axis=0).astype(BF16)
        o_ref[rows, :] = yt.astype(BF16)
        gate_ref[rows, :] = yt[:, RG_BLK * LANES:RG_BLK * LANES + RET_WIDTH]


def _inproj(x, wf, wt, bsz, t):
    m = x.shape[0]
    tiles = t // ROW_TILE

    def feature_major(rows):
        return pl.BlockSpec((1, DIFF_HEADS, 1, rows, ROW_TILE),
                            lambda i: (i // tiles, 0, i % tiles, 0, 0))

    return pl.pallas_call(
        _inproj_kernel,
        grid=(m // ROW_TILE,),
        in_specs=[pl.BlockSpec((ROW_TILE, D_MODEL), lambda i: (i, 0)),
                  _resident(wf.shape), _resident(wt.shape)],
        out_specs=[pl.BlockSpec((ROW_TILE, TOK_WIDTH), lambda i: (i, 0)),
                   pl.BlockSpec((ROW_TILE, RET_WIDTH), lambda i: (i, 0)),
                   feature_major(DIFF_DV), feature_major(DIFF_DV + VT_PAD)],
        out_shape=[jax.ShapeDtypeStruct((m, TOK_WIDTH), BF16),
                   jax.ShapeDtypeStruct((m, RET_WIDTH), F32),
                   jax.ShapeDtypeStruct((bsz, DIFF_HEADS, tiles, DIFF_DV, ROW_TILE), BF16),
                   jax.ShapeDtypeStruct((bsz, DIFF_HEADS, tiles, DIFF_DV + VT_PAD, ROW_TILE), BF16)],
        compiler_params=_params("parallel"),
        name="inproj",
    )(x, wf, wt)


def _ret_kernel(q_ref, k_ref, v_ref, gate_ref, dec_ref, gng_ref, gnb_ref, o_ref,
                sf_ref, sb_ref, *, n_chunks):
    c_len = CHUNK
    k_scale = RET_DK ** -0.5
    lane = lax.broadcasted_iota(jnp.int32, (1, LANES), 1)
    lo_half = lane < RET_DK
    m_lo = lo_half.astype(F32)
    m_hi = 1.0 - m_lo
    lg = jax.nn.log_sigmoid(dec_ref[0])
    lgf0, lgf1, lgb0, lgb1 = lg[0:1], lg[1:2], lg[2:3], lg[3:4]
    lgf = jnp.where(lo_half, lgf0, lgf1)
    lgb = jnp.where(lo_half, lgb0, lgb1)

    ri = lax.broadcasted_iota(jnp.int32, (c_len, c_len), 0)
    ci = lax.broadcasted_iota(jnp.int32, (c_len, c_len), 1)
    dist = (ri - ci).astype(F32)
    causal = dist >= 0.0
    block_diag = (ri < RET_DK) == (ci < RET_DK)

    def decay_tile(lf, lb):
        return jnp.where(causal, jnp.exp(lf * jnp.maximum(dist, 0.0)),
                         jnp.exp(lb * jnp.maximum(-dist, 0.0))) * k_scale

    d0 = decay_tile(lgf0, lgb0)
    d1 = decay_tile(lgf1, lgb1)
    row = lax.broadcasted_iota(jnp.int32, (c_len, LANES), 0).astype(F32)
    qdf = jnp.exp(lgf * (row + 1.0))
    qdb = jnp.exp(lgb * (c_len - row))
    kdf = jnp.exp(lgf * (c_len - 1.0 - row)) * k_scale
    kdb = jnp.exp(lgb * row) * k_scale
    gcf = jnp.exp(lgf * c_len)
    gcb = jnp.exp(lgb * c_len)

    def rows_of(c):
        return pl.ds(pl.multiple_of(c * c_len, c_len), c_len)

    def kv_body(c, carry):
        rows = rows_of(c)
        k = k_ref[0, rows, :].astype(F32)
        v = v_ref[0, rows, :]
        kvf = jnp.dot((k * kdf).T.astype(BF16), v, preferred_element_type=F32)
        kvb = jnp.dot((k * kdb).T.astype(BF16), v, preferred_element_type=F32)
        sf_ref[c] = jnp.where(block_diag, kvf, 0.0)
        sb_ref[c] = jnp.where(block_diag, kvb, 0.0)
        return carry

    lax.fori_loop(0, n_chunks, kv_body, 0, unroll=16)

    def fwd_scan(c, s):
        kv = sf_ref[c]
        sf_ref[c] = s
        return s * gcf + kv

    lax.fori_loop(0, n_chunks, fwd_scan, jnp.zeros((c_len, LANES), F32))

    def bwd_scan(i, s):
        c = n_chunks - 1 - i
        kv = sb_ref[c]
        sb_ref[c] = s
        return s * gcb + kv

    lax.fori_loop(0, n_chunks, bwd_scan, jnp.zeros((c_len, LANES), F32))

    gng = gng_ref[...]
    gnb = gnb_ref[...]
    inv_dv = 1.0 / RET_DK

    def out_body(c, carry):
        rows = rows_of(c)
        q = q_ref[0, rows, :].astype(F32)
        k = k_ref[0, rows, :]
        v = v_ref[0, rows, :].astype(F32)
        s0 = lax.dot_general((q * m_lo).astype(BF16), k, NT_DIMS, preferred_element_type=F32)
        s1 = lax.dot_general((q * m_hi).astype(BF16), k, NT_DIMS, preferred_element_type=F32)
        o = jnp.dot((s0 * d0).astype(BF16), (v * m_lo).astype(BF16), preferred_element_type=F32)
        o = o + jnp.dot((s1 * d1).astype(BF16), (v * m_hi).astype(BF16), preferred_element_type=F32)
        o = o + jnp.dot((q * qdf).astype(BF16), sf_ref[c].astype(BF16), preferred_element_type=F32)
        o = o + jnp.dot((q * qdb).astype(BF16), sb_ref[c].astype(BF16), preferred_element_type=F32)
        s_lo = jnp.sum(o * m_lo, axis=-1, keepdims=True)
        s_hi = jnp.sum(o * m_hi, axis=-1, keepdims=True)
        d = o - jnp.where(lo_half, s_lo, s_hi) * inv_dv
        dd = d * d
        v_lo = jnp.sum(dd * m_lo, axis=-1, keepdims=True)
        v_hi = jnp.sum(dd * m_hi, axis=-1, keepdims=True)
        var = jnp.where(lo_half, v_lo, v_hi) * inv_dv
        y = d * lax.rsqrt(var + EPS) * gng + gnb
        o_ref[0, rows, :] = (y * jax.nn.silu(gate_ref[0, rows, :])).astype(BF16)
        return carry

    lax.fori_loop(0, n_chunks, out_body, 0, unroll=16)


def _retention(qkv, gate, dec, gng, gnb):
    bsz, t, _ = qkv.shape
    n_chunks = t // CHUNK
    n_pairs = RET_WIDTH // LANES

    def col(blk):
        return pl.BlockSpec((1, t, LANES), lambda b, p: (b, 0, blk + p))

    return pl.pallas_call(
        functools.partial(_ret_kernel, n_chunks=n_chunks),
        grid=(bsz, n_pairs),
        in_specs=[col(RQ_BLK), col(RK_BLK), col(RV_BLK),
                  pl.BlockSpec((1, t, LANES), lambda b, p: (b, 0, p)),
                  pl.BlockSpec((1, 4, LANES), lambda b, p: (p, 0, 0)),
                  pl.BlockSpec((1, LANES), lambda b, p: (0, p)),
                  pl.BlockSpec((1, LANES), lambda b, p: (0, p))],
        out_specs=pl.BlockSpec((1, t, LANES), lambda b, p: (b, 0, p)),
        out_shape=jax.ShapeDtypeStruct((bsz, t, RET_WIDTH), BF16),
        scratch_shapes=[pltpu.VMEM((n_chunks, CHUNK, LANES), F32),
                        pltpu.VMEM((n_chunks, CHUNK, LANES), F32)],
        compiler_params=_params("parallel", "parallel"),
        name="retention",
    )(qkv, qkv, qkv, gate, dec, gng, gnb)


def _split3(x):
    x1 = x.astype(BF16).astype(F32)
    r = x - x1
    x2 = r.astype(BF16).astype(F32)
    x3 = (r - x2).astype(BF16).astype(F32)
    return x1, x2, x3


def _diff_kernel(slopes_ref, trips_ref, qt_ref, k_ref, vt_ref, kfeat_ref, lam_ref, g_ref, o_ref,
                 dbias_ref, s_even_ref, s_odd_ref, acc_ref, qaug_ref, *, n_kv, n_q, lam0):
    c = slopes_ref[pl.program_id(1)] * LOG2E

    kio = lax.broadcasted_iota(jnp.int32, (KV_TILE, Q_TILE), 0)
    qio = lax.broadcasted_iota(jnp.int32, (KV_TILE, Q_TILE), 1)
    dbias_ref[...] = -(c * jnp.abs(kio - qio).astype(F32))

    lv = lam_ref[...]
    lam = (jnp.exp(jnp.sum(lv[0:1] * lv[1:2], axis=-1, keepdims=True))
           - jnp.exp(jnp.sum(lv[2:3] * lv[3:4], axis=-1, keepdims=True)) + lam0)
    out_gain = g_ref[...] * (1.0 - lam0)

    feat = lax.broadcasted_iota(jnp.int32, (LANES, 1), 0)
    rowi = lax.broadcasted_iota(jnp.int32, (AUG_ROWS, Q_TILE), 0)
    lane_q = lax.broadcasted_iota(jnp.int32, (1, Q_TILE), 1)
    c_row = jnp.full((1, Q_TILE), c, F32)
    c_parts = _split3(c_row)
    bufs = (s_even_ref, s_odd_ref)

    def key_rows(j):
        return pl.ds(pl.multiple_of(j * KV_TILE, KV_TILE), KV_TILE)

    def diag_tile(qs):
        return qs

    def tile_of(qs, pos):
        jd = diag_tile(qs)
        return jnp.where(pos == 0, jd, jnp.where(pos - 1 < jd, pos - 1, pos))

    def prepare_queries(qs, slot):
        qt = qt_ref[0, 0, qs].astype(F32)
        qpos = (qs * Q_TILE + lane_q).astype(F32)
        t_parts = _split3(-(c_row * qpos))
        coef = [float(FEAT_RADIX) * cp for cp in c_parts] + list(c_parts) + list(t_parts)
        aug = jnp.zeros((AUG_ROWS, Q_TILE), F32)
        for r, row in enumerate(coef):
            aug = jnp.where(rowi == r, row, aug)
        tail = jnp.concatenate([aug.astype(BF16),
                                jnp.zeros((LANES - AUG_ROWS, Q_TILE), BF16)], axis=0)
        qaug_ref[slot, 0] = jnp.concatenate([jnp.where(feat < DIFF_DH, qt, 0.0).astype(BF16), tail], axis=0)
        qaug_ref[slot, 1] = jnp.concatenate([jnp.where(feat < DIFF_DH, 0.0, qt).astype(BF16), tail], axis=0)

    def diag_scores(qs, slot):
        k_t = k_ref[0, key_rows(diag_tile(qs)), :]
        return lambda mp: jnp.dot(k_t, qaug_ref[slot, mp, 0:LANES, :],
                                  preferred_element_type=F32) + dbias_ref[...]

    def offdiag_scores(qs, slot, pos):
        j = tile_of(qs, pos)
        side = (j > diag_tile(qs)).astype(jnp.int32)
        lhs = jnp.concatenate([k_ref[0, key_rows(j), :], kfeat_ref[side, key_rows(j), :]], axis=1)
        return lambda mp: jnp.dot(lhs, qaug_ref[slot, mp], preferred_element_type=F32)

    def score_part(scores, mp, dst_ref):
        s = scores(mp)
        dst_ref[mp] = s
        return jnp.max(s, axis=0, keepdims=True)

    def softmax_part(slot, mp, vt_t, src_ref, col_max, m_old):
        m_new = jnp.maximum(m_old, col_max)
        alpha = jnp.exp2(m_old - m_new)
        p = jnp.exp2(src_ref[mp] - m_new).astype(BF16)
        acc_ref[slot, mp] = (alpha * acc_ref[slot, mp]
                             + jnp.dot(vt_t, p, preferred_element_type=F32))
        return m_new

    def stage_pair(scores, dst_ref, qs, slot, pos, src_ref, col_max, m_run):
        vt_t = vt_ref[0, 0, tile_of(qs, pos)]
        new_cm, new_m = [], []
        for mp in range(2):
            new_cm.append(score_part(scores, mp, dst_ref))
            new_m.append(softmax_part(slot, mp, vt_t, src_ref, col_max[mp], m_run[mp]))
        return tuple(new_cm), tuple(new_m)

    def write_output(qs, slot):
        a0 = acc_ref[slot, 0]
        a1 = acc_ref[slot, 1]
        att = (a0[:DIFF_DV] / a0[DIFF_DV:DIFF_DV + 1]
               - lam * (a1[:DIFF_DV] / a1[DIFF_DV:DIFF_DV + 1]))
        ms = jnp.mean(att * att, axis=0, keepdims=True)
        o_ref[0, 0, qs] = (att * lax.rsqrt(ms + EPS) * out_gain).astype(BF16)

    def query_tile_body(qs, cm):
        slot = qs % 2
        prv = jnp.maximum(qs - 1, 0)
        nxt = jnp.minimum(qs + 1, n_q - 1)
        acc_ref[slot] = jnp.zeros(acc_ref.shape[1:], F32)
        m_init = jnp.full((1, Q_TILE), -1e30, F32)

        def pair_body(i, carry):
            m_run, cm_even = carry
            cm_odd, m_run = stage_pair(offdiag_scores(qs, slot, 2 * i + 1), s_odd_ref,
                                       qs, slot, 2 * i, s_even_ref, cm_even, m_run)
            cm_even, m_run = stage_pair(offdiag_scores(qs, slot, 2 * i + 2), s_even_ref,
                                        qs, slot, 2 * i + 1, s_odd_ref, cm_odd, m_run)
            return m_run, cm_even

        m_run, cm_even = lax.fori_loop(0, trips_ref[0], pair_body, ((m_init, m_init), cm))
        write_output(prv, 1 - slot)
        cm_odd, m_run = stage_pair(offdiag_scores(qs, slot, n_kv - 1), s_odd_ref,
                                   qs, slot, n_kv - 2, s_even_ref, cm_even, m_run)
        prepare_queries(nxt, 1 - slot)
        cm_next, _ = stage_pair(diag_scores(nxt, 1 - slot), s_even_ref,
                                qs, slot, n_kv - 1, s_odd_ref, cm_odd, m_run)
        return cm_next

    acc_ref[...] = jnp.ones(acc_ref.shape, F32)
    prepare_queries(0, 0)
    first = diag_scores(0, 0)
    lax.fori_loop(0, n_q, query_tile_body,
                  tuple(score_part(first, mp, s_even_ref) for mp in range(2)))
    write_output(n_q - 1, (n_q - 1) % 2)


def _key_features(t):
    pos = jnp.arange(t, dtype=jnp.int32)
    hi = (pos // FEAT_RADIX).astype(F32)
    lo = (pos % FEAT_RADIX).astype(F32)
    one = jnp.ones((t,), F32)
    cols = jnp.stack([hi, hi, hi, lo, lo, lo, one, one, one], axis=1)
    cols = jnp.pad(cols, ((0, 0), (0, LANES - cols.shape[1])))
    return jnp.stack([cols, -cols]).astype(BF16)


def _diff_attention(qkv, qt, vt, slopes, lam_rows, subln_g, lam0):
    bsz, t, _ = qkv.shape
    n_kv = t // KV_TILE
    assert n_kv % 2 == 0 and KV_TILE == Q_TILE == ROW_TILE and t <= FEAT_RADIX * 256

    def feature_major(tiles, rows):
        return pl.BlockSpec((1, 1, tiles, rows, Q_TILE), lambda b, h, *_: (b, h, 0, 0, 0))

    return pl.pallas_call(
        functools.partial(_diff_kernel, n_kv=n_kv, n_q=t // Q_TILE, lam0=lam0),
        grid_spec=pltpu.PrefetchScalarGridSpec(
            num_scalar_prefetch=2,
            grid=(bsz, DIFF_HEADS),
            in_specs=[feature_major(n_kv, DIFF_DV),
                      pl.BlockSpec((1, t, LANES), lambda b, h, *_: (b, 0, DK_BLK + h)),
                      feature_major(n_kv, DIFF_DV + VT_PAD),
                      pl.BlockSpec((2, t, LANES), lambda b, h, *_: (0, 0, 0),
                                   pipeline_mode=pl.Buffered(1)),
                      pl.BlockSpec((8, LANES), lambda b, h, *_: (0, 0)),
                      pl.BlockSpec((DIFF_DV, 1), lambda b, h, *_: (0, 0))],
            out_specs=feature_major(n_kv, DIFF_DV),
            scratch_shapes=[pltpu.VMEM((KV_TILE, Q_TILE), F32),
                            pltpu.VMEM((2, KV_TILE, Q_TILE), F32),
                            pltpu.VMEM((2, KV_TILE, Q_TILE), F32),
                            pltpu.VMEM((2, 2, DIFF_DV + VT_PAD, Q_TILE), F32),
                            pltpu.VMEM((2, 2, 2 * LANES, Q_TILE), BF16)]),
        out_shape=jax.ShapeDtypeStruct((bsz, DIFF_HEADS, n_kv, DIFF_DV, Q_TILE), BF16),
        compiler_params=_params("parallel", "parallel"),
        name="diff_attention",
    )(slopes, jnp.full((1,), n_kv // 2 - 1, jnp.int32), qt, qkv, vt, _key_features(t),
      lam_rows, subln_g)


def _lambda_init(layer):
    return 0.8 - 0.6 * math.exp(-0.3 * layer)


def _trunk(x, p):
    bsz, t, _ = x.shape
    m = bsz * t
    xf = x.reshape(m, D_MODEL)
    for l in range(DEPTH):
        xf = _ffn_ln(xf, p["ffn1_w13"][l], p["ffn1_w2"][l], p["ln1_g"][l], p["ln1_b"][l])
        qkv, gate, qt, vt = _inproj(xf, p["w_in_fm"][l], p["w_in_tok"][l], bsz, t)
        qkv = qkv.reshape(bsz, t, TOK_WIDTH)
        ret = _retention(qkv, gate.reshape(bsz, t, RET_WIDTH), p["ret_dec"][l],
                         p["ret_gn_g"][l], p["ret_gn_b"][l])
        dif = _diff_attention(qkv, qt, vt, p["slopes"], p["lam_rows"][l],
                              p["diff_subln_g"][l], _lambda_init(l))
        xf = _ffn_ln(xf, p["ffn2_w13"][l], p["ffn2_w2"][l], p["ln3_g"][l], p["ln3_b"][l],
                     mix=(ret.reshape(m, RET_WIDTH), dif,
                          p["w_out"][l], p["ln2_g"][l], p["ln2_b"][l]))
    return xf.reshape(bsz, t, D_MODEL)


def kernel(x_prompt, x_sample, w_in, w_out, ret_decay_f, ret_decay_b, ret_gn_g, ret_gn_b,
           diff_lq1, diff_lk1, diff_lq2, diff_lk2, diff_subln_g,
           ffn1_w13, ffn1_w2, ffn2_w13, ffn2_w2,
           ln1_g, ln1_b, ln2_g, ln2_b, ln3_g, ln3_b):
    n_pairs = RET_WIDTH // LANES

    def pair_rows(dec):
        return jnp.broadcast_to(dec.reshape(DEPTH, n_pairs, 2, 1), (DEPTH, n_pairs, 2, LANES))

    row = lambda a: a.reshape(DEPTH, 1, -1)
    lam_rows = jnp.stack([diff_lq1, diff_lk1, diff_lq2, diff_lk2], axis=1)
    lam_rows = jnp.pad(lam_rows, ((0, 0), (0, 4), (0, LANES - DIFF_DH)))
    p = {
        "w_in_fm": jnp.concatenate([w_in[..., DQ_COL:DK_COL], w_in[..., DV_COL:]], axis=-1).astype(BF16),
        "w_in_tok": jnp.concatenate([w_in[..., :DQ_COL], w_in[..., DK_COL:DV_COL]], axis=-1).astype(BF16),
        "w_out": w_out.astype(BF16),
        "ffn1_w13": ffn1_w13.astype(BF16), "ffn1_w2": ffn1_w2.astype(BF16),
        "ffn2_w13": ffn2_w13.astype(BF16), "ffn2_w2": ffn2_w2.astype(BF16),
        "ret_dec": jnp.concatenate([pair_rows(ret_decay_f), pair_rows(ret_decay_b)], axis=2),
        "ret_gn_g": row(ret_gn_g), "ret_gn_b": row(ret_gn_b),
        "lam_rows": lam_rows, "diff_subln_g": diff_subln_g.reshape(DEPTH, DIFF_DV, 1),
        "slopes": jnp.asarray([2.0 ** (-8.0 * (h + 1) / DIFF_HEADS) for h in range(DIFF_HEADS)], F32),
        "ln1_g": row(ln1_g), "ln1_b": row(ln1_b), "ln2_g": row(ln2_g), "ln2_b": row(ln2_b),
        "ln3_g": row(ln3_g), "ln3_b": row(ln3_b),
    }
    return _trunk(x_prompt, p), _trunk(x_sample, p)
```

```python
import functools
import math

import jax
import jax.numpy as jnp
from jax import lax
from jax.experimental import pallas as pl
from jax.experimental.pallas import tpu as pltpu

D_MODEL = 1024
DEPTH = 2
RET_DK = 64
RET_WIDTH = 512
DIFF_HEADS = 4
DIFF_DH = 64
DIFF_DV = 128
DIFF_WIDTH = 512
D_FF = 2816
CHUNK = 128
EPS = 1e-5
ALPHA = (2 * DEPTH) ** 0.25

LANES = 128
ROW_TILE = 512
FFN_ROW_TILE = 1024
ROW_SUB = 256
Q_TILE = 512
KV_TILE = 512
VT_PAD = 16
AUG_ROWS = 16
FEAT_RADIX = 64
LOG2E = math.log2(math.e)
VMEM_LIMIT = 56 * 1024 * 1024

DQ_COL, DK_COL, DV_COL = 2048, 2560, 3072
RQ_BLK, RK_BLK, RV_BLK, RG_BLK, DK_BLK = 0, 4, 8, 12, 16
TOK_WIDTH = 2560

BF16 = jnp.bfloat16
F32 = jnp.float32
NT_DIMS = (((1,), (1,)), ((), ()))


def _resident(shape):
    nd = len(shape)
    return pl.BlockSpec(shape, lambda *_: (0,) * nd, pipeline_mode=pl.Buffered(1))


def _layer_norm(y, g, b):
    mu = jnp.mean(y, axis=-1, keepdims=True)
    d = y - mu
    var = jnp.mean(d * d, axis=-1, keepdims=True)
    return d * lax.rsqrt(var + EPS) * g + b


def _params(*sem):
    return pltpu.CompilerParams(dimension_semantics=sem, vmem_limit_bytes=VMEM_LIMIT)


def _ffn_residual_ln(x, w13_ref, w2_ref, g_ref, b_ref):
    xb = x.astype(BF16)
    a = jnp.dot(xb, w13_ref[:, :D_FF], preferred_element_type=F32)
    b = jnp.dot(xb, w13_ref[:, D_FF:], preferred_element_type=F32)
    h = (jax.nn.silu(a) * b).astype(BF16)
    ffn = jnp.dot(h, w2_ref[...], preferred_element_type=F32)
    return _layer_norm(ALPHA * x + 0.5 * ffn, g_ref[...], b_ref[...])


def _ffn_ln_kernel(x_ref, w13_ref, w2_ref, g_ref, b_ref, o_ref):
    for r in range(0, FFN_ROW_TILE, ROW_SUB):
        o_ref[r:r + ROW_SUB, :] = _ffn_residual_ln(x_ref[r:r + ROW_SUB, :], w13_ref, w2_ref,
                                                   g_ref, b_ref)


def _mix_ffn_ln_kernel(x_ref, r_ref, dt_ref, wo_ref, gm_ref, bm_ref,
                       w13_ref, w2_ref, g_ref, b_ref, o_ref):
    def mixed(r):
        rows = slice(r, r + ROW_SUB)
        tile, col = divmod(r, Q_TILE)
        d = jnp.concatenate([dt_ref[0, h, tile, :, col:col + ROW_SUB].astype(F32).T.astype(BF16)
                             for h in range(DIFF_HEADS)], axis=1)
        mix = jnp.dot(r_ref[rows, :], wo_ref[0:RET_WIDTH, :], preferred_element_type=F32)
        mix = mix + jnp.dot(d, wo_ref[RET_WIDTH:, :], preferred_element_type=F32)
        return _layer_norm(ALPHA * x_ref[rows, :] + mix, gm_ref[...], bm_ref[...])

    x = mixed(0)
    for r in range(0, FFN_ROW_TILE, ROW_SUB):
        x_next = mixed(r + ROW_SUB) if r + ROW_SUB < FFN_ROW_TILE else None
        o_ref[r:r + ROW_SUB, :] = _ffn_residual_ln(x, w13_ref, w2_ref, g_ref, b_ref)
        x = x_next


def _ffn_ln(x, w13, w2, g, b, mix=None):
    m = x.shape[0]
    row = pl.BlockSpec((FFN_ROW_TILE, D_MODEL), lambda i: (i, 0))
    vec = _resident((1, D_MODEL))
    ffn_specs = [_resident(w13.shape), _resident(w2.shape), vec, vec]
    if mix is None:
        body, args, specs = _ffn_ln_kernel, (x, w13, w2, g, b), [row] + ffn_specs
    else:
        ret, dif, w_out, gm, bm = mix
        half = pl.BlockSpec((FFN_ROW_TILE, RET_WIDTH), lambda i: (i, 0))
        per_step = FFN_ROW_TILE // Q_TILE
        steps = dif.shape[2] // per_step
        dif_spec = pl.BlockSpec((1, DIFF_HEADS, per_step, DIFF_DV, Q_TILE),
                                lambda i: (i // steps, 0, i % steps, 0, 0))
        body = _mix_ffn_ln_kernel
        args = (x, ret, dif, w_out, gm, bm, w13, w2, g, b)
        specs = [row, half, dif_spec, _resident(w_out.shape), vec, vec] + ffn_specs
    return pl.pallas_call(
        body,
        grid=(m // FFN_ROW_TILE,),
        in_specs=specs,
        out_specs=row,
        out_shape=jax.ShapeDtypeStruct((m, D_MODEL), F32),
        compiler_params=_params("parallel"),
        name="ffn_ln" if mix is None else "mix_ffn_ln",
    )(*args)


def _inproj_kernel(x_ref, wf_ref, wt_ref, o_ref, gate_ref, qt_ref, vt_ref):
    ones_row = (lax.broadcasted_iota(jnp.int32, (VT_PAD, ROW_SUB), 0) == 0).astype(F32)
    for r in range(0, ROW_TILE, ROW_SUB):
        rows = slice(r, r + ROW_SUB)
        xb = x_ref[rows, :].astype(BF16)
        yf = jnp.dot(xb, wf_ref[...], preferred_element_type=F32)
        yt = jnp.dot(xb, wt_ref[...], preferred_element_type=F32)
        for h in range(DIFF_HEADS):
            q = yf[:, h * LANES:(h + 1) * LANES] * (LOG2E * DIFF_DH ** -0.5)
            qt_ref[0, h, 0, :, rows] = q.T.astype(BF16)
            v = yf[:, DIFF_WIDTH + h * LANES:DIFF_WIDTH + (h + 1) * LANES]
            vt_ref[0, h, 0, :, rows] = jnp.concatenate([v.T, ones_row], axis=0).astype(BF16)
        o_ref[rows, :] = yt.astype(BF16)
        gate_ref[rows, :] = yt[:, RG_BLK * LANES:RG_BLK * LANES + RET_WIDTH]


def _inproj(x, wf, wt, bsz, t):
    m = x.shape[0]
    tiles = t // ROW_TILE

    def feature_major(rows):
        return pl.BlockSpec((1, DIFF_HEADS, 1, rows, ROW_TILE),
                            lambda i: (i // tiles, 0, i % tiles, 0, 0))

    return pl.pallas_call(
        _inproj_kernel,
        grid=(m // ROW_TILE,),
        in_specs=[pl.BlockSpec((ROW_TILE, D_MODEL), lambda i: (i, 0)),
                  _resident(wf.shape), _resident(wt.shape)],
        out_specs=[pl.BlockSpec((ROW_TILE, TOK_WIDTH), lambda i: (i, 0)),
                   pl.BlockSpec((ROW_TILE, RET_WIDTH), lambda i: (i, 0)),
                   feature_major(DIFF_DV), feature_major(DIFF_DV + VT_PAD)],
        out_shape=[jax.ShapeDtypeStruct((m, TOK_WIDTH), BF16),
                   jax.ShapeDtypeStruct((m, RET_WIDTH), F32),
                   jax.ShapeDtypeStruct((bsz, DIFF_HEADS, tiles, DIFF_DV, ROW_TILE), BF16),
                   jax.ShapeDtypeStruct((bsz, DIFF_HEADS, tiles, DIFF_DV + VT_PAD, ROW_TILE), BF16)],
        compiler_params=_params("parallel"),
        name="inproj",
    )(x, wf, wt)


def _ret_kernel(q_ref, k_ref, v_ref, gate_ref, dec_ref, gng_ref, gnb_ref, o_ref,
                sf_ref, sb_ref, *, n_chunks):
    c_len = CHUNK
    k_scale = RET_DK ** -0.5
    lane = lax.broadcasted_iota(jnp.int32, (1, LANES), 1)
    lo_half = lane < RET_DK
    m_lo = lo_half.astype(F32)
    m_hi = 1.0 - m_lo
    lg = jax.nn.log_sigmoid(dec_ref[0])
    lgf0, lgf1, lgb0, lgb1 = lg[0:1], lg[1:2], lg[2:3], lg[3:4]
    lgf = jnp.where(lo_half, lgf0, lgf1)
    lgb = jnp.where(lo_half, lgb0, lgb1)

    ri = lax.broadcasted_iota(jnp.int32, (c_len, c_len), 0)
    ci = lax.broadcasted_iota(jnp.int32, (c_len, c_len), 1)
    dist = (ri - ci).astype(F32)
    causal = dist >= 0.0
    block_diag = (ri < RET_DK) == (ci < RET_DK)

    def decay_tile(lf, lb):
        return jnp.where(causal, jnp.exp(lf * jnp.maximum(dist, 0.0)),
                         jnp.exp(lb * jnp.maximum(-dist, 0.0))) * k_scale

    d0 = decay_tile(lgf0, lgb0)
    d1 = decay_tile(lgf1, lgb1)
    row = lax.broadcasted_iota(jnp.int32, (c_len, LANES), 0).astype(F32)
    qdf = jnp.exp(lgf * (row + 1.0))
    qdb = jnp.exp(lgb * (c_len - row))
    kdf = jnp.exp(lgf * (c_len - 1.0 - row)) * k_scale
    kdb = jnp.exp(lgb * row) * k_scale
    gcf = jnp.exp(lgf * c_len)
    gcb = jnp.exp(lgb * c_len)

    def rows_of(c):
        return pl.ds(pl.multiple_of(c * c_len, c_len), c_len)

    def kv_body(c, carry):
        rows = rows_of(c)
        k = k_ref[0, rows, :].astype(F32)
        v = v_ref[0, rows, :]
        kvf = jnp.dot((k * kdf).T.astype(BF16), v, preferred_element_type=F32)
        kvb = jnp.dot((k * kdb).T.astype(BF16), v, preferred_element_type=F32)
        sf_ref[c] = jnp.where(block_diag, kvf, 0.0)
        sb_ref[c] = jnp.where(block_diag, kvb, 0.0)
        return carry

    lax.fori_loop(0, n_chunks, kv_body, 0, unroll=16)

    def fwd_scan(c, s):
        kv = sf_ref[c]
        sf_ref[c] = s
        return s * gcf + kv

    lax.fori_loop(0, n_chunks, fwd_scan, jnp.zeros((c_len, LANES), F32))

    def bwd_scan(i, s):
        c = n_chunks - 1 - i
        kv = sb_ref[c]
        sb_ref[c] = s
        return s * gcb + kv

    lax.fori_loop(0, n_chunks, bwd_scan, jnp.zeros((c_len, LANES), F32))

    gng = gng_ref[...]
    gnb = gnb_ref[...]
    inv_dv = 1.0 / RET_DK

    def out_body(c, carry):
        rows = rows_of(c)
        q = q_ref[0, rows, :].astype(F32)
        k = k_ref[0, rows, :]
        v = v_ref[0, rows, :].astype(F32)
        s0 = lax.dot_general((q * m_lo).astype(BF16), k, NT_DIMS, preferred_element_type=F32)
        s1 = lax.dot_general((q * m_hi).astype(BF16), k, NT_DIMS, preferred_element_type=F32)
        o = jnp.dot((s0 * d0).astype(BF16), (v * m_lo).astype(BF16), preferred_element_type=F32)
        o = o + jnp.dot((s1 * d1).astype(BF16), (v * m_hi).astype(BF16), preferred_element_type=F32)
        o = o + jnp.dot((q * qdf).astype(BF16), sf_ref[c].astype(BF16), preferred_element_type=F32)
        o = o + jnp.dot((q * qdb).astype(BF16), sb_ref[c].astype(BF16), preferred_element_type=F32)
        s_lo = jnp.sum(o * m_lo, axis=-1, keepdims=True)
        s_hi = jnp.sum(o * m_hi, axis=-1, keepdims=True)
        d = o - jnp.where(lo_half, s_lo, s_hi) * inv_dv
        dd = d * d
        v_lo = jnp.sum(dd * m_lo, axis=-1, keepdims=True)
        v_hi = jnp.sum(dd * m_hi, axis=-1, keepdims=True)
        var = jnp.where(lo_half, v_lo, v_hi) * inv_dv
        y = d * lax.rsqrt(var + EPS) * gng + gnb
        o_ref[0, rows, :] = (y * jax.nn.silu(gate_ref[0, rows, :])).astype(BF16)
        return carry

    lax.fori_loop(0, n_chunks, out_body, 0, unroll=16)


def _retention(qkv, gate, dec, gng, gnb):
    bsz, t, _ = qkv.shape
    n_chunks = t // CHUNK
    n_pairs = RET_WIDTH // LANES

    def col(blk):
        return pl.BlockSpec((1, t, LANES), lambda b, p: (b, 0, blk + p))

    return pl.pallas_call(
        functools.partial(_ret_kernel, n_chunks=n_chunks),
        grid=(bsz, n_pairs),
        in_specs=[col(RQ_BLK), col(RK_BLK), col(RV_BLK),
                  pl.BlockSpec((1, t, LANES), lambda b, p: (b, 0, p)),
                  pl.BlockSpec((1, 4, LANES), lambda b, p: (p, 0, 0)),
                  pl.BlockSpec((1, LANES), lambda b, p: (0, p)),
                  pl.BlockSpec((1, LANES), lambda b, p: (0, p))],
        out_specs=pl.BlockSpec((1, t, LANES), lambda b, p: (b, 0, p)),
        out_shape=jax.ShapeDtypeStruct((bsz, t, RET_WIDTH), BF16),
        scratch_shapes=[pltpu.VMEM((n_chunks, CHUNK, LANES), F32),
                        pltpu.VMEM((n_chunks, CHUNK, LANES), F32)],
        compiler_params=_params("parallel", "parallel"),
        name="retention",
    )(qkv, qkv, qkv, gate, dec, gng, gnb)


def _split3(x):
    x1 = x.astype(BF16).astype(F32)
    r = x - x1
    x2 = r.astype(BF16).astype(F32)
    x3 = (r - x2).astype(BF16).astype(F32)
    return x1, x2, x3


def _diff_kernel(slopes_ref, trips_ref, qt_ref, k_ref, vt_ref, kfeat_ref, lam_ref, g_ref, o_ref,
                 dbias_ref, s_even_ref, s_odd_ref, acc_ref, qaug_ref, *, n_kv, n_q, lam0):
    c = slopes_ref[pl.program_id(1)] * LOG2E

    kio = lax.broadcasted_iota(jnp.int32, (KV_TILE, Q_TILE), 0)
    qio = lax.broadcasted_iota(jnp.int32, (KV_TILE, Q_TILE), 1)
    dbias_ref[...] = -(c * jnp.abs(kio - qio).astype(F32))

    lv = lam_ref[...]
    lam = (jnp.exp(jnp.sum(lv[0:1] * lv[1:2], axis=-1, keepdims=True))
           - jnp.exp(jnp.sum(lv[2:3] * lv[3:4], axis=-1, keepdims=True)) + lam0)
    out_gain = g_ref[...] * (1.0 - lam0)

    feat = lax.broadcasted_iota(jnp.int32, (LANES, 1), 0)
    rowi = lax.broadcasted_iota(jnp.int32, (AUG_ROWS, Q_TILE), 0)
    lane_q = lax.broadcasted_iota(jnp.int32, (1, Q_TILE), 1)
    c_row = jnp.full((1, Q_TILE), c, F32)
    c_parts = _split3(c_row)

    def key_rows(j):
        return pl.ds(pl.multiple_of(j * KV_TILE, KV_TILE), KV_TILE)

    def diag_tile(qs):
        return qs

    def tile_of(qs, pos):
        jd = diag_tile(qs)
        return jnp.where(pos == 0, jd, jnp.where(pos - 1 < jd, pos - 1, pos))

    def prepare_queries(qs, slot):
        qt = qt_ref[0, 0, qs].astype(F32)
        qpos = (qs * Q_TILE + lane_q).astype(F32)
        t_parts = _split3(-(c_row * qpos))
        coef = [float(FEAT_RADIX) * cp for cp in c_parts] + list(c_parts) + list(t_parts)
        aug = jnp.zeros((AUG_ROWS, Q_TILE), F32)
        for r, row in enumerate(coef):
            aug = jnp.where(rowi == r, row, aug)
        tail = jnp.concatenate([aug.astype(BF16),
                                jnp.zeros((LANES - AUG_ROWS, Q_TILE), BF16)], axis=0)
        qaug_ref[slot, 0] = jnp.concatenate([jnp.where(feat < DIFF_DH, qt, 0.0).astype(BF16), tail], axis=0)
        qaug_ref[slot, 1] = jnp.concatenate([jnp.where(feat < DIFF_DH, 0.0, qt).astype(BF16), tail], axis=0)

    def diag_scores(qs, slot):
        k_t = k_ref[0, key_rows(diag_tile(qs)), :]
        return lambda mp: jnp.dot(k_t, qaug_ref[slot, mp, 0:LANES, :],
                                  preferred_element_type=F32) + dbias_ref[...]

    def offdiag_scores(qs, slot, pos):
        j = tile_of(qs, pos)
        side = (j > diag_tile(qs)).astype(jnp.int32)
        lhs = jnp.concatenate([k_ref[0, key_rows(j), :], kfeat_ref[side, key_rows(j), :]], axis=1)
        return lambda mp: jnp.dot(lhs, qaug_ref[slot, mp], preferred_element_type=F32)

    def score_part(scores, mp, dst_ref):
        s = scores(mp)
        dst_ref[mp] = s
        return jnp.max(s, axis=0, keepdims=True)

    def softmax_part(slot, mp, vt_t, src_ref, col_max, m_old):
        m_new = jnp.maximum(m_old, col_max)
        alpha = jnp.exp2(m_old - m_new)
        p = jnp.exp2(src_ref[mp] - m_new).astype(BF16)
        acc_ref[slot, mp] = (alpha * acc_ref[slot, mp]
                             + jnp.dot(vt_t, p, preferred_element_type=F32))
        return m_new

    def stage_pair(scores, dst_ref, qs, slot, pos, src_ref, col_max, m_run):
        vt_t = vt_ref[0, 0, tile_of(qs, pos)]
        new_cm, new_m = [], []
        for mp in range(2):
            new_cm.append(score_part(scores, mp, dst_ref))
            new_m.append(softmax_part(slot, mp, vt_t, src_ref, col_max[mp], m_run[mp]))
        return tuple(new_cm), tuple(new_m)

    def write_output(qs, slot):
        a0 = acc_ref[slot, 0]
        a1 = acc_ref[slot, 1]
        att = (a0[:DIFF_DV] / a0[DIFF_DV:DIFF_DV + 1]
               - lam * (a1[:DIFF_DV] / a1[DIFF_DV:DIFF_DV + 1]))
        ms = jnp.mean(att * att, axis=0, keepdims=True)
        o_ref[0, 0, qs] = (att * lax.rsqrt(ms + EPS) * out_gain).astype(BF16)

    def query_tile_body(qs, cm):
        slot = qs % 2
        prv = jnp.maximum(qs - 1, 0)
        nxt = jnp.minimum(qs + 1, n_q - 1)
        acc_ref[slot] = jnp.zeros(acc_ref.shape[1:], F32)
        m_init = jnp.full((1, Q_TILE), -1e30, F32)

        def pair_body(i, carry):
            m_run, cm_even = carry
            cm_odd, m_run = stage_pair(offdiag_scores(qs, slot, 2 * i + 1), s_odd_ref,
                                       qs, slot, 2 * i, s_even_ref, cm_even, m_run)
            cm_even, m_run = stage_pair(offdiag_scores(qs, slot, 2 * i + 2), s_even_ref,
                                        qs, slot, 2 * i + 1, s_odd_ref, cm_odd, m_run)
            return m_run, cm_even

        m_run, cm_even = lax.fori_loop(0, trips_ref[0], pair_body, ((m_init, m_init), cm))
        write_output(prv, 1 - slot)
        cm_odd, m_run = stage_pair(offdiag_scores(qs, slot, n_kv - 1), s_odd_ref,
                                   qs, slot, n_kv - 2, s_even_ref, cm_even, m_run)
        prepare_queries(nxt, 1 - slot)
        cm_next, _ = stage_pair(diag_scores(nxt, 1 - slot), s_even_ref,
                                qs, slot, n_kv - 1, s_odd_ref, cm_odd, m_run)
        return cm_next

    acc_ref[...] = jnp.ones(acc_ref.shape, F32)
    prepare_queries(0, 0)
    first = diag_scores(0, 0)
    lax.fori_loop(0, n_q, query_tile_body,
                  tuple(score_part(first, mp, s_even_ref) for mp in range(2)))
    write_output(n_q - 1, (n_q - 1) % 2)


def _key_features(t):
    pos = jnp.arange(t, dtype=jnp.int32)
    hi = (pos // FEAT_RADIX).astype(F32)
    lo = (pos % FEAT_RADIX).astype(F32)
    one = jnp.ones((t,), F32)
    cols = jnp.stack([hi, hi, hi, lo, lo, lo, one, one, one], axis=1)
    cols = jnp.pad(cols, ((0, 0), (0, LANES - cols.shape[1])))
    return jnp.stack([cols, -cols]).astype(BF16)


def _diff_attention(qkv, qt, vt, slopes, lam_rows, subln_g, lam0):
    bsz, t, _ = qkv.shape
    n_kv = t // KV_TILE
    assert n_kv % 2 == 0 and KV_TILE == Q_TILE == ROW_TILE and t <= FEAT_RADIX * 256

    def feature_major(tiles, rows):
        return pl.BlockSpec((1, 1, tiles, rows, Q_TILE), lambda b, h, *_: (b, h, 0, 0, 0))

    return pl.pallas_call(
        functools.partial(_diff_kernel, n_kv=n_kv, n_q=t // Q_TILE, lam0=lam0),
        grid_spec=pltpu.PrefetchScalarGridSpec(
            num_scalar_prefetch=2,
            grid=(bsz, DIFF_HEADS),
            in_specs=[feature_major(n_kv, DIFF_DV),
                      pl.BlockSpec((1, t, LANES), lambda b, h, *_: (b, 0, DK_BLK + h)),
                      feature_major(n_kv, DIFF_DV + VT_PAD),
                      pl.BlockSpec((2, t, LANES), lambda b, h, *_: (0, 0, 0),
                                   pipeline_mode=pl.Buffered(1)),
                      pl.BlockSpec((8, LANES), lambda b, h, *_: (0, 0)),
                      pl.BlockSpec((DIFF_DV, 1), lambda b, h, *_: (0, 0))],
            out_specs=feature_major(n_kv, DIFF_DV),
            scratch_shapes=[pltpu.VMEM((KV_TILE, Q_TILE), F32),
                            pltpu.VMEM((2, KV_TILE, Q_TILE), F32),
                            pltpu.VMEM((2, KV_TILE, Q_TILE), F32),
                            pltpu.VMEM((2, 2, DIFF_DV + VT_PAD, Q_TILE), F32),
                            pltpu.VMEM((2, 2, 2 * LANES, Q_TILE), BF16)]),
        out_shape=jax.ShapeDtypeStruct((bsz, DIFF_HEADS, n_kv, DIFF_DV, Q_TILE), BF16),
        compiler_params=_params("parallel", "parallel"),
        name="diff_attention",
    )(slopes, jnp.full((1,), n_kv // 2 - 1, jnp.int32), qt, qkv, vt, _key_features(t),
      lam_rows, subln_g)


def _lambda_init(layer):
    return 0.8 - 0.6 * math.exp(-0.3 * layer)


def _trunk(x, p):
    bsz, t, _ = x.shape
    m = bsz * t
    xf = x.reshape(m, D_MODEL)
    for l in range(DEPTH):
        xf = _ffn_ln(xf, p["ffn1_w13"][l], p["ffn1_w2"][l], p["ln1_g"][l], p["ln1_b"][l])
        qkv, gate, qt, vt = _inproj(xf, p["w_in_fm"][l], p["w_in_tok"][l], bsz, t)
        qkv = qkv.reshape(bsz, t, TOK_WIDTH)
        ret = _retention(qkv, gate.reshape(bsz, t, RET_WIDTH), p["ret_dec"][l],
                         p["ret_gn_g"][l], p["ret_gn_b"][l])
        dif = _diff_attention(qkv, qt, vt, p["slopes"], p["lam_rows"][l],
                              p["diff_subln_g"][l], _lambda_init(l))
        xf = _ffn_ln(xf, p["ffn2_w13"][l], p["ffn2_w2"][l], p["ln3_g"][l], p["ln3_b"][l],
                     mix=(ret.reshape(m, RET_WIDTH), dif,
                          p["w_out"][l], p["ln2_g"][l], p["ln2_b"][l]))
    return xf.reshape(bsz, t, D_MODEL)


def kernel(x_prompt, x_sample, w_in, w_out, ret_decay_f, ret_decay_b, ret_gn_g, ret_gn_b,
           diff_lq1, diff_lk1, diff_lq2, diff_lk2, diff_subln_g,
           ffn1_w13, ffn1_w2, ffn2_w13, ffn2_w2,
           ln1_g, ln1_b, ln2_g, ln2_b, ln3_g, ln3_b):
    n_pairs = RET_WIDTH // LANES

    def pair_rows(dec):
        return jnp.broadcast_to(dec.reshape(DEPTH, n_pairs, 2, 1), (DEPTH, n_pairs, 2, LANES))

    row = lambda a: a.reshape(DEPTH, 1, -1)
    lam_rows = jnp.stack([diff_lq1, diff_lk1, diff_lq2, diff_lk2], axis=1)
    lam_rows = jnp.pad(lam_rows, ((0, 0), (0, 4), (0, LANES - DIFF_DH)))
    p = {
        "w_in_fm": jnp.concatenate([w_in[..., DQ_COL:DK_COL], w_in[..., DV_COL:]], axis=-1).astype(BF16),
        "w_in_tok": jnp.concatenate([w_in[..., :DQ_COL], w_in[..., DK_COL:DV_COL]], axis=-1).astype(BF16),
        "w_out": w_out.astype(BF16),
        "ffn1_w13": ffn1_w13.astype(BF16), "ffn1_w2": ffn1_w2.astype(BF16),
        "ffn2_w13": ffn2_w13.astype(BF16), "ffn2_w2": ffn2_w2.astype(BF16),
        "ret_dec": jnp.concatenate([pair_rows(ret_decay_f), pair_rows(ret_decay_b)], axis=2),
        "ret_gn_g": row(ret_gn_g), "ret_gn_b": row(ret_gn_b),
        "lam_rows": lam_rows, "diff_subln_g": diff_subln_g.reshape(DEPTH, DIFF_DV, 1),
        "slopes": jnp.asarray([2.0 ** (-8.0 * (h + 1) / DIFF_HEADS) for h in range(DIFF_HEADS)], F32),
        "ln1_g": row(ln1_g), "ln1_b": row(ln1_b), "ln2_g": row(ln2_g), "ln2_b": row(ln2_b),
        "ln3_g": row(ln3_g), "ln3_b": row(ln3_b),
    }
    return _trunk(x_prompt, p), _trunk(x_sample, p)
```

```python
import functools
import math

import jax
import jax.numpy as jnp
from jax import lax
from jax.experimental import pallas as pl
from jax.experimental.pallas import tpu as pltpu

D_MODEL = 1024
DEPTH = 2
RET_DK = 64
RET_WIDTH = 512
DIFF_HEADS = 4
DIFF_DH = 64
DIFF_DV = 128
DIFF_WIDTH = 512
D_FF = 2816
CHUNK = 128
EPS = 1e-5
ALPHA = (2 * DEPTH) ** 0.25

LANES = 128
ROW_TILE = 1024
FFN_ROW_TILE = 1024
ROW_SUB = 256
Q_TILE = 512
KV_TILE = 512
VT_PAD = 16
AUG_ROWS = 16
FEAT_RADIX = 64
LOG2E = math.log2(math.e)
VMEM_LIMIT = 56 * 1024 * 1024

DQ_COL, DK_COL, DV_COL = 2048, 2560, 3072
RQ_BLK, RK_BLK, RV_BLK, RG_BLK, DK_BLK = 0, 4, 8, 12, 16
TOK_WIDTH = 2560

BF16 = jnp.bfloat16
F32 = jnp.float32
NT_DIMS = (((1,), (1,)), ((), ()))


def _resident(shape):
    nd = len(shape)
    return pl.BlockSpec(shape, lambda *_: (0,) * nd, pipeline_mode=pl.Buffered(1))


def _layer_norm(y, g, b):
    mu = jnp.mean(y, axis=-1, keepdims=True)
    d = y - mu
    var = jnp.mean(d * d, axis=-1, keepdims=True)
    return d * lax.rsqrt(var + EPS) * g + b


def _params(*sem):
    return pltpu.CompilerParams(dimension_semantics=sem, vmem_limit_bytes=VMEM_LIMIT)


def _ffn_residual_ln(x, w13_ref, w2_ref, g_ref, b_ref):
    xb = x.astype(BF16)
    a = jnp.dot(xb, w13_ref[:, :D_FF], preferred_element_type=F32)
    b = jnp.dot(xb, w13_ref[:, D_FF:], preferred_element_type=F32)
    h = (jax.nn.silu(a) * b).astype(BF16)
    ffn = jnp.dot(h, w2_ref[...], preferred_element_type=F32)
    return _layer_norm(ALPHA * x + 0.5 * ffn, g_ref[...], b_ref[...])


def _ffn_ln_kernel(x_ref, w13_ref, w2_ref, g_ref, b_ref, o_ref):
    for r in range(0, FFN_ROW_TILE, ROW_SUB):
        o_ref[r:r + ROW_SUB, :] = _ffn_residual_ln(x_ref[r:r + ROW_SUB, :], w13_ref, w2_ref,
                                                   g_ref, b_ref)


def _mix_ffn_ln_kernel(x_ref, r_ref, dt_ref, wo_ref, gm_ref, bm_ref,
                       w13_ref, w2_ref, g_ref, b_ref, o_ref):
    def mixed(r):
        rows = slice(r, r + ROW_SUB)
        tile, col = divmod(r, Q_TILE)
        d = jnp.concatenate([dt_ref[0, h, tile, :, col:col + ROW_SUB].astype(F32).T.astype(BF16)
                             for h in range(DIFF_HEADS)], axis=1)
        mix = jnp.dot(r_ref[rows, :], wo_ref[0:RET_WIDTH, :], preferred_element_type=F32)
        mix = mix + jnp.dot(d, wo_ref[RET_WIDTH:, :], preferred_element_type=F32)
        return _layer_norm(ALPHA * x_ref[rows, :] + mix, gm_ref[...], bm_ref[...])

    x = mixed(0)
    for r in range(0, FFN_ROW_TILE, ROW_SUB):
        x_next = mixed(r + ROW_SUB) if r + ROW_SUB < FFN_ROW_TILE else None
        o_ref[r:r + ROW_SUB, :] = _ffn_residual_ln(x, w13_ref, w2_ref, g_ref, b_ref)
        x = x_next


def _ffn_ln(x, w13, w2, g, b, mix=None):
    m = x.shape[0]
    row = pl.BlockSpec((FFN_ROW_TILE, D_MODEL), lambda i: (i, 0))
    vec = _resident((1, D_MODEL))
    ffn_specs = [_resident(w13.shape), _resident(w2.shape), vec, vec]
    if mix is None:
        body, args, specs = _ffn_ln_kernel, (x, w13, w2, g, b), [row] + ffn_specs
    else:
        ret, dif, w_out, gm, bm = mix
        half = pl.BlockSpec((FFN_ROW_TILE, RET_WIDTH), lambda i: (i, 0))
        per_step = FFN_ROW_TILE // Q_TILE
        steps = dif.shape[2] // per_step
        dif_spec = pl.BlockSpec((1, DIFF_HEADS, per_step, DIFF_DV, Q_TILE),
                                lambda i: (i // steps, 0, i % steps, 0, 0))
        body = _mix_ffn_ln_kernel
        args = (x, ret, dif, w_out, gm, bm, w13, w2, g, b)
        specs = [row, half, dif_spec, _resident(w_out.shape), vec, vec] + ffn_specs
    return pl.pallas_call(
        body,
        grid=(m // FFN_ROW_TILE,),
        in_specs=specs,
        out_specs=row,
        out_shape=jax.ShapeDtypeStruct((m, D_MODEL), F32),
        compiler_params=_params("parallel"),
        name="ffn_ln" if mix is None else "mix_ffn_ln",
    )(*args)


def _inproj_kernel(x_ref, wf_ref, wt_ref, o_ref, gate_ref, qt_ref, vt_ref):
    ones_row = (lax.broadcasted_iota(jnp.int32, (VT_PAD, ROW_SUB), 0) == 0).astype(F32)
    for r in range(0, ROW_TILE, ROW_SUB):
        rows = slice(r, r + ROW_SUB)
        tile, col = divmod(r, Q_TILE)
        cols = slice(col, col + ROW_SUB)
        xb = x_ref[rows, :].astype(BF16)
        yf = jnp.dot(xb, wf_ref[...], preferred_element_type=F32)
        yt = jnp.dot(xb, wt_ref[...], preferred_element_type=F32)
        for h in range(DIFF_HEADS):
            q = yf[:, h * LANES:(h + 1) * LANES] * (LOG2E * DIFF_DH ** -0.5)
            qt_ref[0, h, tile, :, cols] = q.T.astype(BF16)
            v = yf[:, DIFF_WIDTH + h * LANES:DIFF_WIDTH + (h + 1) * LANES]
            vt_ref[0, h, tile, :, cols] = jnp.concatenate([v.T, ones_row], axis=0).astype(BF16)
        o_ref[rows, :] = yt.astype(BF16)
        gate_ref[rows, :] = yt[:, RG_BLK * LANES:RG_BLK * LANES + RET_WIDTH]


def _inproj(x, wf, wt, bsz, t):
    m = x.shape[0]
    per_step = ROW_TILE // Q_TILE
    steps = t // ROW_TILE

    def feature_major(rows):
        return pl.BlockSpec((1, DIFF_HEADS, per_step, rows, Q_TILE),
                            lambda i: (i // steps, 0, i % steps, 0, 0))

    return pl.pallas_call(
        _inproj_kernel,
        grid=(m // ROW_TILE,),
        in_specs=[pl.BlockSpec((ROW_TILE, D_MODEL), lambda i: (i, 0)),
                  _resident(wf.shape), _resident(wt.shape)],
        out_specs=[pl.BlockSpec((ROW_TILE, TOK_WIDTH), lambda i: (i, 0)),
                   pl.BlockSpec((ROW_TILE, RET_WIDTH), lambda i: (i, 0)),
                   feature_major(DIFF_DV), feature_major(DIFF_DV + VT_PAD)],
        out_shape=[jax.ShapeDtypeStruct((m, TOK_WIDTH), BF16),
                   jax.ShapeDtypeStruct((m, RET_WIDTH), F32),
                   jax.ShapeDtypeStruct((bsz, DIFF_HEADS, t // Q_TILE, DIFF_DV, Q_TILE), BF16),
                   jax.ShapeDtypeStruct((bsz, DIFF_HEADS, t // Q_TILE, DIFF_DV + VT_PAD, Q_TILE), BF16)],
        compiler_params=_params("parallel"),
        name="inproj",
    )(x, wf, wt)


def _ret_kernel(q_ref, k_ref, v_ref, gate_ref, dec_ref, gng_ref, gnb_ref, o_ref,
                sf_ref, sb_ref, *, n_chunks):
    c_len = CHUNK
    k_scale = RET_DK ** -0.5
    lane = lax.broadcasted_iota(jnp.int32, (1, LANES), 1)
    lo_half = lane < RET_DK
    m_lo = lo_half.astype(F32)
    m_hi = 1.0 - m_lo
    lg = jax.nn.log_sigmoid(dec_ref[0])
    lgf0, lgf1, lgb0, lgb1 = lg[0:1], lg[1:2], lg[2:3], lg[3:4]
    lgf = jnp.where(lo_half, lgf0, lgf1)
    lgb = jnp.where(lo_half, lgb0, lgb1)

    ri = lax.broadcasted_iota(jnp.int32, (c_len, c_len), 0)
    ci = lax.broadcasted_iota(jnp.int32, (c_len, c_len), 1)
    dist = (ri - ci).astype(F32)
    causal = dist >= 0.0
    block_diag = (ri < RET_DK) == (ci < RET_DK)

    def decay_tile(lf, lb):
        return jnp.where(causal, jnp.exp(lf * jnp.maximum(dist, 0.0)),
                         jnp.exp(lb * jnp.maximum(-dist, 0.0))) * k_scale

    d0 = decay_tile(lgf0, lgb0)
    d1 = decay_tile(lgf1, lgb1)
    row = lax.broadcasted_iota(jnp.int32, (c_len, LANES), 0).astype(F32)
    qdf = jnp.exp(lgf * (row + 1.0))
    qdb = jnp.exp(lgb * (c_len - row))
    kdf = jnp.exp(lgf * (c_len - 1.0 - row)) * k_scale
    kdb = jnp.exp(lgb * row) * k_scale
    gcf = jnp.exp(lgf * c_len)
    gcb = jnp.exp(lgb * c_len)

    def rows_of(c):
        return pl.ds(pl.multiple_of(c * c_len, c_len), c_len)

    def kv_body(c, carry):
        rows = rows_of(c)
        k = k_ref[0, rows, :].astype(F32)
        v = v_ref[0, rows, :]
        kvf = jnp.dot((k * kdf).T.astype(BF16), v, preferred_element_type=F32)
        kvb = jnp.dot((k * kdb).T.astype(BF16), v, preferred_element_type=F32)
        sf_ref[c] = jnp.where(block_diag, kvf, 0.0)
        sb_ref[c] = jnp.where(block_diag, kvb, 0.0)
        return carry

    lax.fori_loop(0, n_chunks, kv_body, 0, unroll=16)

    def fwd_scan(c, s):
        kv = sf_ref[c]
        sf_ref[c] = s
        return s * gcf + kv

    lax.fori_loop(0, n_chunks, fwd_scan, jnp.zeros((c_len, LANES), F32))

    def bwd_scan(i, s):
        c = n_chunks - 1 - i
        kv = sb_ref[c]
        sb_ref[c] = s
        return s * gcb + kv

    lax.fori_loop(0, n_chunks, bwd_scan, jnp.zeros((c_len, LANES), F32))

    gng = gng_ref[...]
    gnb = gnb_ref[...]
    inv_dv = 1.0 / RET_DK

    def out_body(c, carry):
        rows = rows_of(c)
        q = q_ref[0, rows, :].astype(F32)
        k = k_ref[0, rows, :]
        v = v_ref[0, rows, :].astype(F32)
        s0 = lax.dot_general((q * m_lo).astype(BF16), k, NT_DIMS, preferred_element_type=F32)
        s1 = lax.dot_general((q * m_hi).astype(BF16), k, NT_DIMS, preferred_element_type=F32)
        o = jnp.dot((s0 * d0).astype(BF16), (v * m_lo).astype(BF16), preferred_element_type=F32)
        o = o + jnp.dot((s1 * d1).astype(BF16), (v * m_hi).astype(BF16), preferred_element_type=F32)
        o = o + jnp.dot((q * qdf).astype(BF16), sf_ref[c].astype(BF16), preferred_element_type=F32)
        o = o + jnp.dot((q * qdb).astype(BF16), sb_ref[c].astype(BF16), preferred_element_type=F32)
        s_lo = jnp.sum(o * m_lo, axis=-1, keepdims=True)
        s_hi = jnp.sum(o * m_hi, axis=-1, keepdims=True)
        d = o - jnp.where(lo_half, s_lo, s_hi) * inv_dv
        dd = d * d
        v_lo = jnp.sum(dd * m_lo, axis=-1, keepdims=True)
        v_hi = jnp.sum(dd * m_hi, axis=-1, keepdims=True)
        var = jnp.where(lo_half, v_lo, v_hi) * inv_dv
        y = d * lax.rsqrt(var + EPS) * gng + gnb
        o_ref[0, rows, :] = (y * jax.nn.silu(gate_ref[0, rows, :])).astype(BF16)
        return carry

    lax.fori_loop(0, n_chunks, out_body, 0, unroll=16)


def _retention(qkv, gate, dec, gng, gnb):
    bsz, t, _ = qkv.shape
    n_chunks = t // CHUNK
    n_pairs = RET_WIDTH // LANES

    def col(blk):
        return pl.BlockSpec((1, t, LANES), lambda b, p: (b, 0, blk + p))

    return pl.pallas_call(
        functools.partial(_ret_kernel, n_chunks=n_chunks),
        grid=(bsz, n_pairs),
        in_specs=[col(RQ_BLK), col(RK_BLK), col(RV_BLK),
                  pl.BlockSpec((1, t, LANES), lambda b, p: (b, 0, p)),
                  pl.BlockSpec((1, 4, LANES), lambda b, p: (p, 0, 0)),
                  pl.BlockSpec((1, LANES), lambda b, p: (0, p)),
                  pl.BlockSpec((1, LANES), lambda b, p: (0, p))],
        out_specs=pl.BlockSpec((1, t, LANES), lambda b, p: (b, 0, p)),
        out_shape=jax.ShapeDtypeStruct((bsz, t, RET_WIDTH), BF16),
        scratch_shapes=[pltpu.VMEM((n_chunks, CHUNK, LANES), F32),
                        pltpu.VMEM((n_chunks, CHUNK, LANES), F32)],
        compiler_params=_params("parallel", "parallel"),
        name="retention",
    )(qkv, qkv, qkv, gate, dec, gng, gnb)


def _split3(x):
    x1 = x.astype(BF16).astype(F32)
    r = x - x1
    x2 = r.astype(BF16).astype(F32)
    x3 = (r - x2).astype(BF16).astype(F32)
    return x1, x2, x3


def _diff_kernel(slopes_ref, trips_ref, qt_ref, k_ref, vt_ref, kfeat_ref, lam_ref, g_ref, o_ref,
                 dbias_ref, s_even_ref, s_odd_ref, acc_ref, qaug_ref, *, n_kv, n_q, lam0):
    c = slopes_ref[pl.program_id(1)] * LOG2E

    kio = lax.broadcasted_iota(jnp.int32, (KV_TILE, Q_TILE), 0)
    qio = lax.broadcasted_iota(jnp.int32, (KV_TILE, Q_TILE), 1)
    dbias_ref[...] = -(c * jnp.abs(kio - qio).astype(F32))

    lv = lam_ref[...]
    lam = (jnp.exp(jnp.sum(lv[0:1] * lv[1:2], axis=-1, keepdims=True))
           - jnp.exp(jnp.sum(lv[2:3] * lv[3:4], axis=-1, keepdims=True)) + lam0)
    out_gain = g_ref[...] * (1.0 - lam0)

    feat = lax.broadcasted_iota(jnp.int32, (LANES, 1), 0)
    rowi = lax.broadcasted_iota(jnp.int32, (AUG_ROWS, Q_TILE), 0)
    lane_q = lax.broadcasted_iota(jnp.int32, (1, Q_TILE), 1)
    c_row = jnp.full((1, Q_TILE), c, F32)
    c_parts = _split3(c_row)

    def key_rows(j):
        return pl.ds(pl.multiple_of(j * KV_TILE, KV_TILE), KV_TILE)

    def diag_tile(qs):
        return qs

    def tile_of(qs, pos):
        jd = diag_tile(qs)
        return jnp.where(pos == 0, jd, jnp.where(pos - 1 < jd, pos - 1, pos))

    def prepare_queries(qs, slot):
        qt = qt_ref[0, 0, qs].astype(F32)
        qpos = (qs * Q_TILE + lane_q).astype(F32)
        t_parts = _split3(-(c_row * qpos))
        coef = [float(FEAT_RADIX) * cp for cp in c_parts] + list(c_parts) + list(t_parts)
        aug = jnp.zeros((AUG_ROWS, Q_TILE), F32)
        for r, row in enumerate(coef):
            aug = jnp.where(rowi == r, row, aug)
        tail = jnp.concatenate([aug.astype(BF16),
                                jnp.zeros((LANES - AUG_ROWS, Q_TILE), BF16)], axis=0)
        qaug_ref[slot, 0] = jnp.concatenate([jnp.where(feat < DIFF_DH, qt, 0.0).astype(BF16), tail], axis=0)
        qaug_ref[slot, 1] = jnp.concatenate([jnp.where(feat < DIFF_DH, 0.0, qt).astype(BF16), tail], axis=0)

    def diag_scores(qs, slot):
        k_t = k_ref[0, key_rows(diag_tile(qs)), :]
        return lambda mp: jnp.dot(k_t, qaug_ref[slot, mp, 0:LANES, :],
                                  preferred_element_type=F32) + dbias_ref[...]

    def offdiag_scores(qs, slot, pos):
        j = tile_of(qs, pos)
        side = (j > diag_tile(qs)).astype(jnp.int32)
        lhs = jnp.concatenate([k_ref[0, key_rows(j), :], kfeat_ref[side, key_rows(j), :]], axis=1)
        return lambda mp: jnp.dot(lhs, qaug_ref[slot, mp], preferred_element_type=F32)

    def score_part(scores, mp, dst_ref):
        s = scores(mp)
        dst_ref[mp] = s
        return jnp.max(s, axis=0, keepdims=True)

    def softmax_part(slot, mp, vt_t, src_ref, col_max, m_old):
        m_new = jnp.maximum(m_old, col_max)
        alpha = jnp.exp2(m_old - m_new)
        p = jnp.exp2(src_ref[mp] - m_new).astype(BF16)
        acc_ref[slot, mp] = (alpha * acc_ref[slot, mp]
                             + jnp.dot(vt_t, p, preferred_element_type=F32))
        return m_new

    def stage_pair(scores, dst_ref, qs, slot, pos, src_ref, col_max, m_run):
        vt_t = vt_ref[0, 0, tile_of(qs, pos)]
        new_cm, new_m = [], []
        for mp in range(2):
            new_cm.append(score_part(scores, mp, dst_ref))
            new_m.append(softmax_part(slot, mp, vt_t, src_ref, col_max[mp], m_run[mp]))
        return tuple(new_cm), tuple(new_m)

    def write_output(qs, slot):
        a0 = acc_ref[slot, 0]
        a1 = acc_ref[slot, 1]
        att = (a0[:DIFF_DV] / a0[DIFF_DV:DIFF_DV + 1]
               - lam * (a1[:DIFF_DV] / a1[DIFF_DV:DIFF_DV + 1]))
        ms = jnp.mean(att * att, axis=0, keepdims=True)
        o_ref[0, 0, qs] = (att * lax.rsqrt(ms + EPS) * out_gain).astype(BF16)

    def query_tile_body(qs, cm):
        slot = qs % 2
        prv = jnp.maximum(qs - 1, 0)
        nxt = jnp.minimum(qs + 1, n_q - 1)
        acc_ref[slot] = jnp.zeros(acc_ref.shape[1:], F32)
        m_init = jnp.full((1, Q_TILE), -1e30, F32)

        def pair_body(i, carry):
            m_run, cm_even = carry
            cm_odd, m_run = stage_pair(offdiag_scores(qs, slot, 2 * i + 1), s_odd_ref,
                                       qs, slot, 2 * i, s_even_ref, cm_even, m_run)
            cm_even, m_run = stage_pair(offdiag_scores(qs, slot, 2 * i + 2), s_even_ref,
                                        qs, slot, 2 * i + 1, s_odd_ref, cm_odd, m_run)
            return m_run, cm_even

        m_run, cm_even = lax.fori_loop(0, trips_ref[0], pair_body, ((m_init, m_init), cm))
        cm_odd, m_run = stage_pair(offdiag_scores(qs, slot, n_kv - 1), s_odd_ref,
                                   qs, slot, n_kv - 2, s_even_ref, cm_even, m_run)
        write_output(prv, 1 - slot)
        prepare_queries(nxt, 1 - slot)
        cm_next, _ = stage_pair(diag_scores(nxt, 1 - slot), s_even_ref,
                                qs, slot, n_kv - 1, s_odd_ref, cm_odd, m_run)
        return cm_next

    acc_ref[...] = jnp.ones(acc_ref.shape, F32)
    prepare_queries(0, 0)
    first = diag_scores(0, 0)
    lax.fori_loop(0, n_q, query_tile_body,
                  tuple(score_part(first, mp, s_even_ref) for mp in range(2)))
    write_output(n_q - 1, (n_q - 1) % 2)


def _key_features(t):
    pos = jnp.arange(t, dtype=jnp.int32)
    hi = (pos // FEAT_RADIX).astype(F32)
    lo = (pos % FEAT_RADIX).astype(F32)
    one = jnp.ones((t,), F32)
    cols = jnp.stack([hi, hi, hi, lo, lo, lo, one, one, one], axis=1)
    cols = jnp.pad(cols, ((0, 0), (0, LANES - cols.shape[1])))
    return jnp.stack([cols, -cols]).astype(BF16)


def _diff_attention(qkv, qt, vt, slopes, lam_rows, subln_g, lam0):
    bsz, t, _ = qkv.shape
    n_kv = t // KV_TILE
    assert n_kv % 2 == 0 and KV_TILE == Q_TILE and t <= FEAT_RADIX * 256

    def feature_major(tiles, rows):
        return pl.BlockSpec((1, 1, tiles, rows, Q_TILE), lambda b, h, *_: (b, h, 0, 0, 0))

    return pl.pallas_call(
        functools.partial(_diff_kernel, n_kv=n_kv, n_q=t // Q_TILE, lam0=lam0),
        grid_spec=pltpu.PrefetchScalarGridSpec(
            num_scalar_prefetch=2,
            grid=(bsz, DIFF_HEADS),
            in_specs=[feature_major(n_kv, DIFF_DV),
                      pl.BlockSpec((1, t, LANES), lambda b, h, *_: (b, 0, DK_BLK + h)),
                      feature_major(n_kv, DIFF_DV + VT_PAD),
                      pl.BlockSpec((2, t, LANES), lambda b, h, *_: (0, 0, 0),
                                   pipeline_mode=pl.Buffered(1)),
                      pl.BlockSpec((8, LANES), lambda b, h, *_: (0, 0)),
                      pl.BlockSpec((DIFF_DV, 1), lambda b, h, *_: (0, 0))],
            out_specs=feature_major(n_kv, DIFF_DV),
            scratch_shapes=[pltpu.VMEM((KV_TILE, Q_TILE), F32),
                            pltpu.VMEM((2, KV_TILE, Q_TILE), F32),
                            pltpu.VMEM((2, KV_TILE, Q_TILE), F32),
                            pltpu.VMEM((2, 2, DIFF_DV + VT_PAD, Q_TILE), F32),
                            pltpu.VMEM((2, 2, 2 * LANES, Q_TILE), BF16)]),
        out_shape=jax.ShapeDtypeStruct((bsz, DIFF_HEADS, n_kv, DIFF_DV, Q_TILE), BF16),
        compiler_params=_params("parallel", "parallel"),
        name="diff_attention",
    )(slopes, jnp.full((1,), n_kv // 2 - 1, jnp.int32), qt, qkv, vt, _key_features(t),
      lam_rows, subln_g)


def _lambda_init(layer):
    return 0.8 - 0.6 * math.exp(-0.3 * layer)


def _trunk(x, p):
    bsz, t, _ = x.shape
    m = bsz * t
    xf = x.reshape(m, D_MODEL)
    for l in range(DEPTH):
        xf = _ffn_ln(xf, p["ffn1_w13"][l], p["ffn1_w2"][l], p["ln1_g"][l], p["ln1_b"][l])
        qkv, gate, qt, vt = _inproj(xf, p["w_in_fm"][l], p["w_in_tok"][l], bsz, t)
        qkv = qkv.reshape(bsz, t, TOK_WIDTH)
        ret = _retention(qkv, gate.reshape(bsz, t, RET_WIDTH), p["ret_dec"][l],
                         p["ret_gn_g"][l], p["ret_gn_b"][l])
        dif = _diff_attention(qkv, qt, vt, p["slopes"], p["lam_rows"][l],
                              p["diff_subln_g"][l], _lambda_init(l))
        xf = _ffn_ln(xf, p["ffn2_w13"][l], p["ffn2_w2"][l], p["ln3_g"][l], p["ln3_b"][l],
                     mix=(ret.reshape(m, RET_WIDTH), dif,
                          p["w_out"][l], p["ln2_g"][l], p["ln2_b"][l]))
    return xf.reshape(bsz, t, D_MODEL)


def kernel(x_prompt, x_sample, w_in, w_out, ret_decay_f, ret_decay_b, ret_gn_g, ret_gn_b,
           diff_lq1, diff_lk1, diff_lq2, diff_lk2, diff_subln_g,
           ffn1_w13, ffn1_w2, ffn2_w13, ffn2_w2,
           ln1_g, ln1_b, ln2_g, ln2_b, ln3_g, ln3_b):
    n_pairs = RET_WIDTH // LANES

    def pair_rows(dec):
        return jnp.broadcast_to(dec.reshape(DEPTH, n_pairs, 2, 1), (DEPTH, n_pairs, 2, LANES))

    row = lambda a: a.reshape(DEPTH, 1, -1)
    lam_rows = jnp.stack([diff_lq1, diff_lk1, diff_lq2, diff_lk2], axis=1)
    lam_rows = jnp.pad(lam_rows, ((0, 0), (0, 4), (0, LANES - DIFF_DH)))
    p = {
        "w_in_fm": jnp.concatenate([w_in[..., DQ_COL:DK_COL], w_in[..., DV_COL:]], axis=-1).astype(BF16),
        "w_in_tok": jnp.concatenate([w_in[..., :DQ_COL], w_in[..., DK_COL:DV_COL]], axis=-1).astype(BF16),
        "w_out": w_out.astype(BF16),
        "ffn1_w13": ffn1_w13.astype(BF16), "ffn1_w2": ffn1_w2.astype(BF16),
        "ffn2_w13": ffn2_w13.astype(BF16), "ffn2_w2": ffn2_w2.astype(BF16),
        "ret_dec": jnp.concatenate([pair_rows(ret_decay_f), pair_rows(ret_decay_b)], axis=2),
        "ret_gn_g": row(ret_gn_g), "ret_gn_b": row(ret_gn_b),
        "lam_rows": lam_rows, "diff_subln_g": diff_subln_g.reshape(DEPTH, DIFF_DV, 1),
        "slopes": jnp.asarray([2.0 ** (-8.0 * (h + 1) / DIFF_HEADS) for h in range(DIFF_HEADS)], F32),
        "ln1_g": row(ln1_g), "ln1_b": row(ln1_b), "ln2_g": row(ln2_g), "ln2_b": row(ln2_b),
        "ln3_g": row(ln3_g), "ln3_b": row(ln3_b),
    }
    return _trunk(x_prompt, p), _trunk(x_sample, p)
```

```python
import functools
import math

import jax
import jax.numpy as jnp
from jax import lax
from jax.experimental import pallas as pl
from jax.experimental.pallas import tpu as pltpu

D_MODEL = 1024
DEPTH = 2
RET_DK = 64
RET_WIDTH = 512
DIFF_HEADS = 4
DIFF_DH = 64
DIFF_DV = 128
DIFF_WIDTH = 512
D_FF = 2816
CHUNK = 128
EPS = 1e-5
ALPHA = (2 * DEPTH) ** 0.25

LANES = 128
ROW_TILE = 1024
FFN_ROW_TILE = 1024
ROW_SUB = 256
Q_TILE = 512
KV_TILE = 512
VT_PAD = 16
AUG_ROWS = 16
FEAT_RADIX = 64
LOG2E = math.log2(math.e)
VMEM_LIMIT = 56 * 1024 * 1024

DQ_COL, DK_COL, DV_COL = 2048, 2560, 3072
RQ_BLK, RK_BLK, RV_BLK, RG_BLK, DK_BLK = 0, 4, 8, 12, 16
TOK_WIDTH = 2560

BF16 = jnp.bfloat16
F32 = jnp.float32
NT_DIMS = (((1,), (1,)), ((), ()))


def _resident(shape):
    nd = len(shape)
    return pl.BlockSpec(shape, lambda *_: (0,) * nd, pipeline_mode=pl.Buffered(1))


def _layer_norm(y, g, b):
    mu = jnp.mean(y, axis=-1, keepdims=True)
    d = y - mu
    var = jnp.mean(d * d, axis=-1, keepdims=True)
    return d * lax.rsqrt(var + EPS) * g + b


def _params(*sem):
    return pltpu.CompilerParams(dimension_semantics=sem, vmem_limit_bytes=VMEM_LIMIT)


def _ffn_residual_ln(x, w13_ref, w2_ref, g_ref, b_ref):
    xb = x.astype(BF16)
    a = jnp.dot(xb, w13_ref[:, :D_FF], preferred_element_type=F32)
    b = jnp.dot(xb, w13_ref[:, D_FF:], preferred_element_type=F32)
    h = (jax.nn.silu(a) * b).astype(BF16)
    ffn = jnp.dot(h, w2_ref[...], preferred_element_type=F32)
    return _layer_norm(ALPHA * x + 0.5 * ffn, g_ref[...], b_ref[...])


def _ffn_ln_kernel(x_ref, w13_ref, w2_ref, g_ref, b_ref, o_ref):
    for r in range(0, FFN_ROW_TILE, ROW_SUB):
        o_ref[r:r + ROW_SUB, :] = _ffn_residual_ln(x_ref[r:r + ROW_SUB, :], w13_ref, w2_ref,
                                                   g_ref, b_ref)


def _mix_ffn_ln_kernel(x_ref, r_ref, dt_ref, wo_ref, gm_ref, bm_ref,
                       w13_ref, w2_ref, g_ref, b_ref, o_ref):
    def mixed(r):
        rows = slice(r, r + ROW_SUB)
        tile, col = divmod(r, Q_TILE)
        d = jnp.concatenate([dt_ref[0, h, tile, :, col:col + ROW_SUB].astype(F32).T.astype(BF16)
                             for h in range(DIFF_HEADS)], axis=1)
        mix = jnp.dot(r_ref[rows, :], wo_ref[0:RET_WIDTH, :], preferred_element_type=F32)
        mix = mix + jnp.dot(d, wo_ref[RET_WIDTH:, :], preferred_element_type=F32)
        return _layer_norm(ALPHA * x_ref[rows, :] + mix, gm_ref[...], bm_ref[...])

    x = mixed(0)
    for r in range(0, FFN_ROW_TILE, ROW_SUB):
        x_next = mixed(r + ROW_SUB) if r + ROW_SUB < FFN_ROW_TILE else None
        o_ref[r:r + ROW_SUB, :] = _ffn_residual_ln(x, w13_ref, w2_ref, g_ref, b_ref)
        x = x_next


def _ffn_ln(x, w13, w2, g, b, mix=None):
    m = x.shape[0]
    row = pl.BlockSpec((FFN_ROW_TILE, D_MODEL), lambda i: (i, 0))
    vec = _resident((1, D_MODEL))
    ffn_specs = [_resident(w13.shape), _resident(w2.shape), vec, vec]
    if mix is None:
        body, args, specs = _ffn_ln_kernel, (x, w13, w2, g, b), [row] + ffn_specs
    else:
        ret, dif, w_out, gm, bm = mix
        half = pl.BlockSpec((FFN_ROW_TILE, RET_WIDTH), lambda i: (i, 0))
        per_step = FFN_ROW_TILE // Q_TILE
        steps = dif.shape[2] // per_step
        dif_spec = pl.BlockSpec((1, DIFF_HEADS, per_step, DIFF_DV, Q_TILE),
                                lambda i: (i // steps, 0, i % steps, 0, 0))
        body = _mix_ffn_ln_kernel
        args = (x, ret, dif, w_out, gm, bm, w13, w2, g, b)
        specs = [row, half, dif_spec, _resident(w_out.shape), vec, vec] + ffn_specs
    return pl.pallas_call(
        body,
        grid=(m // FFN_ROW_TILE,),
        in_specs=specs,
        out_specs=row,
        out_shape=jax.ShapeDtypeStruct((m, D_MODEL), F32),
        compiler_params=_params("parallel"),
        name="ffn_ln" if mix is None else "mix_ffn_ln",
    )(*args)


def _inproj_kernel(x_ref, wf_ref, wt_ref, o_ref, gate_ref, qt_ref, vt_ref):
    ones_row = (lax.broadcasted_iota(jnp.int32, (VT_PAD, ROW_SUB), 0) == 0).astype(F32)
    for r in range(0, ROW_TILE, ROW_SUB):
        rows = slice(r, r + ROW_SUB)
        tile, col = divmod(r, Q_TILE)
        cols = slice(col, col + ROW_SUB)
        xb = x_ref[rows, :].astype(BF16)
        yf = jnp.dot(xb, wf_ref[...], preferred_element_type=F32)
        yt = jnp.dot(xb, wt_ref[...], preferred_element_type=F32)
        for h in range(DIFF_HEADS):
            q = yf[:, h * LANES:(h + 1) * LANES] * (LOG2E * DIFF_DH ** -0.5)
            qt_ref[0, h, tile, :, cols] = q.T.astype(BF16)
            v = yf[:, DIFF_WIDTH + h * LANES:DIFF_WIDTH + (h + 1) * LANES]
            vt_ref[0, h, tile, :, cols] = jnp.concatenate([v.T, ones_row], axis=0).astype(BF16)
        o_ref[rows, :] = yt.astype(BF16)
        gate_ref[rows, :] = yt[:, RG_BLK * LANES:RG_BLK * LANES + RET_WIDTH]


def _inproj(x, wf, wt, bsz, t):
    m = x.shape[0]
    per_step = ROW_TILE // Q_TILE
    steps = t // ROW_TILE

    def feature_major(rows):
        return pl.BlockSpec((1, DIFF_HEADS, per_step, rows, Q_TILE),
                            lambda i: (i // steps, 0, i % steps, 0, 0))

    return pl.pallas_call(
        _inproj_kernel,
        grid=(m // ROW_TILE,),
        in_specs=[pl.BlockSpec((ROW_TILE, D_MODEL), lambda i: (i, 0)),
                  _resident(wf.shape), _resident(wt.shape)],
        out_specs=[pl.BlockSpec((ROW_TILE, TOK_WIDTH), lambda i: (i, 0)),
                   pl.BlockSpec((ROW_TILE, RET_WIDTH), lambda i: (i, 0)),
                   feature_major(DIFF_DV), feature_major(DIFF_DV + VT_PAD)],
        out_shape=[jax.ShapeDtypeStruct((m, TOK_WIDTH), BF16),
                   jax.ShapeDtypeStruct((m, RET_WIDTH), F32),
                   jax.ShapeDtypeStruct((bsz, DIFF_HEADS, t // Q_TILE, DIFF_DV, Q_TILE), BF16),
                   jax.ShapeDtypeStruct((bsz, DIFF_HEADS, t // Q_TILE, DIFF_DV + VT_PAD, Q_TILE), BF16)],
        compiler_params=_params("parallel"),
        name="inproj",
    )(x, wf, wt)


def _ret_kernel(q_ref, k_ref, v_ref, gate_ref, dec_ref, gng_ref, gnb_ref, o_ref,
                sf_ref, sb_ref, *, n_chunks):
    c_len = CHUNK
    k_scale = RET_DK ** -0.5
    lane = lax.broadcasted_iota(jnp.int32, (1, LANES), 1)
    lo_half = lane < RET_DK
    m_lo = lo_half.astype(F32)
    m_hi = 1.0 - m_lo
    lg = jax.nn.log_sigmoid(dec_ref[0])
    lgf0, lgf1, lgb0, lgb1 = lg[0:1], lg[1:2], lg[2:3], lg[3:4]
    lgf = jnp.where(lo_half, lgf0, lgf1)
    lgb = jnp.where(lo_half, lgb0, lgb1)

    ri = lax.broadcasted_iota(jnp.int32, (c_len, c_len), 0)
    ci = lax.broadcasted_iota(jnp.int32, (c_len, c_len), 1)
    dist = (ri - ci).astype(F32)
    causal = dist >= 0.0
    block_diag = (ri < RET_DK) == (ci < RET_DK)

    def decay_tile(lf, lb):
        return jnp.where(causal, jnp.exp(lf * jnp.maximum(dist, 0.0)),
                         jnp.exp(lb * jnp.maximum(-dist, 0.0))) * k_scale

    d0 = decay_tile(lgf0, lgb0)
    d1 = decay_tile(lgf1, lgb1)
    row = lax.broadcasted_iota(jnp.int32, (c_len, LANES), 0).astype(F32)
    qdf = jnp.exp(lgf * (row + 1.0))
    qdb = jnp.exp(lgb * (c_len - row))
    kdf = jnp.exp(lgf * (c_len - 1.0 - row)) * k_scale
    kdb = jnp.exp(lgb * row) * k_scale
    gcf = jnp.exp(lgf * c_len)
    gcb = jnp.exp(lgb * c_len)

    def rows_of(c):
        return pl.ds(pl.multiple_of(c * c_len, c_len), c_len)

    def kv_body(c, carry):
        rows = rows_of(c)
        k = k_ref[0, rows, :].astype(F32)
        v = v_ref[0, rows, :]
        kvf = jnp.dot((k * kdf).T.astype(BF16), v, preferred_element_type=F32)
        kvb = jnp.dot((k * kdb).T.astype(BF16), v, preferred_element_type=F32)
        sf_ref[c] = jnp.where(block_diag, kvf, 0.0)
        sb_ref[c] = jnp.where(block_diag, kvb, 0.0)
        return carry

    lax.fori_loop(0, n_chunks, kv_body, 0, unroll=16)

    def fwd_scan(c, s):
        kv = sf_ref[c]
        sf_ref[c] = s
        return s * gcf + kv

    lax.fori_loop(0, n_chunks, fwd_scan, jnp.zeros((c_len, LANES), F32))

    def bwd_scan(i, s):
        c = n_chunks - 1 - i
        kv = sb_ref[c]
        sb_ref[c] = s
        return s * gcb + kv

    lax.fori_loop(0, n_chunks, bwd_scan, jnp.zeros((c_len, LANES), F32))

    gng = gng_ref[...]
    gnb = gnb_ref[...]
    inv_dv = 1.0 / RET_DK

    def out_body(c, carry):
        rows = rows_of(c)
        q = q_ref[0, rows, :].astype(F32)
        k = k_ref[0, rows, :]
        v = v_ref[0, rows, :].astype(F32)
        s0 = lax.dot_general((q * m_lo).astype(BF16), k, NT_DIMS, preferred_element_type=F32)
        s1 = lax.dot_general((q * m_hi).astype(BF16), k, NT_DIMS, preferred_element_type=F32)
        o = jnp.dot((s0 * d0).astype(BF16), (v * m_lo).astype(BF16), preferred_element_type=F32)
        o = o + jnp.dot((s1 * d1).astype(BF16), (v * m_hi).astype(BF16), preferred_element_type=F32)
        o = o + jnp.dot((q * qdf).astype(BF16), sf_ref[c].astype(BF16), preferred_element_type=F32)
        o = o + jnp.dot((q * qdb).astype(BF16), sb_ref[c].astype(BF16), preferred_element_type=F32)
        s_lo = jnp.sum(o * m_lo, axis=-1, keepdims=True)
        s_hi = jnp.sum(o * m_hi, axis=-1, keepdims=True)
        d = o - jnp.where(lo_half, s_lo, s_hi) * inv_dv
        dd = d * d
        v_lo = jnp.sum(dd * m_lo, axis=-1, keepdims=True)
        v_hi = jnp.sum(dd * m_hi, axis=-1, keepdims=True)
        var = jnp.where(lo_half, v_lo, v_hi) * inv_dv
        y = d * lax.rsqrt(var + EPS) * gng + gnb
        o_ref[0, rows, :] = (y * jax.nn.silu(gate_ref[0, rows, :])).astype(BF16)
        return carry

    lax.fori_loop(0, n_chunks, out_body, 0, unroll=16)


def _retention(qkv, gate, dec, gng, gnb):
    bsz, t, _ = qkv.shape
    n_chunks = t // CHUNK
    n_pairs = RET_WIDTH // LANES

    def col(blk):
        return pl.BlockSpec((1, t, LANES), lambda b, p: (b, 0, blk + p))

    return pl.pallas_call(
        functools.partial(_ret_kernel, n_chunks=n_chunks),
        grid=(bsz, n_pairs),
        in_specs=[col(RQ_BLK), col(RK_BLK), col(RV_BLK),
                  pl.BlockSpec((1, t, LANES), lambda b, p: (b, 0, p)),
                  pl.BlockSpec((1, 4, LANES), lambda b, p: (p, 0, 0)),
                  pl.BlockSpec((1, LANES), lambda b, p: (0, p)),
                  pl.BlockSpec((1, LANES), lambda b, p: (0, p))],
        out_specs=pl.BlockSpec((1, t, LANES), lambda b, p: (b, 0, p)),
        out_shape=jax.ShapeDtypeStruct((bsz, t, RET_WIDTH), BF16),
        scratch_shapes=[pltpu.VMEM((n_chunks, CHUNK, LANES), F32),
                        pltpu.VMEM((n_chunks, CHUNK, LANES), F32)],
        compiler_params=_params("parallel", "parallel"),
        name="retention",
    )(qkv, qkv, qkv, gate, dec, gng, gnb)


def _split3(x):
    x1 = x.astype(BF16).astype(F32)
    r = x - x1
    x2 = r.astype(BF16).astype(F32)
    x3 = (r - x2).astype(BF16).astype(F32)
    return x1, x2, x3


def _diff_kernel(slopes_ref, trips_ref, qt_ref, k_ref, vt_ref, kfeat_ref, lam_ref, g_ref, o_ref,
                 dbias_ref, s_even_ref, s_odd_ref, acc_ref, qaug_ref, *, n_kv, n_q, lam0):
    c = slopes_ref[pl.program_id(1)] * LOG2E

    kio = lax.broadcasted_iota(jnp.int32, (KV_TILE, Q_TILE), 0)
    qio = lax.broadcasted_iota(jnp.int32, (KV_TILE, Q_TILE), 1)
    dbias_ref[...] = -(c * jnp.abs(kio - qio).astype(F32))

    lv = lam_ref[...]
    lam = (jnp.exp(jnp.sum(lv[0:1] * lv[1:2], axis=-1, keepdims=True))
           - jnp.exp(jnp.sum(lv[2:3] * lv[3:4], axis=-1, keepdims=True)) + lam0)
    out_gain = g_ref[...] * (1.0 - lam0)

    feat = lax.broadcasted_iota(jnp.int32, (LANES, 1), 0)
    rowi = lax.broadcasted_iota(jnp.int32, (AUG_ROWS, Q_TILE), 0)
    lane_q = lax.broadcasted_iota(jnp.int32, (1, Q_TILE), 1)
    c_row = jnp.full((1, Q_TILE), c, F32)
    c_parts = _split3(c_row)

    def key_rows(j):
        return pl.ds(pl.multiple_of(j * KV_TILE, KV_TILE), KV_TILE)

    def diag_tile(qs):
        return qs

    def tile_of(qs, pos):
        jd = diag_tile(qs)
        return jnp.where(pos == 0, jd, jnp.where(pos - 1 < jd, pos - 1, pos))

    def prepare_queries(qs, slot):
        qt = qt_ref[0, 0, qs].astype(F32)
        qpos = (qs * Q_TILE + lane_q).astype(F32)
        t_parts = _split3(-(c_row * qpos))
        coef = [float(FEAT_RADIX) * cp for cp in c_parts] + list(c_parts) + list(t_parts)
        aug = jnp.zeros((AUG_ROWS, Q_TILE), F32)
        for r, row in enumerate(coef):
            aug = jnp.where(rowi == r, row, aug)
        tail = jnp.concatenate([aug.astype(BF16),
                                jnp.zeros((LANES - AUG_ROWS, Q_TILE), BF16)], axis=0)
        qaug_ref[slot, 0] = jnp.concatenate([jnp.where(feat < DIFF_DH, qt, 0.0).astype(BF16), tail], axis=0)
        qaug_ref[slot, 1] = jnp.concatenate([jnp.where(feat < DIFF_DH, 0.0, qt).astype(BF16), tail], axis=0)

    def diag_scores(qs, slot):
        k_t = k_ref[0, key_rows(diag_tile(qs)), :]
        return lambda mp: jnp.dot(k_t, qaug_ref[slot, mp, 0:LANES, :],
                                  preferred_element_type=F32) + dbias_ref[...]

    def offdiag_scores(qs, slot, pos):
        j = tile_of(qs, pos)
        side = (j > diag_tile(qs)).astype(jnp.int32)
        lhs = jnp.concatenate([k_ref[0, key_rows(j), :], kfeat_ref[side, key_rows(j), :]], axis=1)
        return lambda mp: jnp.dot(lhs, qaug_ref[slot, mp], preferred_element_type=F32)

    def score_part(scores, mp, dst_ref):
        s = scores(mp)
        dst_ref[mp] = s
        return jnp.max(s, axis=0, keepdims=True)

    def softmax_part(slot, mp, vt_t, src_ref, col_max, m_old):
        m_new = jnp.maximum(m_old, col_max)
        alpha = jnp.exp2(m_old - m_new)
        p = jnp.exp2(src_ref[mp] - m_new).astype(BF16)
        acc_ref[slot, mp] = (alpha * acc_ref[slot, mp]
                             + jnp.dot(vt_t, p, preferred_element_type=F32))
        return m_new

    def stage_pair(scores, dst_ref, qs, slot, pos, src_ref, col_max, m_run):
        vt_t = vt_ref[0, 0, tile_of(qs, pos)]
        new_cm, new_m = [], []
        for mp in range(2):
            new_cm.append(score_part(scores, mp, dst_ref))
            new_m.append(softmax_part(slot, mp, vt_t, src_ref, col_max[mp], m_run[mp]))
        return tuple(new_cm), tuple(new_m)

    def write_output(qs, slot):
        a0 = acc_ref[slot, 0]
        a1 = acc_ref[slot, 1]
        att = (a0[:DIFF_DV] / a0[DIFF_DV:DIFF_DV + 1]
               - lam * (a1[:DIFF_DV] / a1[DIFF_DV:DIFF_DV + 1]))
        ms = jnp.mean(att * att, axis=0, keepdims=True)
        o_ref[0, 0, qs] = (att * lax.rsqrt(ms + EPS) * out_gain).astype(BF16)

    def query_tile_body(qs, cm):
        slot = qs % 2
        prv = jnp.maximum(qs - 1, 0)
        nxt = jnp.minimum(qs + 1, n_q - 1)
        acc_ref[slot] = jnp.zeros(acc_ref.shape[1:], F32)
        m_init = jnp.full((1, Q_TILE), -1e30, F32)

        def pair_body(i, carry):
            m_run, cm_even = carry
            cm_odd, m_run = stage_pair(offdiag_scores(qs, slot, 2 * i + 1), s_odd_ref,
                                       qs, slot, 2 * i, s_even_ref, cm_even, m_run)
            cm_even, m_run = stage_pair(offdiag_scores(qs, slot, 2 * i + 2), s_even_ref,
                                        qs, slot, 2 * i + 1, s_odd_ref, cm_odd, m_run)
            return m_run, cm_even

        m_run, cm_even = lax.fori_loop(0, trips_ref[0], pair_body, ((m_init, m_init), cm))
        write_output(prv, 1 - slot)
        cm_odd, m_run = stage_pair(offdiag_scores(qs, slot, n_kv - 1), s_odd_ref,
                                   qs, slot, n_kv - 2, s_even_ref, cm_even, m_run)
        prepare_queries(nxt, 1 - slot)
        cm_next, _ = stage_pair(diag_scores(nxt, 1 - slot), s_even_ref,
                                qs, slot, n_kv - 1, s_odd_ref, cm_odd, m_run)
        return cm_next

    acc_ref[...] = jnp.ones(acc_ref.shape, F32)
    prepare_queries(0, 0)
    first = diag_scores(0, 0)
    lax.fori_loop(0, n_q, query_tile_body,
                  tuple(score_part(first, mp, s_even_ref) for mp in range(2)))
    write_output(n_q - 1, (n_q - 1) % 2)


def _key_features(t):
    pos = jnp.arange(t, dtype=jnp.int32)
    hi = (pos // FEAT_RADIX).astype(F32)
    lo = (pos % FEAT_RADIX).astype(F32)
    one = jnp.ones((t,), F32)
    cols = jnp.stack([hi, hi, hi, lo, lo, lo, one, one, one], axis=1)
    cols = jnp.pad(cols, ((0, 0), (0, LANES - cols.shape[1])))
    return jnp.stack([cols, -cols]).astype(BF16)


def _diff_attention(qkv, qt, vt, slopes, lam_rows, subln_g, lam0):
    bsz, t, _ = qkv.shape
    n_kv = t // KV_TILE
    assert n_kv % 2 == 0 and KV_TILE == Q_TILE and t <= FEAT_RADIX * 256

    def feature_major(tiles, rows):
        return pl.BlockSpec((1, 1, tiles, rows, Q_TILE), lambda b, h, *_: (b, h, 0, 0, 0))

    return pl.pallas_call(
        functools.partial(_diff_kernel, n_kv=n_kv, n_q=t // Q_TILE, lam0=lam0),
        grid_spec=pltpu.PrefetchScalarGridSpec(
            num_scalar_prefetch=2,
            grid=(bsz, DIFF_HEADS),
            in_specs=[feature_major(n_kv, DIFF_DV),
                      pl.BlockSpec((1, t, LANES), lambda b, h, *_: (b, 0, DK_BLK + h)),
                      feature_major(n_kv, DIFF_DV + VT_PAD),
                      pl.BlockSpec((2, t, LANES), lambda b, h, *_: (0, 0, 0),
                                   pipeline_mode=pl.Buffered(1)),
                      pl.BlockSpec((8, LANES), lambda b, h, *_: (0, 0)),
                      pl.BlockSpec((DIFF_DV, 1), lambda b, h, *_: (0, 0))],
            out_specs=feature_major(n_kv, DIFF_DV),
            scratch_shapes=[pltpu.VMEM((KV_TILE, Q_TILE), F32),
                            pltpu.VMEM((2, KV_TILE, Q_TILE), F32),
                            pltpu.VMEM((2, KV_TILE, Q_TILE), F32),
                            pltpu.VMEM((2, 2, DIFF_DV + VT_PAD, Q_TILE), F32),
                            pltpu.VMEM((2, 2, 2 * LANES, Q_TILE), BF16)]),
        out_shape=jax.ShapeDtypeStruct((bsz, DIFF_HEADS, n_kv, DIFF_DV, Q_TILE), BF16),
        compiler_params=_params("parallel", "parallel"),
        name="diff_attention",
    )(slopes, jnp.full((1,), n_kv // 2 - 1, jnp.int32), qt, qkv, vt, _key_features(t),
      lam_rows, subln_g)


def _lambda_init(layer):
    return 0.8 - 0.6 * math.exp(-0.3 * layer)


def _trunk(x, p):
    bsz, t, _ = x.shape
    m = bsz * t
    xf = x.reshape(m, D_MODEL)
    for l in range(DEPTH):
        xf = _ffn_ln(xf, p["ffn1_w13"][l], p["ffn1_w2"][l], p["ln1_g"][l], p["ln1_b"][l])
        qkv, gate, qt, vt = _inproj(xf, p["w_in_fm"][l], p["w_in_tok"][l], bsz, t)
        qkv = qkv.reshape(bsz, t, TOK_WIDTH)
        ret = _retention(qkv, gate.reshape(bsz, t, RET_WIDTH), p["ret_dec"][l],
                         p["ret_gn_g"][l], p["ret_gn_b"][l])
        dif = _diff_attention(qkv, qt, vt, p["slopes"], p["lam_rows"][l],
                              p["diff_subln_g"][l], _lambda_init(l))
        xf = _ffn_ln(xf, p["ffn2_w13"][l], p["ffn2_w2"][l], p["ln3_g"][l], p["ln3_b"][l],
                     mix=(ret.reshape(m, RET_WIDTH), dif,
                          p["w_out"][l], p["ln2_g"][l], p["ln2_b"][l]))
    return xf.reshape(bsz, t, D_MODEL)


def kernel(x_prompt, x_sample, w_in, w_out, ret_decay_f, ret_decay_b, ret_gn_g, ret_gn_b,
           diff_lq1, diff_lk1, diff_lq2, diff_lk2, diff_subln_g,
           ffn1_w13, ffn1_w2, ffn2_w13, ffn2_w2,
           ln1_g, ln1_b, ln2_g, ln2_b, ln3_g, ln3_b):
    n_pairs = RET_WIDTH // LANES

    def pair_rows(dec):
        return jnp.broadcast_to(dec.reshape(DEPTH, n_pairs, 2, 1), (DEPTH, n_pairs, 2, LANES))

    row = lambda a: a.reshape(DEPTH, 1, -1)
    lam_rows = jnp.stack([diff_lq1, diff_lk1, diff_lq2, diff_lk2], axis=1)
    lam_rows = jnp.pad(lam_rows, ((0, 0), (0, 4), (0, LANES - DIFF_DH)))
    p = {
        "w_in_fm": jnp.concatenate([w_in[..., DQ_COL:DK_COL], w_in[..., DV_COL:]], axis=-1).astype(BF16),
        "w_in_tok": jnp.concatenate([w_in[..., :DQ_COL], w_in[..., DK_COL:DV_COL]], axis=-1).astype(BF16),
        "w_out": w_out.astype(BF16),
        "ffn1_w13": ffn1_w13.astype(BF16), "ffn1_w2": ffn1_w2.astype(BF16),
        "ffn2_w13": ffn2_w13.astype(BF16), "ffn2_w2": ffn2_w2.astype(BF16),
        "ret_dec": jnp.concatenate([pair_rows(ret_decay_f), pair_rows(ret_decay_b)], axis=2),
        "ret_gn_g": row(ret_gn_g), "ret_gn_b": row(ret_gn_b),
        "lam_rows": lam_rows, "diff_subln_g": diff_subln_g.reshape(DEPTH, DIFF_DV, 1),
        "slopes": jnp.asarray([2.0 ** (-8.0 * (h + 1) / DIFF_HEADS) for h in range(DIFF_HEADS)], F32),
        "ln1_g": row(ln1_g), "ln1_b": row(ln1_b), "ln2_g": row(ln2_g), "ln2_b": row(ln2_b),
        "ln3_g": row(ln3_g), "ln3_b": row(ln3_b),
    }
    return _trunk(x_prompt, p), _trunk(x_sample, p)
```

```python
import functools
import math

import jax
import jax.numpy as jnp
from jax import lax
from jax.experimental import pallas as pl
from jax.experimental.pallas import tpu as pltpu

D_MODEL = 1024
DEPTH = 2
RET_DK = 64
RET_WIDTH = 512
DIFF_HEADS = 4
DIFF_DH = 64
DIFF_DV = 128
DIFF_WIDTH = 512
D_FF = 2816
CHUNK = 128
EPS = 1e-5
ALPHA = (2 * DEPTH) ** 0.25

LANES = 128
ROW_TILE = 1024
FFN_ROW_TILE = 1024
ROW_SUB = 256
Q_TILE = 512
KV_TILE = 512
VT_PAD = 16
AUG_ROWS = 16
FEAT_RADIX = 64
LOG2E = math.log2(math.e)
VMEM_LIMIT = 56 * 1024 * 1024

DQ_COL, DK_COL, DV_COL = 2048, 2560, 3072
RQ_BLK, RK_BLK, RV_BLK, RG_BLK, DK_BLK = 0, 4, 8, 12, 16
TOK_WIDTH = 2560

BF16 = jnp.bfloat16
F32 = jnp.float32
NT_DIMS = (((1,), (1,)), ((), ()))


def _resident(shape):
    nd = len(shape)
    return pl.BlockSpec(shape, lambda *_: (0,) * nd, pipeline_mode=pl.Buffered(1))


def _layer_norm(y, g, b):
    mu = jnp.mean(y, axis=-1, keepdims=True)
    d = y - mu
    var = jnp.mean(d * d, axis=-1, keepdims=True)
    return d * lax.rsqrt(var + EPS) * g + b


def _params(*sem):
    return pltpu.CompilerParams(dimension_semantics=sem, vmem_limit_bytes=VMEM_LIMIT)


def _ffn_residual_ln(x, w13_ref, w2_ref, g_ref, b_ref):
    xb = x.astype(BF16)
    a = jnp.dot(xb, w13_ref[:, :D_FF], preferred_element_type=F32)
    b = jnp.dot(xb, w13_ref[:, D_FF:], preferred_element_type=F32)
    h = (jax.nn.silu(a) * b).astype(BF16)
    ffn = jnp.dot(h, w2_ref[...], preferred_element_type=F32)
    return _layer_norm(ALPHA * x + 0.5 * ffn, g_ref[...], b_ref[...])


def _ffn_ln_kernel(x_ref, w13_ref, w2_ref, g_ref, b_ref, o_ref):
    for r in range(0, FFN_ROW_TILE, ROW_SUB):
        o_ref[r:r + ROW_SUB, :] = _ffn_residual_ln(x_ref[r:r + ROW_SUB, :], w13_ref, w2_ref,
                                                   g_ref, b_ref)


def _mix_ffn_ln_kernel(x_ref, r_ref, dt_ref, wo_ref, gm_ref, bm_ref,
                       w13_ref, w2_ref, g_ref, b_ref, o_ref):
    def mixed(r):
        rows = slice(r, r + ROW_SUB)
        tile, col = divmod(r, Q_TILE)
        d = jnp.concatenate([dt_ref[0, h, tile, :, col:col + ROW_SUB].astype(F32).T.astype(BF16)
                             for h in range(DIFF_HEADS)], axis=1)
        mix = jnp.dot(r_ref[rows, :], wo_ref[0:RET_WIDTH, :], preferred_element_type=F32)
        mix = mix + jnp.dot(d, wo_ref[RET_WIDTH:, :], preferred_element_type=F32)
        return _layer_norm(ALPHA * x_ref[rows, :] + mix, gm_ref[...], bm_ref[...])

    x = mixed(0)
    for r in range(0, FFN_ROW_TILE, ROW_SUB):
        x_next = mixed(r + ROW_SUB) if r + ROW_SUB < FFN_ROW_TILE else None
        o_ref[r:r + ROW_SUB, :] = _ffn_residual_ln(x, w13_ref, w2_ref, g_ref, b_ref)
        x = x_next


def _ffn_ln(x, w13, w2, g, b, mix=None):
    m = x.shape[0]
    row = pl.BlockSpec((FFN_ROW_TILE, D_MODEL), lambda i: (i, 0))
    vec = _resident((1, D_MODEL))
    ffn_specs = [_resident(w13.shape), _resident(w2.shape), vec, vec]
    if mix is None:
        body, args, specs = _ffn_ln_kernel, (x, w13, w2, g, b), [row] + ffn_specs
    else:
        ret, dif, w_out, gm, bm = mix
        half = pl.BlockSpec((FFN_ROW_TILE, RET_WIDTH), lambda i: (i, 0))
        per_step = FFN_ROW_TILE // Q_TILE
        steps = dif.shape[2] // per_step
        dif_spec = pl.BlockSpec((1, DIFF_HEADS, per_step, DIFF_DV, Q_TILE),
                                lambda i: (i // steps, 0, i % steps, 0, 0))
        body = _mix_ffn_ln_kernel
        args = (x, ret, dif, w_out, gm, bm, w13, w2, g, b)
        specs = [row, half, dif_spec, _resident(w_out.shape), vec, vec] + ffn_specs
    return pl.pallas_call(
        body,
        grid=(m // FFN_ROW_TILE,),
        in_specs=specs,
        out_specs=row,
        out_shape=jax.ShapeDtypeStruct((m, D_MODEL), F32),
        compiler_params=_params("parallel"),
        name="ffn_ln" if mix is None else "mix_ffn_ln",
    )(*args)


def _inproj_kernel(x_ref, wf_ref, wt_ref, o_ref, gate_ref, qt_ref, vt_ref):
    ones_row = (lax.broadcasted_iota(jnp.int32, (VT_PAD, ROW_SUB), 0) == 0).astype(F32)
    for r in range(0, ROW_TILE, ROW_SUB):
        rows = slice(r, r + ROW_SUB)
        tile, col = divmod(r, Q_TILE)
        cols = slice(col, col + ROW_SUB)
        xb = x_ref[rows, :].astype(BF16)
        yf = jnp.dot(xb, wf_ref[...], preferred_element_type=F32)
        yt = jnp.dot(xb, wt_ref[...], preferred_element_type=F32)
        for h in range(DIFF_HEADS):
            q = yf[:, h * LANES:(h + 1) * LANES] * (LOG2E * DIFF_DH ** -0.5)
            qt_ref[0, h, tile, :, cols] = q.T.astype(BF16)
            v = yf[:, DIFF_WIDTH + h * LANES:DIFF_WIDTH + (h + 1) * LANES]
            vt_ref[0, h, tile, :, cols] = jnp.concatenate([v.T, ones_row], axis=0).astype(BF16)
        o_ref[rows, :] = yt.astype(BF16)
        gate_ref[rows, :] = yt[:, RG_BLK * LANES:RG_BLK * LANES + RET_WIDTH]


def _inproj(x, wf, wt, bsz, t):
    m = x.shape[0]
    per_step = ROW_TILE // Q_TILE
    steps = t // ROW_TILE

    def feature_major(rows):
        return pl.BlockSpec((1, DIFF_HEADS, per_step, rows, Q_TILE),
                            lambda i: (i // steps, 0, i % steps, 0, 0))

    return pl.pallas_call(
        _inproj_kernel,
        grid=(m // ROW_TILE,),
        in_specs=[pl.BlockSpec((ROW_TILE, D_MODEL), lambda i: (i, 0)),
                  _resident(wf.shape), _resident(wt.shape)],
        out_specs=[pl.BlockSpec((ROW_TILE, TOK_WIDTH), lambda i: (i, 0)),
                   pl.BlockSpec((ROW_TILE, RET_WIDTH), lambda i: (i, 0)),
                   feature_major(DIFF_DV), feature_major(DIFF_DV + VT_PAD)],
        out_shape=[jax.ShapeDtypeStruct((m, TOK_WIDTH), BF16),
                   jax.ShapeDtypeStruct((m, RET_WIDTH), F32),
                   jax.ShapeDtypeStruct((bsz, DIFF_HEADS, t // Q_TILE, DIFF_DV, Q_TILE), BF16),
                   jax.ShapeDtypeStruct((bsz, DIFF_HEADS, t // Q_TILE, DIFF_DV + VT_PAD, Q_TILE), BF16)],
        compiler_params=_params("parallel"),
        name="inproj",
    )(x, wf, wt)


def _ret_kernel(q_ref, k_ref, v_ref, gate_ref, dec_ref, gng_ref, gnb_ref, o_ref,
                sf_ref, sb_ref, *, n_chunks):
    c_len = CHUNK
    k_scale = RET_DK ** -0.5
    lane = lax.broadcasted_iota(jnp.int32, (1, LANES), 1)
    lo_half = lane < RET_DK
    m_lo = lo_half.astype(F32)
    m_hi = 1.0 - m_lo
    lg = jax.nn.log_sigmoid(dec_ref[0])
    lgf0, lgf1, lgb0, lgb1 = lg[0:1], lg[1:2], lg[2:3], lg[3:4]
    lgf = jnp.where(lo_half, lgf0, lgf1)
    lgb = jnp.where(lo_half, lgb0, lgb1)

    ri = lax.broadcasted_iota(jnp.int32, (c_len, c_len), 0)
    ci = lax.broadcasted_iota(jnp.int32, (c_len, c_len), 1)
    dist = (ri - ci).astype(F32)
    causal = dist >= 0.0
    block_diag = (ri < RET_DK) == (ci < RET_DK)

    def decay_tile(lf, lb):
        return jnp.where(causal, jnp.exp(lf * jnp.maximum(dist, 0.0)),
                         jnp.exp(lb * jnp.maximum(-dist, 0.0))) * k_scale

    d0 = decay_tile(lgf0, lgb0)
    d1 = decay_tile(lgf1, lgb1)
    row = lax.broadcasted_iota(jnp.int32, (c_len, LANES), 0).astype(F32)
    qdf = jnp.exp(lgf * (row + 1.0))
    qdb = jnp.exp(lgb * (c_len - row))
    kdf = jnp.exp(lgf * (c_len - 1.0 - row)) * k_scale
    kdb = jnp.exp(lgb * row) * k_scale
    gcf = jnp.exp(lgf * c_len)
    gcb = jnp.exp(lgb * c_len)

    def rows_of(c):
        return pl.ds(pl.multiple_of(c * c_len, c_len), c_len)

    def kv_body(c, carry):
        rows = rows_of(c)
        k = k_ref[0, rows, :].astype(F32)
        v = v_ref[0, rows, :]
        kvf = jnp.dot((k * kdf).T.astype(BF16), v, preferred_element_type=F32)
        kvb = jnp.dot((k * kdb).T.astype(BF16), v, preferred_element_type=F32)
        sf_ref[c] = jnp.where(block_diag, kvf, 0.0)
        sb_ref[c] = jnp.where(block_diag, kvb, 0.0)
        return carry

    lax.fori_loop(0, n_chunks, kv_body, 0, unroll=32)

    def fwd_scan(c, s):
        kv = sf_ref[c]
        sf_ref[c] = s
        return s * gcf + kv

    lax.fori_loop(0, n_chunks, fwd_scan, jnp.zeros((c_len, LANES), F32))

    def bwd_scan(i, s):
        c = n_chunks - 1 - i
        kv = sb_ref[c]
        sb_ref[c] = s
        return s * gcb + kv

    lax.fori_loop(0, n_chunks, bwd_scan, jnp.zeros((c_len, LANES), F32))

    gng = gng_ref[...]
    gnb = gnb_ref[...]
    inv_dv = 1.0 / RET_DK

    def out_body(c, carry):
        rows = rows_of(c)
        q = q_ref[0, rows, :].astype(F32)
        k = k_ref[0, rows, :]
        v = v_ref[0, rows, :].astype(F32)
        s0 = lax.dot_general((q * m_lo).astype(BF16), k, NT_DIMS, preferred_element_type=F32)
        s1 = lax.dot_general((q * m_hi).astype(BF16), k, NT_DIMS, preferred_element_type=F32)
        o = jnp.dot((s0 * d0).astype(BF16), (v * m_lo).astype(BF16), preferred_element_type=F32)
        o = o + jnp.dot((s1 * d1).astype(BF16), (v * m_hi).astype(BF16), preferred_element_type=F32)
        o = o + jnp.dot((q * qdf).astype(BF16), sf_ref[c].astype(BF16), preferred_element_type=F32)
        o = o + jnp.dot((q * qdb).astype(BF16), sb_ref[c].astype(BF16), preferred_element_type=F32)
        s_lo = jnp.sum(o * m_lo, axis=-1, keepdims=True)
        s_hi = jnp.sum(o * m_hi, axis=-1, keepdims=True)
        d = o - jnp.where(lo_half, s_lo, s_hi) * inv_dv
        dd = d * d
        v_lo = jnp.sum(dd * m_lo, axis=-1, keepdims=True)
        v_hi = jnp.sum(dd * m_hi, axis=-1, keepdims=True)
        var = jnp.where(lo_half, v_lo, v_hi) * inv_dv
        y = d * lax.rsqrt(var + EPS) * gng + gnb
        o_ref[0, rows, :] = (y * jax.nn.silu(gate_ref[0, rows, :])).astype(BF16)
        return carry

    lax.fori_loop(0, n_chunks, out_body, 0, unroll=32)


def _retention(qkv, gate, dec, gng, gnb):
    bsz, t, _ = qkv.shape
    n_chunks = t // CHUNK
    n_pairs = RET_WIDTH // LANES

    def col(blk):
        return pl.BlockSpec((1, t, LANES), lambda b, p: (b, 0, blk + p))

    return pl.pallas_call(
        functools.partial(_ret_kernel, n_chunks=n_chunks),
        grid=(bsz, n_pairs),
        in_specs=[col(RQ_BLK), col(RK_BLK), col(RV_BLK),
                  pl.BlockSpec((1, t, LANES), lambda b, p: (b, 0, p)),
                  pl.BlockSpec((1, 4, LANES), lambda b, p: (p, 0, 0)),
                  pl.BlockSpec((1, LANES), lambda b, p: (0, p)),
                  pl.BlockSpec((1, LANES), lambda b, p: (0, p))],
        out_specs=pl.BlockSpec((1, t, LANES), lambda b, p: (b, 0, p)),
        out_shape=jax.ShapeDtypeStruct((bsz, t, RET_WIDTH), BF16),
        scratch_shapes=[pltpu.VMEM((n_chunks, CHUNK, LANES), F32),
                        pltpu.VMEM((n_chunks, CHUNK, LANES), F32)],
        compiler_params=_params("parallel", "parallel"),
        name="retention",
    )(qkv, qkv, qkv, gate, dec, gng, gnb)


def _split3(x):
    x1 = x.astype(BF16).astype(F32)
    r = x - x1
    x2 = r.astype(BF16).astype(F32)
    x3 = (r - x2).astype(BF16).astype(F32)
    return x1, x2, x3


def _diff_kernel(slopes_ref, trips_ref, qt_ref, k_ref, vt_ref, kfeat_ref, lam_ref, g_ref, o_ref,
                 dbias_ref, s_even_ref, s_odd_ref, acc_ref, qaug_ref, *, n_kv, n_q, lam0):
    c = slopes_ref[pl.program_id(1)] * LOG2E

    kio = lax.broadcasted_iota(jnp.int32, (KV_TILE, Q_TILE), 0)
    qio = lax.broadcasted_iota(jnp.int32, (KV_TILE, Q_TILE), 1)
    dbias_ref[...] = -(c * jnp.abs(kio - qio).astype(F32))

    lv = lam_ref[...]
    lam = (jnp.exp(jnp.sum(lv[0:1] * lv[1:2], axis=-1, keepdims=True))
           - jnp.exp(jnp.sum(lv[2:3] * lv[3:4], axis=-1, keepdims=True)) + lam0)
    out_gain = g_ref[...] * (1.0 - lam0)

    feat = lax.broadcasted_iota(jnp.int32, (LANES, 1), 0)
    rowi = lax.broadcasted_iota(jnp.int32, (AUG_ROWS, Q_TILE), 0)
    lane_q = lax.broadcasted_iota(jnp.int32, (1, Q_TILE), 1)
    c_row = jnp.full((1, Q_TILE), c, F32)
    c_parts = _split3(c_row)

    def key_rows(j):
        return pl.ds(pl.multiple_of(j * KV_TILE, KV_TILE), KV_TILE)

    def diag_tile(qs):
        return qs

    def tile_of(qs, pos):
        jd = diag_tile(qs)
        return jnp.where(pos == 0, jd, jnp.where(pos - 1 < jd, pos - 1, pos))

    def prepare_queries(qs, slot):
        qt = qt_ref[0, 0, qs].astype(F32)
        qpos = (qs * Q_TILE + lane_q).astype(F32)
        t_parts = _split3(-(c_row * qpos))
        coef = [float(FEAT_RADIX) * cp for cp in c_parts] + list(c_parts) + list(t_parts)
        aug = jnp.zeros((AUG_ROWS, Q_TILE), F32)
        for r, row in enumerate(coef):
            aug = jnp.where(rowi == r, row, aug)
        tail = jnp.concatenate([aug.astype(BF16),
                                jnp.zeros((LANES - AUG_ROWS, Q_TILE), BF16)], axis=0)
        qaug_ref[slot, 0] = jnp.concatenate([jnp.where(feat < DIFF_DH, qt, 0.0).astype(BF16), tail], axis=0)
        qaug_ref[slot, 1] = jnp.concatenate([jnp.where(feat < DIFF_DH, 0.0, qt).astype(BF16), tail], axis=0)

    def diag_scores(qs, slot):
        k_t = k_ref[0, key_rows(diag_tile(qs)), :]
        return lambda mp: jnp.dot(k_t, qaug_ref[slot, mp, 0:LANES, :],
                                  preferred_element_type=F32) + dbias_ref[...]

    def offdiag_scores(qs, slot, pos):
        j = tile_of(qs, pos)
        side = (j > diag_tile(qs)).astype(jnp.int32)
        lhs = jnp.concatenate([k_ref[0, key_rows(j), :], kfeat_ref[side, key_rows(j), :]], axis=1)
        return lambda mp: jnp.dot(lhs, qaug_ref[slot, mp], preferred_element_type=F32)

    def score_part(scores, mp, dst_ref):
        s = scores(mp)
        dst_ref[mp] = s
        return jnp.max(s, axis=0, keepdims=True)

    def softmax_part(slot, mp, vt_t, src_ref, col_max, m_old):
        m_new = jnp.maximum(m_old, col_max)
        alpha = jnp.exp2(m_old - m_new)
        p = jnp.exp2(src_ref[mp] - m_new).astype(BF16)
        acc_ref[slot, mp] = (alpha * acc_ref[slot, mp]
                             + jnp.dot(vt_t, p, preferred_element_type=F32))
        return m_new

    def stage_pair(scores, dst_ref, qs, slot, pos, src_ref, col_max, m_run):
        vt_t = vt_ref[0, 0, tile_of(qs, pos)]
        new_cm, new_m = [], []
        for mp in range(2):
            new_cm.append(score_part(scores, mp, dst_ref))
            new_m.append(softmax_part(slot, mp, vt_t, src_ref, col_max[mp], m_run[mp]))
        return tuple(new_cm), tuple(new_m)

    def write_output(qs, slot):
        a0 = acc_ref[slot, 0]
        a1 = acc_ref[slot, 1]
        att = (a0[:DIFF_DV] / a0[DIFF_DV:DIFF_DV + 1]
               - lam * (a1[:DIFF_DV] / a1[DIFF_DV:DIFF_DV + 1]))
        ms = jnp.mean(att * att, axis=0, keepdims=True)
        o_ref[0, 0, qs] = (att * lax.rsqrt(ms + EPS) * out_gain).astype(BF16)

    def query_tile_body(qs, cm):
        slot = qs % 2
        prv = jnp.maximum(qs - 1, 0)
        nxt = jnp.minimum(qs + 1, n_q - 1)
        acc_ref[slot] = jnp.zeros(acc_ref.shape[1:], F32)
        m_init = jnp.full((1, Q_TILE), -1e30, F32)

        def pair_body(i, carry):
            m_run, cm_even = carry
            cm_odd, m_run = stage_pair(offdiag_scores(qs, slot, 2 * i + 1), s_odd_ref,
                                       qs, slot, 2 * i, s_even_ref, cm_even, m_run)
            cm_even, m_run = stage_pair(offdiag_scores(qs, slot, 2 * i + 2), s_even_ref,
                                        qs, slot, 2 * i + 1, s_odd_ref, cm_odd, m_run)
            return m_run, cm_even

        m_run, cm_even = lax.fori_loop(0, trips_ref[0], pair_body, ((m_init, m_init), cm))
        write_output(prv, 1 - slot)
        cm_odd, m_run = stage_pair(offdiag_scores(qs, slot, n_kv - 1), s_odd_ref,
                                   qs, slot, n_kv - 2, s_even_ref, cm_even, m_run)
        prepare_queries(nxt, 1 - slot)
        cm_next, _ = stage_pair(diag_scores(nxt, 1 - slot), s_even_ref,
                                qs, slot, n_kv - 1, s_odd_ref, cm_odd, m_run)
        return cm_next

    acc_ref[...] = jnp.ones(acc_ref.shape, F32)
    prepare_queries(0, 0)
    first = diag_scores(0, 0)
    lax.fori_loop(0, n_q, query_tile_body,
                  tuple(score_part(first, mp, s_even_ref) for mp in range(2)))
    write_output(n_q - 1, (n_q - 1) % 2)


def _key_features(t):
    pos = jnp.arange(t, dtype=jnp.int32)
    hi = (pos // FEAT_RADIX).astype(F32)
    lo = (pos % FEAT_RADIX).astype(F32)
    one = jnp.ones((t,), F32)
    cols = jnp.stack([hi, hi, hi, lo, lo, lo, one, one, one], axis=1)
    cols = jnp.pad(cols, ((0, 0), (0, LANES - cols.shape[1])))
    return jnp.stack([cols, -cols]).astype(BF16)


def _diff_attention(qkv, qt, vt, slopes, lam_rows, subln_g, lam0):
    bsz, t, _ = qkv.shape
    n_kv = t // KV_TILE
    assert n_kv % 2 == 0 and KV_TILE == Q_TILE and t <= FEAT_RADIX * 256

    def feature_major(tiles, rows):
        return pl.BlockSpec((1, 1, tiles, rows, Q_TILE), lambda b, h, *_: (b, h, 0, 0, 0))

    return pl.pallas_call(
        functools.partial(_diff_kernel, n_kv=n_kv, n_q=t // Q_TILE, lam0=lam0),
        grid_spec=pltpu.PrefetchScalarGridSpec(
            num_scalar_prefetch=2,
            grid=(bsz, DIFF_HEADS),
            in_specs=[feature_major(n_kv, DIFF_DV),
                      pl.BlockSpec((1, t, LANES), lambda b, h, *_: (b, 0, DK_BLK + h)),
                      feature_major(n_kv, DIFF_DV + VT_PAD),
                      pl.BlockSpec((2, t, LANES), lambda b, h, *_: (0, 0, 0),
                                   pipeline_mode=pl.Buffered(1)),
                      pl.BlockSpec((8, LANES), lambda b, h, *_: (0, 0)),
                      pl.BlockSpec((DIFF_DV, 1), lambda b, h, *_: (0, 0))],
            out_specs=feature_major(n_kv, DIFF_DV),
            scratch_shapes=[pltpu.VMEM((KV_TILE, Q_TILE), F32),
                            pltpu.VMEM((2, KV_TILE, Q_TILE), F32),
                            pltpu.VMEM((2, KV_TILE, Q_TILE), F32),
                            pltpu.VMEM((2, 2, DIFF_DV + VT_PAD, Q_TILE), F32),
                            pltpu.VMEM((2, 2, 2 * LANES, Q_TILE), BF16)]),
        out_shape=jax.ShapeDtypeStruct((bsz, DIFF_HEADS, n_kv, DIFF_DV, Q_TILE), BF16),
        compiler_params=_params("parallel", "parallel"),
        name="diff_attention",
    )(slopes, jnp.full((1,), n_kv // 2 - 1, jnp.int32), qt, qkv, vt, _key_features(t),
      lam_rows, subln_g)


def _lambda_init(layer):
    return 0.8 - 0.6 * math.exp(-0.3 * layer)


def _trunk(x, p):
    bsz, t, _ = x.shape
    m = bsz * t
    xf = x.reshape(m, D_MODEL)
    for l in range(DEPTH):
        xf = _ffn_ln(xf, p["ffn1_w13"][l], p["ffn1_w2"][l], p["ln1_g"][l], p["ln1_b"][l])
        qkv, gate, qt, vt = _inproj(xf, p["w_in_fm"][l], p["w_in_tok"][l], bsz, t)
        qkv = qkv.reshape(bsz, t, TOK_WIDTH)
        ret = _retention(qkv, gate.reshape(bsz, t, RET_WIDTH), p["ret_dec"][l],
                         p["ret_gn_g"][l], p["ret_gn_b"][l])
        dif = _diff_attention(qkv, qt, vt, p["slopes"], p["lam_rows"][l],
                              p["diff_subln_g"][l], _lambda_init(l))
        xf = _ffn_ln(xf, p["ffn2_w13"][l], p["ffn2_w2"][l], p["ln3_g"][l], p["ln3_b"][l],
                     mix=(ret.reshape(m, RET_WIDTH), dif,
                          p["w_out"][l], p["ln2_g"][l], p["ln2_b"][l]))
    return xf.reshape(bsz, t, D_MODEL)


def kernel(x_prompt, x_sample, w_in, w_out, ret_decay_f, ret_decay_b, ret_gn_g, ret_gn_b,
           diff_lq1, diff_lk1, diff_lq2, diff_lk2, diff_subln_g,
           ffn1_w13, ffn1_w2, ffn2_w13, ffn2_w2,
           ln1_g, ln1_b, ln2_g, ln2_b, ln3_g, ln3_b):
    n_pairs = RET_WIDTH // LANES

    def pair_rows(dec):
        return jnp.broadcast_to(dec.reshape(DEPTH, n_pairs, 2, 1), (DEPTH, n_pairs, 2, LANES))

    row = lambda a: a.reshape(DEPTH, 1, -1)
    lam_rows = jnp.stack([diff_lq1, diff_lk1, diff_lq2, diff_lk2], axis=1)
    lam_rows = jnp.pad(lam_rows, ((0, 0), (0, 4), (0, LANES - DIFF_DH)))
    p = {
        "w_in_fm": jnp.concatenate([w_in[..., DQ_COL:DK_COL], w_in[..., DV_COL:]], axis=-1).astype(BF16),
        "w_in_tok": jnp.concatenate([w_in[..., :DQ_COL], w_in[..., DK_COL:DV_COL]], axis=-1).astype(BF16),
        "w_out": w_out.astype(BF16),
        "ffn1_w13": ffn1_w13.astype(BF16), "ffn1_w2": ffn1_w2.astype(BF16),
        "ffn2_w13": ffn2_w13.astype(BF16), "ffn2_w2": ffn2_w2.astype(BF16),
        "ret_dec": jnp.concatenate([pair_rows(ret_decay_f), pair_rows(ret_decay_b)], axis=2),
        "ret_gn_g": row(ret_gn_g), "ret_gn_b": row(ret_gn_b),
        "lam_rows": lam_rows, "diff_subln_g": diff_subln_g.reshape(DEPTH, DIFF_DV, 1),
        "slopes": jnp.asarray([2.0 ** (-8.0 * (h + 1) / DIFF_HEADS) for h in range(DIFF_HEADS)], F32),
        "ln1_g": row(ln1_g), "ln1_b": row(ln1_b), "ln2_g": row(ln2_g), "ln2_b": row(ln2_b),
        "ln3_g": row(ln3_g), "ln3_b": row(ln3_b),
    }
    return _trunk(x_prompt, p), _trunk(x_sample, p)
```

```python
import functools
import math

import jax
import jax.numpy as jnp
from jax import lax
from jax.experimental import pallas as pl
from jax.experimental.pallas import tpu as pltpu

D_MODEL = 1024
DEPTH = 2
RET_DK = 64
RET_WIDTH = 512
DIFF_HEADS = 4
DIFF_DH = 64
DIFF_DV = 128
DIFF_WIDTH = 512
D_FF = 2816
CHUNK = 128
RET_CHAINS = 32
EPS = 1e-5
ALPHA = (2 * DEPTH) ** 0.25

LANES = 128
ROW_TILE = 1024
FFN_ROW_TILE = 1024
ROW_SUB = 256
Q_TILE = 512
KV_TILE = 512
VT_PAD = 16
AUG_ROWS = 16
FEAT_RADIX = 64
LOG2E = math.log2(math.e)
VMEM_LIMIT = 56 * 1024 * 1024

DQ_COL, DK_COL, DV_COL = 2048, 2560, 3072
RQ_BLK, RK_BLK, RV_BLK, RG_BLK, DK_BLK = 0, 4, 8, 12, 16
TOK_WIDTH = 2560

BF16 = jnp.bfloat16
F32 = jnp.float32
NT_DIMS = (((1,), (1,)), ((), ()))


def _resident(shape):
    nd = len(shape)
    return pl.BlockSpec(shape, lambda *_: (0,) * nd, pipeline_mode=pl.Buffered(1))


def _layer_norm(y, g, b):
    mu = jnp.mean(y, axis=-1, keepdims=True)
    d = y - mu
    var = jnp.mean(d * d, axis=-1, keepdims=True)
    return d * lax.rsqrt(var + EPS) * g + b


def _params(*sem):
    return pltpu.CompilerParams(dimension_semantics=sem, vmem_limit_bytes=VMEM_LIMIT)


def _ffn_residual_ln(x, w13_ref, w2_ref, g_ref, b_ref):
    xb = x.astype(BF16)
    a = jnp.dot(xb, w13_ref[:, :D_FF], preferred_element_type=F32)
    b = jnp.dot(xb, w13_ref[:, D_FF:], preferred_element_type=F32)
    h = (jax.nn.silu(a) * b).astype(BF16)
    ffn = jnp.dot(h, w2_ref[...], preferred_element_type=F32)
    return _layer_norm(ALPHA * x + 0.5 * ffn, g_ref[...], b_ref[...])


def _ffn_ln_kernel(x_ref, w13_ref, w2_ref, g_ref, b_ref, o_ref):
    for r in range(0, FFN_ROW_TILE, ROW_SUB):
        o_ref[r:r + ROW_SUB, :] = _ffn_residual_ln(x_ref[r:r + ROW_SUB, :], w13_ref, w2_ref,
                                                   g_ref, b_ref)


def _mix_ffn_ln_kernel(x_ref, r_ref, dt_ref, wo_ref, gm_ref, bm_ref,
                       w13_ref, w2_ref, g_ref, b_ref, o_ref):
    def mixed(r):
        rows = slice(r, r + ROW_SUB)
        tile, col = divmod(r, Q_TILE)
        d = jnp.concatenate([dt_ref[0, h, tile, :, col:col + ROW_SUB].astype(F32).T.astype(BF16)
                             for h in range(DIFF_HEADS)], axis=1)
        mix = jnp.dot(r_ref[rows, :], wo_ref[0:RET_WIDTH, :], preferred_element_type=F32)
        mix = mix + jnp.dot(d, wo_ref[RET_WIDTH:, :], preferred_element_type=F32)
        return _layer_norm(ALPHA * x_ref[rows, :] + mix, gm_ref[...], bm_ref[...])

    x = mixed(0)
    for r in range(0, FFN_ROW_TILE, ROW_SUB):
        x_next = mixed(r + ROW_SUB) if r + ROW_SUB < FFN_ROW_TILE else None
        o_ref[r:r + ROW_SUB, :] = _ffn_residual_ln(x, w13_ref, w2_ref, g_ref, b_ref)
        x = x_next


def _ffn_ln(x, w13, w2, g, b, mix=None):
    m = x.shape[0]
    row = pl.BlockSpec((FFN_ROW_TILE, D_MODEL), lambda i: (i, 0))
    vec = _resident((1, D_MODEL))
    ffn_specs = [_resident(w13.shape), _resident(w2.shape), vec, vec]
    if mix is None:
        body, args, specs = _ffn_ln_kernel, (x, w13, w2, g, b), [row] + ffn_specs
    else:
        ret, dif, w_out, gm, bm = mix
        half = pl.BlockSpec((FFN_ROW_TILE, RET_WIDTH), lambda i: (i, 0))
        per_step = FFN_ROW_TILE // Q_TILE
        steps = dif.shape[2] // per_step
        dif_spec = pl.BlockSpec((1, DIFF_HEADS, per_step, DIFF_DV, Q_TILE),
                                lambda i: (i // steps, 0, i % steps, 0, 0))
        body = _mix_ffn_ln_kernel
        args = (x, ret, dif, w_out, gm, bm, w13, w2, g, b)
        specs = [row, half, dif_spec, _resident(w_out.shape), vec, vec] + ffn_specs
    return pl.pallas_call(
        body,
        grid=(m // FFN_ROW_TILE,),
        in_specs=specs,
        out_specs=row,
        out_shape=jax.ShapeDtypeStruct((m, D_MODEL), F32),
        compiler_params=_params("parallel"),
        name="ffn_ln" if mix is None else "mix_ffn_ln",
    )(*args)


def _inproj_kernel(x_ref, wf_ref, wt_ref, o_ref, gate_ref, qt_ref, vt_ref):
    ones_row = (lax.broadcasted_iota(jnp.int32, (VT_PAD, ROW_SUB), 0) == 0).astype(F32)
    for r in range(0, ROW_TILE, ROW_SUB):
        rows = slice(r, r + ROW_SUB)
        tile, col = divmod(r, Q_TILE)
        cols = slice(col, col + ROW_SUB)
        xb = x_ref[rows, :].astype(BF16)
        yf = jnp.dot(xb, wf_ref[...], preferred_element_type=F32)
        yt = jnp.dot(xb, wt_ref[...], preferred_element_type=F32)
        for h in range(DIFF_HEADS):
            q = yf[:, h * LANES:(h + 1) * LANES] * (LOG2E * DIFF_DH ** -0.5)
            qt_ref[0, h, tile, :, cols] = q.T.astype(BF16)
            v = yf[:, DIFF_WIDTH + h * LANES:DIFF_WIDTH + (h + 1) * LANES]
            vt_ref[0, h, tile, :, cols] = jnp.concatenate([v.T, ones_row], axis=0).astype(BF16)
        o_ref[rows, :] = yt.astype(BF16)
        gate_ref[rows, :] = yt[:, RG_BLK * LANES:RG_BLK * LANES + RET_WIDTH]


def _inproj(x, wf, wt, bsz, t):
    m = x.shape[0]
    per_step = ROW_TILE // Q_TILE
    steps = t // ROW_TILE

    def feature_major(rows):
        return pl.BlockSpec((1, DIFF_HEADS, per_step, rows, Q_TILE),
                            lambda i: (i // steps, 0, i % steps, 0, 0))

    return pl.pallas_call(
        _inproj_kernel,
        grid=(m // ROW_TILE,),
        in_specs=[pl.BlockSpec((ROW_TILE, D_MODEL), lambda i: (i, 0)),
                  _resident(wf.shape), _resident(wt.shape)],
        out_specs=[pl.BlockSpec((ROW_TILE, TOK_WIDTH), lambda i: (i, 0)),
                   pl.BlockSpec((ROW_TILE, RET_WIDTH), lambda i: (i, 0)),
                   feature_major(DIFF_DV), feature_major(DIFF_DV + VT_PAD)],
        out_shape=[jax.ShapeDtypeStruct((m, TOK_WIDTH), BF16),
                   jax.ShapeDtypeStruct((m, RET_WIDTH), F32),
                   jax.ShapeDtypeStruct((bsz, DIFF_HEADS, t // Q_TILE, DIFF_DV, Q_TILE), BF16),
                   jax.ShapeDtypeStruct((bsz, DIFF_HEADS, t // Q_TILE, DIFF_DV + VT_PAD, Q_TILE), BF16)],
        compiler_params=_params("parallel"),
        name="inproj",
    )(x, wf, wt)


def _ret_kernel(q_ref, k_ref, v_ref, gate_ref, dec_ref, gng_ref, gnb_ref, o_ref,
                sf_ref, sb_ref, *, n_chunks, n_sub):
    c_len = CHUNK
    k_scale = RET_DK ** -0.5
    lane = lax.broadcasted_iota(jnp.int32, (1, LANES), 1)
    lo_half = lane < RET_DK
    m_lo = lo_half.astype(F32)
    m_hi = 1.0 - m_lo
    ri = lax.broadcasted_iota(jnp.int32, (c_len, c_len), 0)
    ci = lax.broadcasted_iota(jnp.int32, (c_len, c_len), 1)
    dist = (ri - ci).astype(F32)
    causal = dist >= 0.0
    block_diag = (ri < RET_DK) == (ci < RET_DK)
    row = lax.broadcasted_iota(jnp.int32, (c_len, LANES), 0).astype(F32)
    inv_dv = 1.0 / RET_DK

    def decay_tile(lf, lb):
        return jnp.where(causal, jnp.exp(lf * jnp.maximum(dist, 0.0)),
                         jnp.exp(lb * jnp.maximum(-dist, 0.0))) * k_scale

    def pair_tiles(s):
        lg = jax.nn.log_sigmoid(dec_ref[s])
        lgf0, lgf1, lgb0, lgb1 = lg[0:1], lg[1:2], lg[2:3], lg[3:4]
        lgf = jnp.where(lo_half, lgf0, lgf1)
        lgb = jnp.where(lo_half, lgb0, lgb1)
        lanes = slice(s * LANES, (s + 1) * LANES)
        return dict(
            lanes=lanes, d0=decay_tile(lgf0, lgb0), d1=decay_tile(lgf1, lgb1),
            qdf=jnp.exp(lgf * (row + 1.0)), qdb=jnp.exp(lgb * (c_len - row)),
            kdf=jnp.exp(lgf * (c_len - 1.0 - row)) * k_scale, kdb=jnp.exp(lgb * row) * k_scale,
            gcf=jnp.exp(lgf * c_len), gcb=jnp.exp(lgb * c_len),
            gng=gng_ref[:, lanes], gnb=gnb_ref[:, lanes])

    pairs = [pair_tiles(s) for s in range(n_sub)]

    def rows_of(c):
        return pl.ds(pl.multiple_of(c * c_len, c_len), c_len)

    def kv_body(c, carry):
        rows = rows_of(c)
        for s, p in enumerate(pairs):
            k = k_ref[0, rows, p["lanes"]].astype(F32)
            v = v_ref[0, rows, p["lanes"]]
            kvf = jnp.dot((k * p["kdf"]).T.astype(BF16), v, preferred_element_type=F32)
            kvb = jnp.dot((k * p["kdb"]).T.astype(BF16), v, preferred_element_type=F32)
            sf_ref[s, c] = jnp.where(block_diag, kvf, 0.0)
            sb_ref[s, c] = jnp.where(block_diag, kvb, 0.0)
        return carry

    lax.fori_loop(0, n_chunks, kv_body, 0, unroll=RET_CHAINS)

    for s, p in enumerate(pairs):
        def fwd_scan(c, st, s=s, p=p):
            kv = sf_ref[s, c]
            sf_ref[s, c] = st
            return st * p["gcf"] + kv

        lax.fori_loop(0, n_chunks, fwd_scan, jnp.zeros((c_len, LANES), F32))

        def bwd_scan(i, st, s=s, p=p):
            c = n_chunks - 1 - i
            kv = sb_ref[s, c]
            sb_ref[s, c] = st
            return st * p["gcb"] + kv

        lax.fori_loop(0, n_chunks, bwd_scan, jnp.zeros((c_len, LANES), F32))

    def out_body(c, carry):
        rows = rows_of(c)
        for s, p in enumerate(pairs):
            q = q_ref[0, rows, p["lanes"]].astype(F32)
            k = k_ref[0, rows, p["lanes"]]
            v = v_ref[0, rows, p["lanes"]].astype(F32)
            s0 = lax.dot_general((q * m_lo).astype(BF16), k, NT_DIMS, preferred_element_type=F32)
            s1 = lax.dot_general((q * m_hi).astype(BF16), k, NT_DIMS, preferred_element_type=F32)
            o = jnp.dot((s0 * p["d0"]).astype(BF16), (v * m_lo).astype(BF16),
                        preferred_element_type=F32)
            o = o + jnp.dot((s1 * p["d1"]).astype(BF16), (v * m_hi).astype(BF16),
                            preferred_element_type=F32)
            o = o + jnp.dot((q * p["qdf"]).astype(BF16), sf_ref[s, c].astype(BF16),
                            preferred_element_type=F32)
            o = o + jnp.dot((q * p["qdb"]).astype(BF16), sb_ref[s, c].astype(BF16),
                            preferred_element_type=F32)
            s_lo = jnp.sum(o * m_lo, axis=-1, keepdims=True)
            s_hi = jnp.sum(o * m_hi, axis=-1, keepdims=True)
            d = o - jnp.where(lo_half, s_lo, s_hi) * inv_dv
            dd = d * d
            v_lo = jnp.sum(dd * m_lo, axis=-1, keepdims=True)
            v_hi = jnp.sum(dd * m_hi, axis=-1, keepdims=True)
            var = jnp.where(lo_half, v_lo, v_hi) * inv_dv
            y = d * lax.rsqrt(var + EPS) * p["gng"] + p["gnb"]
            o_ref[0, rows, p["lanes"]] = (y * jax.nn.silu(gate_ref[0, rows, p["lanes"]])).astype(BF16)
        return carry

    lax.fori_loop(0, n_chunks, out_body, 0, unroll=RET_CHAINS)


def _retention(qkv, gate, dec, gng, gnb):
    bsz, t, _ = qkv.shape
    n_chunks = t // CHUNK
    n_pairs = RET_WIDTH // LANES
    n_sub = max(1, min(n_pairs, RET_CHAINS // n_chunks))
    width = n_sub * LANES

    def col(blk):
        return pl.BlockSpec((1, t, width), lambda b, p: (b, 0, blk // n_sub + p))

    return pl.pallas_call(
        functools.partial(_ret_kernel, n_chunks=n_chunks, n_sub=n_sub),
        grid=(bsz, n_pairs // n_sub),
        in_specs=[col(RQ_BLK), col(RK_BLK), col(RV_BLK),
                  pl.BlockSpec((1, t, width), lambda b, p: (b, 0, p)),
                  pl.BlockSpec((n_sub, 4, LANES), lambda b, p: (p, 0, 0)),
                  pl.BlockSpec((1, width), lambda b, p: (0, p)),
                  pl.BlockSpec((1, width), lambda b, p: (0, p))],
        out_specs=pl.BlockSpec((1, t, width), lambda b, p: (b, 0, p)),
        out_shape=jax.ShapeDtypeStruct((bsz, t, RET_WIDTH), BF16),
        scratch_shapes=[pltpu.VMEM((n_sub, n_chunks, CHUNK, LANES), F32),
                        pltpu.VMEM((n_sub, n_chunks, CHUNK, LANES), F32)],
        compiler_params=_params("parallel", "parallel"),
        name="retention",
    )(qkv, qkv, qkv, gate, dec, gng, gnb)


def _split3(x):
    x1 = x.astype(BF16).astype(F32)
    r = x - x1
    x2 = r.astype(BF16).astype(F32)
    x3 = (r - x2).astype(BF16).astype(F32)
    return x1, x2, x3


def _diff_kernel(slopes_ref, trips_ref, qt_ref, k_ref, vt_ref, kfeat_ref, lam_ref, g_ref, o_ref,
                 dbias_ref, s_even_ref, s_odd_ref, acc_ref, qaug_ref, *, n_kv, n_q, lam0):
    c = slopes_ref[pl.program_id(1)] * LOG2E

    kio = lax.broadcasted_iota(jnp.int32, (KV_TILE, Q_TILE), 0)
    qio = lax.broadcasted_iota(jnp.int32, (KV_TILE, Q_TILE), 1)
    dbias_ref[...] = -(c * jnp.abs(kio - qio).astype(F32))

    lv = lam_ref[...]
    lam = (jnp.exp(jnp.sum(lv[0:1] * lv[1:2], axis=-1, keepdims=True))
           - jnp.exp(jnp.sum(lv[2:3] * lv[3:4], axis=-1, keepdims=True)) + lam0)
    out_gain = g_ref[...] * (1.0 - lam0)

    feat = lax.broadcasted_iota(jnp.int32, (LANES, 1), 0)
    rowi = lax.broadcasted_iota(jnp.int32, (AUG_ROWS, Q_TILE), 0)
    lane_q = lax.broadcasted_iota(jnp.int32, (1, Q_TILE), 1)
    c_row = jnp.full((1, Q_TILE), c, F32)
    c_parts = _split3(c_row)

    def key_rows(j):
        return pl.ds(pl.multiple_of(j * KV_TILE, KV_TILE), KV_TILE)

    def diag_tile(qs):
        return qs

    def tile_of(qs, pos):
        jd = diag_tile(qs)
        return jnp.where(pos == 0, jd, jnp.where(pos - 1 < jd, pos - 1, pos))

    def prepare_queries(qs, slot):
        qt = qt_ref[0, 0, qs].astype(F32)
        qpos = (qs * Q_TILE + lane_q).astype(F32)
        t_parts = _split3(-(c_row * qpos))
        coef = [float(FEAT_RADIX) * cp for cp in c_parts] + list(c_parts) + list(t_parts)
        aug = jnp.zeros((AUG_ROWS, Q_TILE), F32)
        for r, row in enumerate(coef):
            aug = jnp.where(rowi == r, row, aug)
        tail = jnp.concatenate([aug.astype(BF16),
                                jnp.zeros((LANES - AUG_ROWS, Q_TILE), BF16)], axis=0)
        qaug_ref[slot, 0] = jnp.concatenate([jnp.where(feat < DIFF_DH, qt, 0.0).astype(BF16), tail], axis=0)
        qaug_ref[slot, 1] = jnp.concatenate([jnp.where(feat < DIFF_DH, 0.0, qt).astype(BF16), tail], axis=0)

    def diag_scores(qs, slot):
        k_t = k_ref[0, key_rows(diag_tile(qs)), :]
        return lambda mp: jnp.dot(k_t, qaug_ref[slot, mp, 0:LANES, :],
                                  preferred_element_type=F32) + dbias_ref[...]

    def offdiag_scores(qs, slot, pos):
        j = tile_of(qs, pos)
        side = (j > diag_tile(qs)).astype(jnp.int32)
        lhs = jnp.concatenate([k_ref[0, key_rows(j), :], kfeat_ref[side, key_rows(j), :]], axis=1)
        return lambda mp: jnp.dot(lhs, qaug_ref[slot, mp], preferred_element_type=F32)

    def score_part(scores, mp, dst_ref):
        s = scores(mp)
        dst_ref[mp] = s
        return jnp.max(s, axis=0, keepdims=True)

    def softmax_part(slot, mp, vt_t, src_ref, col_max, m_old):
        m_new = jnp.maximum(m_old, col_max)
        alpha = jnp.exp2(m_old - m_new)
        p = jnp.exp2(src_ref[mp] - m_new).astype(BF16)
        acc_ref[slot, mp] = (alpha * acc_ref[slot, mp]
                             + jnp.dot(vt_t, p, preferred_element_type=F32))
        return m_new

    def stage_pair(scores, dst_ref, qs, slot, pos, src_ref, col_max, m_run):
        vt_t = vt_ref[0, 0, tile_of(qs, pos)]
        new_cm, new_m = [], []
        for mp in range(2):
            new_cm.append(score_part(scores, mp, dst_ref))
            new_m.append(softmax_part(slot, mp, vt_t, src_ref, col_max[mp], m_run[mp]))
        return tuple(new_cm), tuple(new_m)

    def write_output(qs, slot):
        a0 = acc_ref[slot, 0]
        a1 = acc_ref[slot, 1]
        att = (a0[:DIFF_DV] / a0[DIFF_DV:DIFF_DV + 1]
               - lam * (a1[:DIFF_DV] / a1[DIFF_DV:DIFF_DV + 1]))
        ms = jnp.mean(att * att, axis=0, keepdims=True)
        o_ref[0, 0, qs] = (att * lax.rsqrt(ms + EPS) * out_gain).astype(BF16)

    def query_tile_body(qs, cm):
        slot = qs % 2
        prv = jnp.maximum(qs - 1, 0)
        nxt = jnp.minimum(qs + 1, n_q - 1)
        acc_ref[slot] = jnp.zeros(acc_ref.shape[1:], F32)
        m_init = jnp.full((1, Q_TILE), -1e30, F32)

        def pair_body(i, carry):
            m_run, cm_even = carry
            cm_odd, m_run = stage_pair(offdiag_scores(qs, slot, 2 * i + 1), s_odd_ref,
                                       qs, slot, 2 * i, s_even_ref, cm_even, m_run)
            cm_even, m_run = stage_pair(offdiag_scores(qs, slot, 2 * i + 2), s_even_ref,
                                        qs, slot, 2 * i + 1, s_odd_ref, cm_odd, m_run)
            return m_run, cm_even

        m_run, cm_even = lax.fori_loop(0, trips_ref[0], pair_body, ((m_init, m_init), cm))
        write_output(prv, 1 - slot)
        cm_odd, m_run = stage_pair(offdiag_scores(qs, slot, n_kv - 1), s_odd_ref,
                                   qs, slot, n_kv - 2, s_even_ref, cm_even, m_run)
        prepare_queries(nxt, 1 - slot)
        cm_next, _ = stage_pair(diag_scores(nxt, 1 - slot), s_even_ref,
                                qs, slot, n_kv - 1, s_odd_ref, cm_odd, m_run)
        return cm_next

    acc_ref[...] = jnp.ones(acc_ref.shape, F32)
    prepare_queries(0, 0)
    first = diag_scores(0, 0)
    lax.fori_loop(0, n_q, query_tile_body,
                  tuple(score_part(first, mp, s_even_ref) for mp in range(2)))
    write_output(n_q - 1, (n_q - 1) % 2)


def _key_features(t):
    pos = jnp.arange(t, dtype=jnp.int32)
    hi = (pos // FEAT_RADIX).astype(F32)
    lo = (pos % FEAT_RADIX).astype(F32)
    one = jnp.ones((t,), F32)
    cols = jnp.stack([hi, hi, hi, lo, lo, lo, one, one, one], axis=1)
    cols = jnp.pad(cols, ((0, 0), (0, LANES - cols.shape[1])))
    return jnp.stack([cols, -cols]).astype(BF16)


def _diff_attention(qkv, qt, vt, slopes, lam_rows, subln_g, lam0):
    bsz, t, _ = qkv.shape
    n_kv = t // KV_TILE
    assert n_kv % 2 == 0 and KV_TILE == Q_TILE and t <= FEAT_RADIX * 256

    def feature_major(tiles, rows):
        return pl.BlockSpec((1, 1, tiles, rows, Q_TILE), lambda b, h, *_: (b, h, 0, 0, 0))

    return pl.pallas_call(
        functools.partial(_diff_kernel, n_kv=n_kv, n_q=t // Q_TILE, lam0=lam0),
        grid_spec=pltpu.PrefetchScalarGridSpec(
            num_scalar_prefetch=2,
            grid=(bsz, DIFF_HEADS),
            in_specs=[feature_major(n_kv, DIFF_DV),
                      pl.BlockSpec((1, t, LANES), lambda b, h, *_: (b, 0, DK_BLK + h)),
                      feature_major(n_kv, DIFF_DV + VT_PAD),
                      pl.BlockSpec((2, t, LANES), lambda b, h, *_: (0, 0, 0),
                                   pipeline_mode=pl.Buffered(1)),
                      pl.BlockSpec((8, LANES), lambda b, h, *_: (0, 0)),
                      pl.BlockSpec((DIFF_DV, 1), lambda b, h, *_: (0, 0))],
            out_specs=feature_major(n_kv, DIFF_DV),
            scratch_shapes=[pltpu.VMEM((KV_TILE, Q_TILE), F32),
                            pltpu.VMEM((2, KV_TILE, Q_TILE), F32),
                            pltpu.VMEM((2, KV_TILE, Q_TILE), F32),
                            pltpu.VMEM((2, 2, DIFF_DV + VT_PAD, Q_TILE), F32),
                            pltpu.VMEM((2, 2, 2 * LANES, Q_TILE), BF16)]),
        out_shape=jax.ShapeDtypeStruct((bsz, DIFF_HEADS, n_kv, DIFF_DV, Q_TILE), BF16),
        compiler_params=_params("parallel", "parallel"),
        name="diff_attention",
    )(slopes, jnp.full((1,), n_kv // 2 - 1, jnp.int32), qt, qkv, vt, _key_features(t),
      lam_rows, subln_g)


def _lambda_init(layer):
    return 0.8 - 0.6 * math.exp(-0.3 * layer)


def _trunk(x, p):
    bsz, t, _ = x.shape
    m = bsz * t
    xf = x.reshape(m, D_MODEL)
    for l in range(DEPTH):
        xf = _ffn_ln(xf, p["ffn1_w13"][l], p["ffn1_w2"][l], p["ln1_g"][l], p["ln1_b"][l])
        qkv, gate, qt, vt = _inproj(xf, p["w_in_fm"][l], p["w_in_tok"][l], bsz, t)
        qkv = qkv.reshape(bsz, t, TOK_WIDTH)
        ret = _retention(qkv, gate.reshape(bsz, t, RET_WIDTH), p["ret_dec"][l],
                         p["ret_gn_g"][l], p["ret_gn_b"][l])
        dif = _diff_attention(qkv, qt, vt, p["slopes"], p["lam_rows"][l],
                              p["diff_subln_g"][l], _lambda_init(l))
        xf = _ffn_ln(xf, p["ffn2_w13"][l], p["ffn2_w2"][l], p["ln3_g"][l], p["ln3_b"][l],
                     mix=(ret.reshape(m, RET_WIDTH), dif,
                          p["w_out"][l], p["ln2_g"][l], p["ln2_b"][l]))
    return xf.reshape(bsz, t, D_MODEL)


def kernel(x_prompt, x_sample, w_in, w_out, ret_decay_f, ret_decay_b, ret_gn_g, ret_gn_b,
           diff_lq1, diff_lk1, diff_lq2, diff_lk2, diff_subln_g,
           ffn1_w13, ffn1_w2, ffn2_w13, ffn2_w2,
           ln1_g, ln1_b, ln2_g, ln2_b, ln3_g, ln3_b):
    n_pairs = RET_WIDTH // LANES

    def pair_rows(dec):
        return jnp.broadcast_to(dec.reshape(DEPTH, n_pairs, 2, 1), (DEPTH, n_pairs, 2, LANES))

    row = lambda a: a.reshape(DEPTH, 1, -1)
    lam_rows = jnp.stack([diff_lq1, diff_lk1, diff_lq2, diff_lk2], axis=1)
    lam_rows = jnp.pad(lam_rows, ((0, 0), (0, 4), (0, LANES - DIFF_DH)))
    p = {
        "w_in_fm": jnp.concatenate([w_in[..., DQ_COL:DK_COL], w_in[..., DV_COL:]], axis=-1).astype(BF16),
        "w_in_tok": jnp.concatenate([w_in[..., :DQ_COL], w_in[..., DK_COL:DV_COL]], axis=-1).astype(BF16),
        "w_out": w_out.astype(BF16),
        "ffn1_w13": ffn1_w13.astype(BF16), "ffn1_w2": ffn1_w2.astype(BF16),
        "ffn2_w13": ffn2_w13.astype(BF16), "ffn2_w2": ffn2_w2.astype(BF16),
        "ret_dec": jnp.concatenate([pair_rows(ret_decay_f), pair_rows(ret_decay_b)], axis=2),
        "ret_gn_g": row(ret_gn_g), "ret_gn_b": row(ret_gn_b),
        "lam_rows": lam_rows, "diff_subln_g": diff_subln_g.reshape(DEPTH, DIFF_DV, 1),
        "slopes": jnp.asarray([2.0 ** (-8.0 * (h + 1) / DIFF_HEADS) for h in range(DIFF_HEADS)], F32),
        "ln1_g": row(ln1_g), "ln1_b": row(ln1_b), "ln2_g": row(ln2_g), "ln2_b": row(ln2_b),
        "ln3_g": row(ln3_g), "ln3_b": row(ln3_b),
    }
    return _trunk(x_prompt, p), _trunk(x_sample, p)
```

```python
import functools
import math

import jax
import jax.numpy as jnp
from jax import lax
from jax.experimental import pallas as pl
from jax.experimental.pallas import tpu as pltpu

D_MODEL = 1024
DEPTH = 2
RET_DK = 64
RET_WIDTH = 512
DIFF_HEADS = 4
DIFF_DH = 64
DIFF_DV = 128
DIFF_WIDTH = 512
D_FF = 2816
CHUNK = 128
RET_CHAINS = 64
EPS = 1e-5
ALPHA = (2 * DEPTH) ** 0.25

LANES = 128
ROW_TILE = 1024
FFN_ROW_TILE = 1024
ROW_SUB = 256
Q_TILE = 512
KV_TILE = 512
VT_PAD = 16
AUG_ROWS = 16
FEAT_RADIX = 64
LOG2E = math.log2(math.e)
VMEM_LIMIT = 56 * 1024 * 1024

DQ_COL, DK_COL, DV_COL = 2048, 2560, 3072
RQ_BLK, RK_BLK, RV_BLK, RG_BLK, DK_BLK = 0, 4, 8, 12, 16
TOK_WIDTH = 2560

BF16 = jnp.bfloat16
F32 = jnp.float32
NT_DIMS = (((1,), (1,)), ((), ()))


def _resident(shape):
    nd = len(shape)
    return pl.BlockSpec(shape, lambda *_: (0,) * nd, pipeline_mode=pl.Buffered(1))


def _layer_norm(y, g, b):
    mu = jnp.mean(y, axis=-1, keepdims=True)
    d = y - mu
    var = jnp.mean(d * d, axis=-1, keepdims=True)
    return d * lax.rsqrt(var + EPS) * g + b


def _params(*sem):
    return pltpu.CompilerParams(dimension_semantics=sem, vmem_limit_bytes=VMEM_LIMIT)


def _ffn_residual_ln(x, w13_ref, w2_ref, g_ref, b_ref):
    xb = x.astype(BF16)
    a = jnp.dot(xb, w13_ref[:, :D_FF], preferred_element_type=F32)
    b = jnp.dot(xb, w13_ref[:, D_FF:], preferred_element_type=F32)
    h = (jax.nn.silu(a) * b).astype(BF16)
    ffn = jnp.dot(h, w2_ref[...], preferred_element_type=F32)
    return _layer_norm(ALPHA * x + 0.5 * ffn, g_ref[...], b_ref[...])


def _ffn_ln_kernel(x_ref, w13_ref, w2_ref, g_ref, b_ref, o_ref):
    for r in range(0, FFN_ROW_TILE, ROW_SUB):
        o_ref[r:r + ROW_SUB, :] = _ffn_residual_ln(x_ref[r:r + ROW_SUB, :], w13_ref, w2_ref,
                                                   g_ref, b_ref)


def _mix_ffn_ln_kernel(x_ref, r_ref, dt_ref, wo_ref, gm_ref, bm_ref,
                       w13_ref, w2_ref, g_ref, b_ref, o_ref):
    def mixed(r):
        rows = slice(r, r + ROW_SUB)
        tile, col = divmod(r, Q_TILE)
        d = jnp.concatenate([dt_ref[0, h, tile, :, col:col + ROW_SUB].astype(F32).T.astype(BF16)
                             for h in range(DIFF_HEADS)], axis=1)
        mix = jnp.dot(r_ref[rows, :], wo_ref[0:RET_WIDTH, :], preferred_element_type=F32)
        mix = mix + jnp.dot(d, wo_ref[RET_WIDTH:, :], preferred_element_type=F32)
        return _layer_norm(ALPHA * x_ref[rows, :] + mix, gm_ref[...], bm_ref[...])

    x = mixed(0)
    for r in range(0, FFN_ROW_TILE, ROW_SUB):
        x_next = mixed(r + ROW_SUB) if r + ROW_SUB < FFN_ROW_TILE else None
        o_ref[r:r + ROW_SUB, :] = _ffn_residual_ln(x, w13_ref, w2_ref, g_ref, b_ref)
        x = x_next


def _ffn_ln(x, w13, w2, g, b, mix=None):
    m = x.shape[0]
    row = pl.BlockSpec((FFN_ROW_TILE, D_MODEL), lambda i: (i, 0))
    vec = _resident((1, D_MODEL))
    ffn_specs = [_resident(w13.shape), _resident(w2.shape), vec, vec]
    if mix is None:
        body, args, specs = _ffn_ln_kernel, (x, w13, w2, g, b), [row] + ffn_specs
    else:
        ret, dif, w_out, gm, bm = mix
        half = pl.BlockSpec((FFN_ROW_TILE, RET_WIDTH), lambda i: (i, 0))
        per_step = FFN_ROW_TILE // Q_TILE
        steps = dif.shape[2] // per_step
        dif_spec = pl.BlockSpec((1, DIFF_HEADS, per_step, DIFF_DV, Q_TILE),
                                lambda i: (i // steps, 0, i % steps, 0, 0))
        body = _mix_ffn_ln_kernel
        args = (x, ret, dif, w_out, gm, bm, w13, w2, g, b)
        specs = [row, half, dif_spec, _resident(w_out.shape), vec, vec] + ffn_specs
    return pl.pallas_call(
        body,
        grid=(m // FFN_ROW_TILE,),
        in_specs=specs,
        out_specs=row,
        out_shape=jax.ShapeDtypeStruct((m, D_MODEL), F32),
        compiler_params=_params("parallel"),
        name="ffn_ln" if mix is None else "mix_ffn_ln",
    )(*args)


def _inproj_kernel(x_ref, wf_ref, wt_ref, o_ref, gate_ref, qt_ref, vt_ref):
    ones_row = (lax.broadcasted_iota(jnp.int32, (VT_PAD, ROW_SUB), 0) == 0).astype(F32)
    for r in range(0, ROW_TILE, ROW_SUB):
        rows = slice(r, r + ROW_SUB)
        tile, col = divmod(r, Q_TILE)
        cols = slice(col, col + ROW_SUB)
        xb = x_ref[rows, :].astype(BF16)
        yf = jnp.dot(xb, wf_ref[...], preferred_element_type=F32)
        yt = jnp.dot(xb, wt_ref[...], preferred_element_type=F32)
        for h in range(DIFF_HEADS):
            q = yf[:, h * LANES:(h + 1) * LANES] * (LOG2E * DIFF_DH ** -0.5)
            qt_ref[0, h, tile, :, cols] = q.T.astype(BF16)
            v = yf[:, DIFF_WIDTH + h * LANES:DIFF_WIDTH + (h + 1) * LANES]
            vt_ref[0, h, tile, :, cols] = jnp.concatenate([v.T, ones_row], axis=0).astype(BF16)
        o_ref[rows, :] = yt.astype(BF16)
        gate_ref[rows, :] = yt[:, RG_BLK * LANES:RG_BLK * LANES + RET_WIDTH]


def _inproj(x, wf, wt, bsz, t):
    m = x.shape[0]
    per_step = ROW_TILE // Q_TILE
    steps = t // ROW_TILE

    def feature_major(rows):
        return pl.BlockSpec((1, DIFF_HEADS, per_step, rows, Q_TILE),
                            lambda i: (i // steps, 0, i % steps, 0, 0))

    return pl.pallas_call(
        _inproj_kernel,
        grid=(m // ROW_TILE,),
        in_specs=[pl.BlockSpec((ROW_TILE, D_MODEL), lambda i: (i, 0)),
                  _resident(wf.shape), _resident(wt.shape)],
        out_specs=[pl.BlockSpec((ROW_TILE, TOK_WIDTH), lambda i: (i, 0)),
                   pl.BlockSpec((ROW_TILE, RET_WIDTH), lambda i: (i, 0)),
                   feature_major(DIFF_DV), feature_major(DIFF_DV + VT_PAD)],
        out_shape=[jax.ShapeDtypeStruct((m, TOK_WIDTH), BF16),
                   jax.ShapeDtypeStruct((m, RET_WIDTH), F32),
                   jax.ShapeDtypeStruct((bsz, DIFF_HEADS, t // Q_TILE, DIFF_DV, Q_TILE), BF16),
                   jax.ShapeDtypeStruct((bsz, DIFF_HEADS, t // Q_TILE, DIFF_DV + VT_PAD, Q_TILE), BF16)],
        compiler_params=_params("parallel"),
        name="inproj",
    )(x, wf, wt)


def _ret_kernel(q_ref, k_ref, v_ref, gate_ref, dec_ref, gng_ref, gnb_ref, o_ref,
                sf_ref, sb_ref, *, n_chunks, n_sub):
    c_len = CHUNK
    k_scale = RET_DK ** -0.5
    lane = lax.broadcasted_iota(jnp.int32, (1, LANES), 1)
    lo_half = lane < RET_DK
    m_lo = lo_half.astype(F32)
    m_hi = 1.0 - m_lo
    ri = lax.broadcasted_iota(jnp.int32, (c_len, c_len), 0)
    ci = lax.broadcasted_iota(jnp.int32, (c_len, c_len), 1)
    dist = (ri - ci).astype(F32)
    causal = dist >= 0.0
    block_diag = (ri < RET_DK) == (ci < RET_DK)
    row = lax.broadcasted_iota(jnp.int32, (c_len, LANES), 0).astype(F32)
    inv_dv = 1.0 / RET_DK

    def decay_tile(lf, lb):
        return jnp.where(causal, jnp.exp(lf * jnp.maximum(dist, 0.0)),
                         jnp.exp(lb * jnp.maximum(-dist, 0.0))) * k_scale

    def pair_tiles(s):
        lg = jax.nn.log_sigmoid(dec_ref[s])
        lgf0, lgf1, lgb0, lgb1 = lg[0:1], lg[1:2], lg[2:3], lg[3:4]
        lgf = jnp.where(lo_half, lgf0, lgf1)
        lgb = jnp.where(lo_half, lgb0, lgb1)
        lanes = slice(s * LANES, (s + 1) * LANES)
        return dict(
            lanes=lanes, d0=decay_tile(lgf0, lgb0), d1=decay_tile(lgf1, lgb1),
            qdf=jnp.exp(lgf * (row + 1.0)), qdb=jnp.exp(lgb * (c_len - row)),
            kdf=jnp.exp(lgf * (c_len - 1.0 - row)) * k_scale, kdb=jnp.exp(lgb * row) * k_scale,
            gcf=jnp.exp(lgf * c_len), gcb=jnp.exp(lgb * c_len),
            gng=gng_ref[:, lanes], gnb=gnb_ref[:, lanes])

    pairs = [pair_tiles(s) for s in range(n_sub)]

    def rows_of(c):
        return pl.ds(pl.multiple_of(c * c_len, c_len), c_len)

    def kv_body(c, carry):
        rows = rows_of(c)
        for s, p in enumerate(pairs):
            k = k_ref[0, rows, p["lanes"]].astype(F32)
            v = v_ref[0, rows, p["lanes"]]
            kvf = jnp.dot((k * p["kdf"]).T.astype(BF16), v, preferred_element_type=F32)
            kvb = jnp.dot((k * p["kdb"]).T.astype(BF16), v, preferred_element_type=F32)
            sf_ref[s, c] = jnp.where(block_diag, kvf, 0.0)
            sb_ref[s, c] = jnp.where(block_diag, kvb, 0.0)
        return carry

    lax.fori_loop(0, n_chunks, kv_body, 0, unroll=RET_CHAINS)

    for s, p in enumerate(pairs):
        def fwd_scan(c, st, s=s, p=p):
            kv = sf_ref[s, c]
            sf_ref[s, c] = st
            return st * p["gcf"] + kv

        lax.fori_loop(0, n_chunks, fwd_scan, jnp.zeros((c_len, LANES), F32))

        def bwd_scan(i, st, s=s, p=p):
            c = n_chunks - 1 - i
            kv = sb_ref[s, c]
            sb_ref[s, c] = st
            return st * p["gcb"] + kv

        lax.fori_loop(0, n_chunks, bwd_scan, jnp.zeros((c_len, LANES), F32))

    def out_body(c, carry):
        rows = rows_of(c)
        for s, p in enumerate(pairs):
            q = q_ref[0, rows, p["lanes"]].astype(F32)
            k = k_ref[0, rows, p["lanes"]]
            v = v_ref[0, rows, p["lanes"]].astype(F32)
            s0 = lax.dot_general((q * m_lo).astype(BF16), k, NT_DIMS, preferred_element_type=F32)
            s1 = lax.dot_general((q * m_hi).astype(BF16), k, NT_DIMS, preferred_element_type=F32)
            o = jnp.dot((s0 * p["d0"]).astype(BF16), (v * m_lo).astype(BF16),
                        preferred_element_type=F32)
            o = o + jnp.dot((s1 * p["d1"]).astype(BF16), (v * m_hi).astype(BF16),
                            preferred_element_type=F32)
            o = o + jnp.dot((q * p["qdf"]).astype(BF16), sf_ref[s, c].astype(BF16),
                            preferred_element_type=F32)
            o = o + jnp.dot((q * p["qdb"]).astype(BF16), sb_ref[s, c].astype(BF16),
                            preferred_element_type=F32)
            s_lo = jnp.sum(o * m_lo, axis=-1, keepdims=True)
            s_hi = jnp.sum(o * m_hi, axis=-1, keepdims=True)
            d = o - jnp.where(lo_half, s_lo, s_hi) * inv_dv
            dd = d * d
            v_lo = jnp.sum(dd * m_lo, axis=-1, keepdims=True)
            v_hi = jnp.sum(dd * m_hi, axis=-1, keepdims=True)
            var = jnp.where(lo_half, v_lo, v_hi) * inv_dv
            y = d * lax.rsqrt(var + EPS) * p["gng"] + p["gnb"]
            o_ref[0, rows, p["lanes"]] = (y * jax.nn.silu(gate_ref[0, rows, p["lanes"]])).astype(BF16)
        return carry

    lax.fori_loop(0, n_chunks, out_body, 0, unroll=RET_CHAINS)


def _retention(qkv, gate, dec, gng, gnb):
    bsz, t, _ = qkv.shape
    n_chunks = t // CHUNK
    n_pairs = RET_WIDTH // LANES
    n_sub = max(1, min(n_pairs, RET_CHAINS // n_chunks))
    width = n_sub * LANES

    def col(blk):
        return pl.BlockSpec((1, t, width), lambda b, p: (b, 0, blk // n_sub + p))

    return pl.pallas_call(
        functools.partial(_ret_kernel, n_chunks=n_chunks, n_sub=n_sub),
        grid=(bsz, n_pairs // n_sub),
        in_specs=[col(RQ_BLK), col(RK_BLK), col(RV_BLK),
                  pl.BlockSpec((1, t, width), lambda b, p: (b, 0, p)),
                  pl.BlockSpec((n_sub, 4, LANES), lambda b, p: (p, 0, 0)),
                  pl.BlockSpec((1, width), lambda b, p: (0, p)),
                  pl.BlockSpec((1, width), lambda b, p: (0, p))],
        out_specs=pl.BlockSpec((1, t, width), lambda b, p: (b, 0, p)),
        out_shape=jax.ShapeDtypeStruct((bsz, t, RET_WIDTH), BF16),
        scratch_shapes=[pltpu.VMEM((n_sub, n_chunks, CHUNK, LANES), F32),
                        pltpu.VMEM((n_sub, n_chunks, CHUNK, LANES), F32)],
        compiler_params=_params("parallel", "parallel"),
        name="retention",
    )(qkv, qkv, qkv, gate, dec, gng, gnb)


def _split3(x):
    x1 = x.astype(BF16).astype(F32)
    r = x - x1
    x2 = r.astype(BF16).astype(F32)
    x3 = (r - x2).astype(BF16).astype(F32)
    return x1, x2, x3


def _diff_kernel(slopes_ref, trips_ref, qt_ref, k_ref, vt_ref, kfeat_ref, lam_ref, g_ref, o_ref,
                 dbias_ref, s_even_ref, s_odd_ref, acc_ref, qaug_ref, *, n_kv, n_q, lam0):
    c = slopes_ref[pl.program_id(1)] * LOG2E

    kio = lax.broadcasted_iota(jnp.int32, (KV_TILE, Q_TILE), 0)
    qio = lax.broadcasted_iota(jnp.int32, (KV_TILE, Q_TILE), 1)
    dbias_ref[...] = -(c * jnp.abs(kio - qio).astype(F32))

    lv = lam_ref[...]
    lam = (jnp.exp(jnp.sum(lv[0:1] * lv[1:2], axis=-1, keepdims=True))
           - jnp.exp(jnp.sum(lv[2:3] * lv[3:4], axis=-1, keepdims=True)) + lam0)
    out_gain = g_ref[...] * (1.0 - lam0)

    feat = lax.broadcasted_iota(jnp.int32, (LANES, 1), 0)
    rowi = lax.broadcasted_iota(jnp.int32, (AUG_ROWS, Q_TILE), 0)
    lane_q = lax.broadcasted_iota(jnp.int32, (1, Q_TILE), 1)
    c_row = jnp.full((1, Q_TILE), c, F32)
    c_parts = _split3(c_row)

    def key_rows(j):
        return pl.ds(pl.multiple_of(j * KV_TILE, KV_TILE), KV_TILE)

    def diag_tile(qs):
        return qs

    def tile_of(qs, pos):
        jd = diag_tile(qs)
        return jnp.where(pos == 0, jd, jnp.where(pos - 1 < jd, pos - 1, pos))

    def prepare_queries(qs, slot):
        qt = qt_ref[0, 0, qs].astype(F32)
        qpos = (qs * Q_TILE + lane_q).astype(F32)
        t_parts = _split3(-(c_row * qpos))
        coef = [float(FEAT_RADIX) * cp for cp in c_parts] + list(c_parts) + list(t_parts)
        aug = jnp.zeros((AUG_ROWS, Q_TILE), F32)
        for r, row in enumerate(coef):
            aug = jnp.where(rowi == r, row, aug)
        tail = jnp.concatenate([aug.astype(BF16),
                                jnp.zeros((LANES - AUG_ROWS, Q_TILE), BF16)], axis=0)
        qaug_ref[slot, 0] = jnp.concatenate([jnp.where(feat < DIFF_DH, qt, 0.0).astype(BF16), tail], axis=0)
        qaug_ref[slot, 1] = jnp.concatenate([jnp.where(feat < DIFF_DH, 0.0, qt).astype(BF16), tail], axis=0)

    def diag_scores(qs, slot):
        k_t = k_ref[0, key_rows(diag_tile(qs)), :]
        return lambda mp: jnp.dot(k_t, qaug_ref[slot, mp, 0:LANES, :],
                                  preferred_element_type=F32) + dbias_ref[...]

    def offdiag_scores(qs, slot, pos):
        j = tile_of(qs, pos)
        side = (j > diag_tile(qs)).astype(jnp.int32)
        lhs = jnp.concatenate([k_ref[0, key_rows(j), :], kfeat_ref[side, key_rows(j), :]], axis=1)
        return lambda mp: jnp.dot(lhs, qaug_ref[slot, mp], preferred_element_type=F32)

    def score_part(scores, mp, dst_ref):
        s = scores(mp)
        dst_ref[mp] = s
        return jnp.max(s, axis=0, keepdims=True)

    def softmax_part(slot, mp, vt_t, src_ref, col_max, m_old):
        m_new = jnp.maximum(m_old, col_max)
        alpha = jnp.exp2(m_old - m_new)
        p = jnp.exp2(src_ref[mp] - m_new).astype(BF16)
        acc_ref[slot, mp] = (alpha * acc_ref[slot, mp]
                             + jnp.dot(vt_t, p, preferred_element_type=F32))
        return m_new

    def stage_pair(scores, dst_ref, qs, slot, pos, src_ref, col_max, m_run):
        vt_t = vt_ref[0, 0, tile_of(qs, pos)]
        new_cm, new_m = [], []
        for mp in range(2):
            new_cm.append(score_part(scores, mp, dst_ref))
            new_m.append(softmax_part(slot, mp, vt_t, src_ref, col_max[mp], m_run[mp]))
        return tuple(new_cm), tuple(new_m)

    def write_output(qs, slot):
        a0 = acc_ref[slot, 0]
        a1 = acc_ref[slot, 1]
        att = (a0[:DIFF_DV] / a0[DIFF_DV:DIFF_DV + 1]
               - lam * (a1[:DIFF_DV] / a1[DIFF_DV:DIFF_DV + 1]))
        ms = jnp.mean(att * att, axis=0, keepdims=True)
        o_ref[0, 0, qs] = (att * lax.rsqrt(ms + EPS) * out_gain).astype(BF16)

    def query_tile_body(qs, cm):
        slot = qs % 2
        prv = jnp.maximum(qs - 1, 0)
        nxt = jnp.minimum(qs + 1, n_q - 1)
        acc_ref[slot] = jnp.zeros(acc_ref.shape[1:], F32)
        m_init = jnp.full((1, Q_TILE), -1e30, F32)

        def pair_body(i, carry):
            m_run, cm_even = carry
            cm_odd, m_run = stage_pair(offdiag_scores(qs, slot, 2 * i + 1), s_odd_ref,
                                       qs, slot, 2 * i, s_even_ref, cm_even, m_run)
            cm_even, m_run = stage_pair(offdiag_scores(qs, slot, 2 * i + 2), s_even_ref,
                                        qs, slot, 2 * i + 1, s_odd_ref, cm_odd, m_run)
            return m_run, cm_even

        m_run, cm_even = lax.fori_loop(0, trips_ref[0], pair_body, ((m_init, m_init), cm))
        write_output(prv, 1 - slot)
        cm_odd, m_run = stage_pair(offdiag_scores(qs, slot, n_kv - 1), s_odd_ref,
                                   qs, slot, n_kv - 2, s_even_ref, cm_even, m_run)
        prepare_queries(nxt, 1 - slot)
        cm_next, _ = stage_pair(diag_scores(nxt, 1 - slot), s_even_ref,
                                qs, slot, n_kv - 1, s_odd_ref, cm_odd, m_run)
        return cm_next

    acc_ref[...] = jnp.ones(acc_ref.shape, F32)
    prepare_queries(0, 0)
    first = diag_scores(0, 0)
    lax.fori_loop(0, n_q, query_tile_body,
                  tuple(score_part(first, mp, s_even_ref) for mp in range(2)))
    write_output(n_q - 1, (n_q - 1) % 2)


def _key_features(t):
    pos = jnp.arange(t, dtype=jnp.int32)
    hi = (pos // FEAT_RADIX).astype(F32)
    lo = (pos % FEAT_RADIX).astype(F32)
    one = jnp.ones((t,), F32)
    cols = jnp.stack([hi, hi, hi, lo, lo, lo, one, one, one], axis=1)
    cols = jnp.pad(cols, ((0, 0), (0, LANES - cols.shape[1])))
    return jnp.stack([cols, -cols]).astype(BF16)


def _diff_attention(qkv, qt, vt, slopes, lam_rows, subln_g, lam0):
    bsz, t, _ = qkv.shape
    n_kv = t // KV_TILE
    assert n_kv % 2 == 0 and KV_TILE == Q_TILE and t <= FEAT_RADIX * 256

    def feature_major(tiles, rows):
        return pl.BlockSpec((1, 1, tiles, rows, Q_TILE), lambda b, h, *_: (b, h, 0, 0, 0))

    return pl.pallas_call(
        functools.partial(_diff_kernel, n_kv=n_kv, n_q=t // Q_TILE, lam0=lam0),
        grid_spec=pltpu.PrefetchScalarGridSpec(
            num_scalar_prefetch=2,
            grid=(bsz, DIFF_HEADS),
            in_specs=[feature_major(n_kv, DIFF_DV),
                      pl.BlockSpec((1, t, LANES), lambda b, h, *_: (b, 0, DK_BLK + h)),
                      feature_major(n_kv, DIFF_DV + VT_PAD),
                      pl.BlockSpec((2, t, LANES), lambda b, h, *_: (0, 0, 0),
                                   pipeline_mode=pl.Buffered(1)),
                      pl.BlockSpec((8, LANES), lambda b, h, *_: (0, 0)),
                      pl.BlockSpec((DIFF_DV, 1), lambda b, h, *_: (0, 0))],
            out_specs=feature_major(n_kv, DIFF_DV),
            scratch_shapes=[pltpu.VMEM((KV_TILE, Q_TILE), F32),
                            pltpu.VMEM((2, KV_TILE, Q_TILE), F32),
                            pltpu.VMEM((2, KV_TILE, Q_TILE), F32),
                            pltpu.VMEM((2, 2, DIFF_DV + VT_PAD, Q_TILE), F32),
                            pltpu.VMEM((2, 2, 2 * LANES, Q_TILE), BF16)]),
        out_shape=jax.ShapeDtypeStruct((bsz, DIFF_HEADS, n_kv, DIFF_DV, Q_TILE), BF16),
        compiler_params=_params("parallel", "parallel"),
        name="diff_attention",
    )(slopes, jnp.full((1,), n_kv // 2 - 1, jnp.int32), qt, qkv, vt, _key_features(t),
      lam_rows, subln_g)


def _lambda_init(layer):
    return 0.8 - 0.6 * math.exp(-0.3 * layer)


def _trunk(x, p):
    bsz, t, _ = x.shape
    m = bsz * t
    xf = x.reshape(m, D_MODEL)
    for l in range(DEPTH):
        xf = _ffn_ln(xf, p["ffn1_w13"][l], p["ffn1_w2"][l], p["ln1_g"][l], p["ln1_b"][l])
        qkv, gate, qt, vt = _inproj(xf, p["w_in_fm"][l], p["w_in_tok"][l], bsz, t)
        qkv = qkv.reshape(bsz, t, TOK_WIDTH)
        ret = _retention(qkv, gate.reshape(bsz, t, RET_WIDTH), p["ret_dec"][l],
                         p["ret_gn_g"][l], p["ret_gn_b"][l])
        dif = _diff_attention(qkv, qt, vt, p["slopes"], p["lam_rows"][l],
                              p["diff_subln_g"][l], _lambda_init(l))
        xf = _ffn_ln(xf, p["ffn2_w13"][l], p["ffn2_w2"][l], p["ln3_g"][l], p["ln3_b"][l],
                     mix=(ret.reshape(m, RET_WIDTH), dif,
                          p["w_out"][l], p["ln2_g"][l], p["ln2_b"][l]))
    return xf.reshape(bsz, t, D_MODEL)


def kernel(x_prompt, x_sample, w_in, w_out, ret_decay_f, ret_decay_b, ret_gn_g, ret_gn_b,
           diff_lq1, diff_lk1, diff_lq2, diff_lk2, diff_subln_g,
           ffn1_w13, ffn1_w2, ffn2_w13, ffn2_w2,
           ln1_g, ln1_b, ln2_g, ln2_b, ln3_g, ln3_b):
    n_pairs = RET_WIDTH // LANES

    def pair_rows(dec):
        return jnp.broadcast_to(dec.reshape(DEPTH, n_pairs, 2, 1), (DEPTH, n_pairs, 2, LANES))

    row = lambda a: a.reshape(DEPTH, 1, -1)
    lam_rows = jnp.stack([diff_lq1, diff_lk1, diff_lq2, diff_lk2], axis=1)
    lam_rows = jnp.pad(lam_rows, ((0, 0), (0, 4), (0, LANES - DIFF_DH)))
    p = {
        "w_in_fm": jnp.concatenate([w_in[..., DQ_COL:DK_COL], w_in[..., DV_COL:]], axis=-1).astype(BF16),
        "w_in_tok": jnp.concatenate([w_in[..., :DQ_COL], w_in[..., DK_COL:DV_COL]], axis=-1).astype(BF16),
        "w_out": w_out.astype(BF16),
        "ffn1_w13": ffn1_w13.astype(BF16), "ffn1_w2": ffn1_w2.astype(BF16),
        "ffn2_w13": ffn2_w13.astype(BF16), "ffn2_w2": ffn2_w2.astype(BF16),
        "ret_dec": jnp.concatenate([pair_rows(ret_decay_f), pair_rows(ret_decay_b)], axis=2),
        "ret_gn_g": row(ret_gn_g), "ret_gn_b": row(ret_gn_b),
        "lam_rows": lam_rows, "diff_subln_g": diff_subln_g.reshape(DEPTH, DIFF_DV, 1),
        "slopes": jnp.asarray([2.0 ** (-8.0 * (h + 1) / DIFF_HEADS) for h in range(DIFF_HEADS)], F32),
        "ln1_g": row(ln1_g), "ln1_b": row(ln1_b), "ln2_g": row(ln2_g), "ln2_b": row(ln2_b),
        "ln3_g": row(ln3_g), "ln3_b": row(ln3_b),
    }
    return _trunk(x_prompt, p), _trunk(x_sample, p)
```
